```python
import math
import jax
import jax.numpy as jnp
from jax import lax
import numpy as np


D_MODEL = 1024
BATCH = 4
SEQ = 4096
DEPTH = 4

D_MIX = D_MODEL
FOX_HEADS = 8
FOX_HEAD_DIM = 64
FOX_WIDTH = FOX_HEADS * FOX_HEAD_DIM
Q_BLOCK = 128
S5_WIDTH = D_MIX - FOX_WIDTH
S5_GROUP = 16
S5_GROUPS = S5_WIDTH // S5_GROUP
S5_STATE = 64
EVEN_IN = 3 * FOX_WIDTH + FOX_HEADS + S5_WIDTH
MLSTM_HEADS = 8
MLSTM_HEAD_DIM = D_MIX // MLSTM_HEADS
MLSTM_CHUNK = 64
CONV_WIDTH = 4
ODD_IN = 4 * D_MIX + 2 * MLSTM_HEADS
N_GROUPS = 4
EXPERTS_PER_GROUP = 4
N_EXPERTS = N_GROUPS * EXPERTS_PER_GROUP
EXPERT_TOP_K = 2
D_EXPERT = 256
DN_ALPHA = (2 * DEPTH) ** 0.25
DN_BETA = (8 * DEPTH) ** -0.25
N_EVEN = (DEPTH + 1) // 2
N_ODD = DEPTH // 2
LN_EPS = 1e-5

kernel_name = 'fox_s5_mlstm_hmoe_deepnorm_trunk'


def layer_norm(x, g, b):
    xf = x.astype(jnp.float32)
    mu = jnp.mean(xf, axis=-1, keepdims=True)
    var = jnp.mean(jnp.square(xf - mu), axis=-1, keepdims=True)
    y = (xf - mu) * lax.rsqrt(var + LN_EPS) * g.astype(jnp.float32) + b.astype(jnp.float32)
    return y.astype(x.dtype)


def forgetting_attention(q, k, v, f_logit):
    B, S, H, Dh = q.shape
    nb = S // Q_BLOCK
    c = jnp.cumsum(jax.nn.log_sigmoid(f_logit.astype(jnp.float32)), axis=1).transpose(0, 2, 1)
    qh = q.transpose(0, 2, 1, 3)
    kh = k.transpose(0, 2, 1, 3)
    vh = v.transpose(0, 2, 1, 3)
    q_blocks = qh.reshape(B, H, nb, Q_BLOCK, Dh).transpose(2, 0, 1, 3, 4)
    c_blocks = c.reshape(B, H, nb, Q_BLOCK).transpose(2, 0, 1, 3)
    pos_k = jnp.arange(S)
    scale = Dh ** -0.5

    def one_block(args):
        qb, cb, i = args
        s = jnp.einsum('bhqd,bhkd->bhqk', qb, kh).astype(jnp.float32) * scale
        s = s + cb[..., :, None] - c[..., None, :]
        pos_q = i * Q_BLOCK + jnp.arange(Q_BLOCK)
        mask = pos_k[None, :] <= pos_q[:, None]
        s = jnp.where(mask, s, -jnp.inf)
        p = jax.nn.softmax(s, axis=-1).astype(vh.dtype)
        return jnp.einsum('bhqk,bhkd->bhqd', p, vh)

    out = lax.map(one_block, (q_blocks, c_blocks, jnp.arange(nb)))
    return out.transpose(1, 0, 3, 2, 4).reshape(B, S, H, Dh)


def s5_scan(u, a_re, a_im, log_dt, b_re, b_im, c_re, c_im, d_skip):
    dt = jnp.exp(log_dt)[:, None]
    mag = jnp.exp(a_re * dt)
    lb_re = mag * jnp.cos(a_im * dt)
    lb_im = mag * jnp.sin(a_im * dt)
    num_re = lb_re - 1.0
    num_im = lb_im
    den = a_re * a_re + a_im * a_im
    z_re = (num_re * a_re + num_im * a_im) / den
    z_im = (num_im * a_re - num_re * a_im) / den
    bb_re = z_re[..., None] * b_re - z_im[..., None] * b_im
    bb_im = z_re[..., None] * b_im + z_im[..., None] * b_re
    bu_re = jnp.einsum('bsgc,gpc->bsgp', u, bb_re)
    bu_im = jnp.einsum('bsgc,gpc->bsgp', u, bb_im)
    a_re_t = jnp.broadcast_to(lb_re, bu_re.shape)
    a_im_t = jnp.broadcast_to(lb_im, bu_im.shape)

    def combine(e1, e2):
        a1r, a1i, b1r, b1i = e1
        a2r, a2i, b2r, b2i = e2
        return (a2r * a1r - a2i * a1i,
                a2r * a1i + a2i * a1r,
                a2r * b1r - a2i * b1i + b2r,
                a2r * b1i + a2i * b1r + b2i)

    _, _, h_re, h_im = lax.associative_scan(combine, (a_re_t, a_im_t, bu_re, bu_im), axis=1)
    y = jnp.einsum('bsgp,gcp->bsgc', h_re, c_re) - jnp.einsum('bsgp,gcp->bsgc', h_im, c_im)
    return y + d_skip * u


def causal_dwconv(x, w, b):
    C = x.shape[-1]
    y = lax.conv_general_dilated(x, w[:, None, :], window_strides=(1,), padding=[(CONV_WIDTH - 1, 0)],
                                 dimension_numbers=('NWC', 'WIO', 'NWC'), feature_group_count=C)
    return y + b


def mlstm_chunkwise(q, k, v, i_pre, f_pre):
    B, S, H, Dh = q.shape
    L = MLSTM_CHUNK
    NC = S // L
    f32 = jnp.float32

    def to_chunks(t):
        return t.reshape(B, NC, L, H, Dh).transpose(0, 3, 1, 2, 4)

    qc = to_chunks(q)
    kc = to_chunks(k * (Dh ** -0.5))
    vc = to_chunks(v)
    log_i = i_pre.astype(f32).reshape(B, NC, L, H).transpose(0, 3, 1, 2)
    log_f = jax.nn.log_sigmoid(f_pre.astype(f32)).reshape(B, NC, L, H).transpose(0, 3, 1, 2)
    bcum = jnp.cumsum(log_f, axis=-1)
    b_last = bcum[..., -1]
    g = b_last[..., None] - bcum + log_i
    m_loc = jnp.max(g, axis=-1)

    def step(carry, inp):
        Cm, nv, m = carry
        k_n, v_n, g_n, bl_n, ml_n = inp
        m_new = jnp.maximum(bl_n + m, ml_n)
        decay = jnp.exp(bl_n + m - m_new)
        w = jnp.exp(g_n - m_new[..., None])
        C_new = decay[..., None, None] * Cm + jnp.einsum('bhld,bhle->bhde', v_n * w[..., None], k_n)
        n_new = decay[..., None] * nv + jnp.einsum('bhl,bhle->bhe', w, k_n)
        return (C_new, n_new, m_new), (Cm, nv, m)

    init = (jnp.zeros((B, H, Dh, Dh), f32), jnp.zeros((B, H, Dh), f32), jnp.zeros((B, H), f32))
    xs = (kc.transpose(2, 0, 1, 3, 4), vc.transpose(2, 0, 1, 3, 4), g.transpose(2, 0, 1, 3),
          b_last.transpose(2, 0, 1), m_loc.transpose(2, 0, 1))
    _, (C_prev, n_prev, m_prev) = lax.scan(step, init, xs)
    C_prev = C_prev.transpose(1, 2, 0, 3, 4)
    n_prev = n_prev.transpose(1, 2, 0, 3)
    m_prev = m_prev.transpose(1, 2, 0)

    causal = jnp.tril(jnp.ones((L, L), dtype=bool))
    log_D = bcum[..., :, None] - bcum[..., None, :] + log_i[..., None, :]
    log_D = jnp.where(causal, log_D, -jnp.inf)
    m_inter = bcum + m_prev[..., None]
    m_t = jnp.maximum(m_inter, jnp.max(log_D, axis=-1))
    Dmat = jnp.exp(log_D - m_t[..., None])
    s = jnp.einsum('bhnld,bhnjd->bhnlj', qc, kc) * Dmat
    inter_scale = jnp.exp(m_inter - m_t)
    num = jnp.einsum('bhnlj,bhnjd->bhnld', s, vc) + inter_scale[..., None] * jnp.einsum('bhnvk,bhnlk->bhnlv', C_prev, qc)
    den = jnp.sum(s, axis=-1) + inter_scale * jnp.einsum('bhnk,bhnlk->bhnl', n_prev, qc)
    h = num / jnp.maximum(jnp.abs(den), jnp.exp(-m_t))[..., None]
    return h.transpose(0, 2, 3, 1, 4).reshape(B, S, H, Dh).astype(v.dtype)


def fox_s5_mixer(x, w_in, f_bias, a_re, a_im, log_dt, b_re, b_im, c_re, c_im, d_skip, w_glu, b_glu, w_out):
    B, S, _ = x.shape
    z = x @ w_in
    q, k, v, f_logit, u = jnp.split(z, [FOX_WIDTH, 2 * FOX_WIDTH, 3 * FOX_WIDTH, 3 * FOX_WIDTH + FOX_HEADS], axis=-1)
    shp = (B, S, FOX_HEADS, FOX_HEAD_DIM)
    att = forgetting_attention(q.reshape(shp), k.reshape(shp), v.reshape(shp), f_logit + f_bias)
    att = att.reshape(B, S, FOX_WIDTH)
    y = s5_scan(u.reshape(B, S, S5_GROUPS, S5_GROUP), a_re, a_im, log_dt, b_re, b_im, c_re, c_im, d_skip)
    y = jax.nn.gelu(y.reshape(B, S, S5_WIDTH))
    y = y * jax.nn.sigmoid(y @ w_glu + b_glu)
    return jnp.concatenate([att, y], axis=-1) @ w_out


def mlstm_mixer(x, w_in, conv_w, conv_b, i_bias, f_bias, w_out):
    B, S, _ = x.shape
    z = x @ w_in
    qk, v, o, i_pre, f_pre = jnp.split(z, [2 * D_MIX, 3 * D_MIX, 4 * D_MIX, 4 * D_MIX + MLSTM_HEADS], axis=-1)
    qk = jax.nn.silu(causal_dwconv(qk, conv_w, conv_b))
    q, k = jnp.split(qk, 2, axis=-1)
    shp = (B, S, MLSTM_HEADS, MLSTM_HEAD_DIM)
    h = mlstm_chunkwise(q.reshape(shp), k.reshape(shp), v.reshape(shp), i_pre + i_bias, f_pre + f_bias)
    h = h.reshape(B, S, D_MIX) * jax.nn.sigmoid(o)
    return h @ w_out


def hier_moe(x, w_group, b_group, w_expert, b_expert, w_gate, w_up, w_down):
    B, S, D = x.shape
    t = x.reshape(B * S, D)
    g_prob = jax.nn.softmax((t @ w_group + b_group).astype(jnp.float32), axis=-1)
    g_val, g_idx = lax.top_k(g_prob, 1)
    e_all = jnp.einsum('td,gde->tge', t, w_expert) + b_expert
    e_logits = jnp.take_along_axis(e_all, g_idx[:, :, None], axis=1)[:, 0]
    e_prob = jax.nn.softmax(e_logits.astype(jnp.float32), axis=-1)
    e_val, e_idx = lax.top_k(e_prob, EXPERT_TOP_K)
    e_val = e_val / jnp.sum(e_val, axis=-1, keepdims=True)
    weights = g_val * e_val
    expert_id = g_idx * EXPERTS_PER_GROUP + e_idx
    combine = jnp.einsum('tk,tke->te', weights, jax.nn.one_hot(expert_id, N_EXPERTS, dtype=jnp.float32))
    hid = jax.nn.silu(jnp.einsum('td,edf->tef', t, w_gate)) * jnp.einsum('td,edf->tef', t, w_up)
    hid = hid * combine.astype(hid.dtype)[:, :, None]
    out = jnp.einsum('tef,efd->td', hid, w_down)
    return out.reshape(B, S, D)


def setup_inputs(seed: int = 0) -> dict:
    key = jax.random.key(seed)
    ks = iter(jax.random.split(key, 48))
    f32 = jnp.float32

    def nrm(shape, scale):
        return jax.random.normal(next(ks), shape, f32) * scale

    x = nrm((BATCH, SEQ, D_MODEL), 1.0)
    ln_g = 1.0 + nrm((DEPTH, 2, D_MODEL), 0.01)
    ln_b = nrm((DEPTH, 2, D_MODEL), 0.01)
    even_w_in = nrm((N_EVEN, D_MODEL, EVEN_IN), D_MODEL ** -0.5)
    fox_f_bias = jnp.linspace(2.0, 5.0, FOX_HEADS, dtype=f32)[None, :] + nrm((N_EVEN, FOX_HEADS), 0.1)
    s5_a_re = -0.5 + nrm((N_EVEN, S5_GROUPS, S5_STATE), 0.01)
    s5_a_im = jnp.pi * jnp.arange(S5_STATE, dtype=f32)[None, None, :] + nrm((N_EVEN, S5_GROUPS, S5_STATE), 0.01)
    s5_log_dt = jax.random.uniform(next(ks), (N_EVEN, S5_GROUPS), f32, minval=math.log(1e-3), maxval=math.log(1e-1))
    s5_b_re = nrm((N_EVEN, S5_GROUPS, S5_STATE, S5_GROUP), (2 * S5_GROUP) ** -0.5)
    s5_b_im = nrm((N_EVEN, S5_GROUPS, S5_STATE, S5_GROUP), (2 * S5_GROUP) ** -0.5)
    s5_c_re = nrm((N_EVEN, S5_GROUPS, S5_GROUP, S5_STATE), S5_STATE ** -0.5)
    s5_c_im = nrm((N_EVEN, S5_GROUPS, S5_GROUP, S5_STATE), S5_STATE ** -0.5)
    s5_d = nrm((N_EVEN, S5_GROUPS, S5_GROUP), 1.0)
    s5_w_glu = nrm((N_EVEN, S5_WIDTH, S5_WIDTH), S5_WIDTH ** -0.5)
    s5_b_glu = nrm((N_EVEN, S5_WIDTH), 0.01)
    even_w_out = nrm((N_EVEN, D_MIX, D_MODEL), D_MIX ** -0.5 * DN_BETA)
    odd_w_in = nrm((N_ODD, D_MODEL, ODD_IN), D_MODEL ** -0.5)
    mlstm_conv_w = nrm((N_ODD, CONV_WIDTH, 2 * D_MIX), CONV_WIDTH ** -0.5)
    mlstm_conv_b = nrm((N_ODD, 2 * D_MIX), 0.01)
    mlstm_i_bias = nrm((N_ODD, MLSTM_HEADS), 0.1)
    mlstm_f_bias = jnp.linspace(3.0, 6.0, MLSTM_HEADS, dtype=f32)[None, :] + nrm((N_ODD, MLSTM_HEADS), 0.01)
    odd_w_out = nrm((N_ODD, D_MIX, D_MODEL), D_MIX ** -0.5 * DN_BETA)
    moe_w_group = nrm((DEPTH, D_MODEL, N_GROUPS), D_MODEL ** -0.5)
    moe_b_group = nrm((DEPTH, N_GROUPS), 0.01)
    moe_w_expert = nrm((DEPTH, N_GROUPS, D_MODEL, EXPERTS_PER_GROUP), D_MODEL ** -0.5)
    moe_b_expert = nrm((DEPTH, N_GROUPS, EXPERTS_PER_GROUP), 0.01)
    moe_w_gate = nrm((DEPTH, N_EXPERTS, D_MODEL, D_EXPERT), D_MODEL ** -0.5)
    moe_w_up = nrm((DEPTH, N_EXPERTS, D_MODEL, D_EXPERT), D_MODEL ** -0.5)
    moe_w_down = nrm((DEPTH, N_EXPERTS, D_EXPERT, D_MODEL), D_EXPERT ** -0.5 * DN_BETA)
    return {'x': x, 'ln_g': ln_g, 'ln_b': ln_b,
            'even_w_in': even_w_in, 'fox_f_bias': fox_f_bias,
            's5_a_re': s5_a_re, 's5_a_im': s5_a_im, 's5_log_dt': s5_log_dt,
            's5_b_re': s5_b_re, 's5_b_im': s5_b_im, 's5_c_re': s5_c_re, 's5_c_im': s5_c_im,
            's5_d': s5_d, 's5_w_glu': s5_w_glu, 's5_b_glu': s5_b_glu, 'even_w_out': even_w_out,
            'odd_w_in': odd_w_in, 'mlstm_conv_w': mlstm_conv_w, 'mlstm_conv_b': mlstm_conv_b,
            'mlstm_i_bias': mlstm_i_bias, 'mlstm_f_bias': mlstm_f_bias, 'odd_w_out': odd_w_out,
            'moe_w_group': moe_w_group, 'moe_b_group': moe_b_group,
            'moe_w_expert': moe_w_expert, 'moe_b_expert': moe_b_expert,
            'moe_w_gate': moe_w_gate, 'moe_w_up': moe_w_up, 'moe_w_down': moe_w_down}


def reference(x, ln_g, ln_b, even_w_in, fox_f_bias, s5_a_re, s5_a_im, s5_log_dt, s5_b_re, s5_b_im,
              s5_c_re, s5_c_im, s5_d, s5_w_glu, s5_b_glu, even_w_out, odd_w_in, mlstm_conv_w,
              mlstm_conv_b, mlstm_i_bias, mlstm_f_bias, odd_w_out, moe_w_group, moe_b_group,
              moe_w_expert, moe_b_expert, moe_w_gate, moe_w_up, moe_w_down):
    h = x
    for layer in range(DEPTH):
        j = layer // 2
        if layer % 2 == 0:
            mix = fox_s5_mixer(h, even_w_in[j], fox_f_bias[j], s5_a_re[j], s5_a_im[j], s5_log_dt[j],
                               s5_b_re[j], s5_b_im[j], s5_c_re[j], s5_c_im[j], s5_d[j],
                               s5_w_glu[j], s5_b_glu[j], even_w_out[j])
        else:
            mix = mlstm_mixer(h, odd_w_in[j], mlstm_conv_w[j], mlstm_conv_b[j], mlstm_i_bias[j],
                              mlstm_f_bias[j], odd_w_out[j])
        h = layer_norm(DN_ALPHA * h + mix, ln_g[layer, 0], ln_b[layer, 0])
        ffn = hier_moe(h, moe_w_group[layer], moe_b_group[layer], moe_w_expert[layer], moe_b_expert[layer],
                       moe_w_gate[layer], moe_w_up[layer], moe_w_down[layer])
        h = layer_norm(DN_ALPHA * h + ffn, ln_g[layer, 1], ln_b[layer, 1])
    return h
```

```python
import functools

import jax
import jax.numpy as jnp
from jax import lax
from jax.experimental import pallas as pl
from jax.experimental.pallas import tpu as pltpu

F32 = jnp.float32
BF16 = jnp.bfloat16
HIGHEST = lax.Precision.HIGHEST
LN_EPS = 1e-5
NEG_INF = float("-inf")

LANES = 128
VMEM_LIMIT = 56 * 1024 * 1024

TM_PROJ = 256
ATT_BLOCK = 256
S5_CHUNK = 16
MLSTM_CHUNK = 256
TM_MOE = 256
PAIRS_LO = (0, 0, 0, 1, 1, 2)
PAIRS_HI = (1, 2, 3, 2, 3, 3)

NT_DIMS = (((1,), (1,)), ((), ()))
TN_DIMS = (((0,), (0,)), ((), ()))


def _cparams(*sem):
    return pltpu.CompilerParams(dimension_semantics=sem, vmem_limit_bytes=VMEM_LIMIT)


def _log_sigmoid(x):
    return jnp.minimum(x, 0.0) - jnp.log1p(jnp.exp(-jnp.abs(x)))


def _layer_norm(x, g, b):
    mu = jnp.mean(x, axis=-1, keepdims=True)
    xc = x - mu
    var = jnp.mean(xc * xc, axis=-1, keepdims=True)
    return xc * lax.rsqrt(var + LN_EPS) * g + b


def _iota(shape, dim):
    return lax.broadcasted_iota(jnp.int32, shape, dim)


def _proj_kernel(x_ref, w_ref, *o_refs, segs):
    xb = x_ref[...].astype(BF16)
    for o_ref, (start, width) in zip(o_refs, segs):
        o_ref[...] = jnp.dot(xb, w_ref[:, start:start + width],
                             preferred_element_type=F32).astype(o_ref.dtype)


def _proj(x, w, segs, dtypes):
    t, d = x.shape
    tm = TM_PROJ
    return pl.pallas_call(
        functools.partial(_proj_kernel, segs=segs),
        grid=(t // tm,),
        in_specs=[pl.BlockSpec((tm, d), lambda i: (i, 0)),
                  pl.BlockSpec(w.shape, lambda i: (0, 0))],
        out_specs=[pl.BlockSpec((tm, wd), lambda i: (i, 0)) for _, wd in segs],
        out_shape=[jax.ShapeDtypeStruct((t, wd), dt) for (_, wd), dt in zip(segs, dtypes)],
        compiler_params=_cparams("parallel"),
        name="proj_even",
    )(x, w)


def _proj_odd_kernel(x_ref, w_ref, cw_ref, cb_ref, q_ref, k_ref, v_ref, o_ref, g_ref, zs_ref,
                     *, tm, dmix, k_scale, tiles_per_seq, conv_width):
    i = pl.program_id(0)
    xb = x_ref[...].astype(BF16)

    @pl.when(i % tiles_per_seq == 0)
    def _():
        zs_ref[0:8, :] = jnp.zeros((8, 2 * dmix), F32)

    cw = 512
    for c0 in range(0, 2 * dmix, cw):
        zs_ref[8:tm + 8, c0:c0 + cw] = jnp.dot(xb, w_ref[:, c0:c0 + cw], preferred_element_type=F32)
    for c0 in range(0, 2 * dmix, cw):
        acc = jnp.broadcast_to(cb_ref[:, c0:c0 + cw], (tm, cw))
        for j in range(conv_width):
            acc = acc + cw_ref[j:j + 1, c0:c0 + cw] * zs_ref[pl.ds(8 - (conv_width - 1) + j, tm), c0:c0 + cw]
        y = acc * jax.nn.sigmoid(acc)
        if c0 < dmix:
            q_ref[:, c0:c0 + cw] = y.astype(BF16)
        else:
            k_ref[:, c0 - dmix:c0 - dmix + cw] = (y * k_scale).astype(BF16)
    zs_ref[0:8, :] = zs_ref[tm:tm + 8, :]

    v_ref[...] = jnp.dot(xb, w_ref[:, 2 * dmix:3 * dmix], preferred_element_type=F32).astype(BF16)
    o_ref[...] = jnp.dot(xb, w_ref[:, 3 * dmix:4 * dmix], preferred_element_type=F32)
    g_ref[...] = jnp.dot(xb, w_ref[:, 4 * dmix:4 * dmix + LANES], preferred_element_type=F32)


def _proj_odd(x, w, conv_w, conv_b, seq, head_dim):
    t, d = x.shape
    dmix = conv_w.shape[1] // 2
    tm = TM_PROJ
    kern = functools.partial(_proj_odd_kernel, tm=tm, dmix=dmix, k_scale=head_dim ** -0.5,
                             tiles_per_seq=seq // tm, conv_width=conv_w.shape[0])
    row = lambda i: (i, 0)
    fixed = lambda i: (0, 0)
    return pl.pallas_call(
        kern,
        grid=(t // tm,),
        in_specs=[pl.BlockSpec((tm, d), row), pl.BlockSpec(w.shape, fixed),
                  pl.BlockSpec(conv_w.shape, fixed), pl.BlockSpec((1, 2 * dmix), fixed)],
        out_specs=[pl.BlockSpec((tm, dmix), row)] * 4 + [pl.BlockSpec((tm, LANES), row)],
        out_shape=[jax.ShapeDtypeStruct((t, dmix), BF16)] * 3
        + [jax.ShapeDtypeStruct((t, dmix), F32), jax.ShapeDtypeStruct((t, LANES), F32)],
        scratch_shapes=[pltpu.VMEM((tm + 8, 2 * dmix), F32)],
        compiler_params=_cparams("arbitrary"),
        name="proj_odd",
    )(x, w, conv_w, conv_b.reshape(1, -1))


def _fox_kernel(fb_ref, f_ref, q_ref, k_ref, v_ref, o_ref, c_ref, acc_ref, *, blk, seq, dh):
    qi = pl.program_id(2)

    @pl.when(qi == 0)
    def _():
        tri = (_iota((blk, blk), 0) <= _iota((blk, blk), 1)).astype(F32)
        carry = jnp.zeros((2, 1), F32)
        for j in range(seq // blk):
            ls = _log_sigmoid(f_ref[0, 0, :, j * blk:(j + 1) * blk] + fb_ref[0])
            cs = jnp.dot(ls, tri, precision=HIGHEST, preferred_element_type=F32) + carry
            c_ref[:, j * blk:(j + 1) * blk] = cs
            carry = cs[:, blk - 1:blk]

    lane = _iota((blk, LANES), 1)
    q = q_ref[...]
    zero = jnp.zeros_like(q)
    q_heads = (jnp.where(lane < dh, q, zero), jnp.where(lane >= dh, q, zero))
    qstart = pl.multiple_of(qi * blk, blk)
    eye = (_iota((blk, blk), 0) == _iota((blk, blk), 1)).astype(F32)
    c_col = lax.dot_general(eye, c_ref[:, pl.ds(qstart, blk)], NT_DIMS,
                            precision=HIGHEST, preferred_element_type=F32)
    causal = _iota((blk, blk), 1) <= _iota((blk, blk), 0)
    acc_ref[...] = jnp.zeros_like(acc_ref)

    def block(kb, carry, masked):
        kstart = pl.multiple_of(kb * blk, blk)
        kblk = k_ref[pl.ds(kstart, blk), :]
        vblk = v_ref[pl.ds(kstart, blk), :]
        c_row = c_ref[:, pl.ds(kstart, blk)]
        out = []
        for h in range(2):
            m, l = carry[2 * h], carry[2 * h + 1]
            s = lax.dot_general(q_heads[h], kblk, NT_DIMS, preferred_element_type=F32)
            s = s + (c_col[:, h:h + 1] - c_row[h:h + 1, :])
            if masked:
                s = jnp.where(causal, s, NEG_INF)
            m_new = jnp.maximum(m, jnp.max(s, axis=1, keepdims=True))
            alpha = jnp.exp(m - m_new)
            p = jnp.exp(s - m_new)
            l_new = alpha * l + jnp.sum(p, axis=1, keepdims=True)
            acc_ref[h] = alpha * acc_ref[h] + jnp.dot(p.astype(BF16), vblk, preferred_element_type=F32)
            out += [m_new, l_new]
        return tuple(out)

    m0 = jnp.full((blk, 1), NEG_INF, F32)
    l0 = jnp.zeros((blk, 1), F32)
    carry = lax.fori_loop(0, qi, lambda kb, c: block(kb, c, False), (m0, l0, m0, l0))
    carry = block(qi, carry, True)
    o_ref[...] = jnp.where(lane < dh, acc_ref[0] / carry[1], acc_ref[1] / carry[3]).astype(o_ref.dtype)


def _fox_attention(qkv, f_rows, f_bias, batch, seq, heads, dh):
    t = qkv.shape[0]
    blk = ATT_BLOCK
    nq = seq // blk
    hp = heads * dh // LANES
    kern = functools.partial(_fox_kernel, blk=blk, seq=seq, dh=dh)
    return pl.pallas_call(
        kern,
        grid=(batch, hp, nq),
        in_specs=[pl.BlockSpec((1, 2, 1), lambda b, p, i: (p, 0, 0)),
                  pl.BlockSpec((1, 1, 2, seq), lambda b, p, i: (b, p, 0, 0)),
                  pl.BlockSpec((blk, LANES), lambda b, p, i: (b * nq + i, p)),
                  pl.BlockSpec((seq, LANES), lambda b, p, i: (b, hp + p)),
                  pl.BlockSpec((seq, LANES), lambda b, p, i: (b, 2 * hp + p))],
        out_specs=pl.BlockSpec((blk, LANES), lambda b, p, i: (b * nq + i, p)),
        out_shape=jax.ShapeDtypeStruct((t, heads * dh), BF16),
        scratch_shapes=[pltpu.VMEM((2, seq), F32), pltpu.VMEM((2, blk, LANES), F32)],
        compiler_params=_cparams("parallel", "parallel", "arbitrary"),
        name="fox_attention",
    )(f_bias.reshape(hp, 2, 1), f_rows, qkv, qkv, qkv)


def _s5_tables(a_re, a_im, log_dt, b_re, b_im, c_re, c_im, d_skip):
    L = S5_CHUNK
    g, p = a_re.shape
    hc = b_re.shape[-1]
    dt = jnp.exp(log_dt)[:, None]
    mag = jnp.exp(a_re * dt)
    lb_re = mag * jnp.cos(a_im * dt)
    lb_im = mag * jnp.sin(a_im * dt)
    num_re = lb_re - 1.0
    num_im = lb_im
    den = a_re * a_re + a_im * a_im
    z_re = (num_re * a_re + num_im * a_im) / den
    z_im = (num_im * a_re - num_re * a_im) / den
    bb_re = z_re[..., None] * b_re - z_im[..., None] * b_im
    bb_im = z_re[..., None] * b_im + z_im[..., None] * b_re
    tau = jnp.arange(L + 1, dtype=F32)
    pmag = jnp.exp((a_re * dt)[..., None] * tau)
    pw_re = pmag * jnp.cos((a_im * dt)[..., None] * tau)
    pw_im = pmag * jnp.sin((a_im * dt)[..., None] * tau)
    cp_re = c_re[..., None] * pw_re[:, None] - c_im[..., None] * pw_im[:, None]
    cp_im = c_re[..., None] * pw_im[:, None] + c_im[..., None] * pw_re[:, None]
    kern = (jnp.einsum("gopt,gpi->gtoi", cp_re[..., :L], bb_re, precision=HIGHEST)
            - jnp.einsum("gopt,gpi->gtoi", cp_im[..., :L], bb_im, precision=HIGHEST))
    kern = kern.at[:, 0].add(d_skip[:, :, None] * jnp.eye(hc, dtype=F32))
    kern = jnp.concatenate([kern, jnp.zeros((g, 1, hc, hc), F32)], axis=1)
    s_idx = jnp.arange(L)[:, None]
    t_idx = jnp.arange(L)[None, :]
    lag = jnp.where(t_idx >= s_idx, t_idx - s_idx, L)
    toep = kern[:, lag]
    toep = toep.transpose(0, 1, 4, 2, 3).reshape(g, L * hc, L * hc)
    rev = (L - 1) - jnp.arange(L)
    e_re = pw_re[:, :, rev][..., None] * bb_re[:, :, None] - pw_im[:, :, rev][..., None] * bb_im[:, :, None]
    e_im = pw_re[:, :, rev][..., None] * bb_im[:, :, None] + pw_im[:, :, rev][..., None] * bb_re[:, :, None]
    e_re = e_re.transpose(0, 2, 3, 1).reshape(g, L * hc, p)
    e_im = e_im.transpose(0, 2, 3, 1).reshape(g, L * hc, p)
    o_re = cp_re[..., 1:].transpose(0, 2, 3, 1).reshape(g, p, L * hc)
    o_im = -cp_im[..., 1:].transpose(0, 2, 3, 1).reshape(g, p, L * hc)

    def pair_diag(m):
        m = m.reshape(g // 2, 2, *m.shape[1:])
        z = jnp.zeros_like(m[:, 0])
        top = jnp.concatenate([m[:, 0], z], axis=2)
        bot = jnp.concatenate([z, m[:, 1]], axis=2)
        return jnp.concatenate([top, bot], axis=1)

    toep_p = pair_diag(toep)
    w_end = jnp.concatenate([pair_diag(e_re), pair_diag(e_im)], axis=2)
    w_out = jnp.concatenate([pair_diag(o_re), pair_diag(o_im)], axis=1)
    lam_re = pw_re[..., L].reshape(g // 2, 2 * p)
    lam_im = pw_im[..., L].reshape(g // 2, 2 * p)
    return toep_p.astype(BF16), w_end.astype(BF16), w_out.astype(BF16), lam_re, lam_im


def _s5_end_kernel(u_ref, w_ref, ere_ref, eim_ref):
    e = jnp.dot(u_ref[0], w_ref[0], preferred_element_type=F32)
    half = e.shape[1] // 2
    ere_ref[0] = e[:, :half]
    eim_ref[0] = e[:, half:]


def _s5_scan_kernel(lre_ref, lim_ref, ere_ref, eim_ref, hre_ref, him_ref, *, batch, nchunk, rows):
    lr = lre_ref[...]
    li = lim_ref[...]
    npair = lr.shape[0]

    def step(j, carry):
        out = []
        for b in range(batch):
            hr, hi = carry[2 * b], carry[2 * b + 1]
            sl = pl.ds(b * nchunk + j, npair, stride=rows)
            hre_ref[sl, :] = hr
            him_ref[sl, :] = hi
            er = ere_ref[sl, :]
            ei = eim_ref[sl, :]
            out += [lr * hr - li * hi + er, lr * hi + li * hr + ei]
        return tuple(out)

    z = jnp.zeros_like(lr)
    lax.fori_loop(0, nchunk, step, (z,) * (2 * batch))


def _s5_out_kernel(u_ref, t_ref, hre_ref, him_ref, w_ref, y_ref):
    half = w_ref.shape[1] // 2
    y = jnp.dot(u_ref[0], t_ref[0], preferred_element_type=F32)
    y = y + jnp.dot(hre_ref[0].astype(BF16), w_ref[0, :half, :], preferred_element_type=F32)
    y = y + jnp.dot(him_ref[0].astype(BF16), w_ref[0, half:, :], preferred_element_type=F32)
    y_ref[0] = jax.nn.gelu(y)


def _s5(u, tables, batch, seq, groups, hc):
    toep, w_end, w_out, lam_re, lam_im = tables
    L = S5_CHUNK
    t = u.shape[0]
    rows = t // L
    nchunk = seq // L
    npair = groups // 2
    kc = 2 * L * hc
    pw = lam_re.shape[1]
    ug = u.reshape(rows, L, npair, 2, hc).transpose(2, 0, 3, 1, 4).reshape(npair, rows, kc)
    blk3 = lambda a, b: pl.BlockSpec((1, a, b), lambda i: (i, 0, 0))
    e_re, e_im = pl.pallas_call(
        _s5_end_kernel,
        grid=(npair,),
        in_specs=[blk3(rows, kc), blk3(kc, 2 * pw)],
        out_specs=[blk3(rows, pw)] * 2,
        out_shape=[jax.ShapeDtypeStruct((npair, rows, pw), F32)] * 2,
        compiler_params=_cparams("parallel"),
        name="s5_chunk_end",
    )(ug, w_end)
    pg = 8
    lam_spec = pl.BlockSpec((pg, pw), lambda i: (i, 0))
    st_spec = pl.BlockSpec((pg * rows, pw), lambda i: (i, 0))
    h_re, h_im = pl.pallas_call(
        functools.partial(_s5_scan_kernel, batch=batch, nchunk=nchunk, rows=rows),
        grid=(npair // pg,),
        in_specs=[lam_spec, lam_spec, st_spec, st_spec],
        out_specs=[st_spec] * 2,
        out_shape=[jax.ShapeDtypeStruct((npair * rows, pw), F32)] * 2,
        compiler_params=_cparams("parallel"),
        name="s5_chunk_scan",
    )(lam_re, lam_im, e_re.reshape(npair * rows, pw), e_im.reshape(npair * rows, pw))
    y = pl.pallas_call(
        _s5_out_kernel,
        grid=(npair,),
        in_specs=[blk3(rows, kc), blk3(kc, kc), blk3(rows, pw), blk3(rows, pw), blk3(2 * pw, kc)],
        out_specs=blk3(rows, kc),
        out_shape=jax.ShapeDtypeStruct((npair, rows, kc), F32),
        compiler_params=_cparams("parallel"),
        name="s5_out",
    )(ug, toep, h_re.reshape(npair, rows, pw), h_im.reshape(npair, rows, pw), w_out)
    return y.reshape(npair, rows, 2, L, hc).transpose(1, 3, 0, 2, 4).reshape(t, groups * hc)


def _mlstm_kernel(gb_ref, g_ref, q_ref, k_ref, v_ref, o_ref, h_ref, c_ref, *, L, seq, dh):
    tri = (_iota((L, L), 0) <= _iota((L, L), 1)).astype(F32)
    eye = (_iota((L, L), 0) == _iota((L, L), 1)).astype(F32)
    causal = _iota((L, L), 1) <= _iota((L, L), 0)
    lane = _iota((L, LANES), 1)
    one_col = (lane == 0).astype(BF16)
    c_ref[...] = jnp.zeros_like(c_ref)

    def chunk(c, m_prev):
        st = pl.multiple_of(c * L, L)
        gates = g_ref[0, 0, :, pl.ds(st, L)] + gb_ref[0]
        li_row = gates[0:1]
        b_row = jnp.dot(_log_sigmoid(gates[1:2]), tri, precision=HIGHEST, preferred_element_type=F32)
        sub = _iota((8, L), 0)
        rows = jnp.where(sub == 0, b_row, jnp.where(sub == 1, li_row, 0.0))
        cols = lax.dot_general(eye, rows, NT_DIMS, precision=HIGHEST, preferred_element_type=F32)
        b_col, li_col = cols[:, 0:1], cols[:, 1:2]
        b_last = b_row[:, L - 1:L]
        q = q_ref[pl.ds(st, L), :]
        k = k_ref[pl.ds(st, L), :]
        v = v_ref[pl.ds(st, L), :]

        log_d = jnp.where(causal, b_col - b_row + li_row, NEG_INF)
        m_inter = b_col + m_prev
        m_t = jnp.maximum(m_inter, jnp.max(log_d, axis=1, keepdims=True))
        s = lax.dot_general(q, k, NT_DIMS, preferred_element_type=F32) * jnp.exp(log_d - m_t)
        v_aug = jnp.concatenate([v, one_col], axis=1)
        tot = jnp.dot(s.astype(BF16), v_aug, preferred_element_type=F32)
        tot = tot + jnp.exp(m_inter - m_t) * jnp.dot(q, c_ref[...].astype(BF16), preferred_element_type=F32)
        den = jnp.maximum(jnp.abs(tot[:, dh:dh + 1]), jnp.exp(-m_t))
        h = tot[:, :dh] / den
        h_ref[pl.ds(st, L), :] = (h * jax.nn.sigmoid(o_ref[pl.ds(st, L), :])).astype(h_ref.dtype)

        g_row = b_last - b_row + li_row
        m_new = jnp.maximum(b_last + m_prev, jnp.max(g_row, axis=1, keepdims=True))
        w_col = jnp.exp(b_last - b_col + li_col - m_new)
        vw = jnp.concatenate([(v.astype(F32) * w_col).astype(BF16),
                              jnp.where(lane == 0, w_col, 0.0).astype(BF16)], axis=1)
        c_ref[...] = (jnp.exp(b_last + m_prev - m_new) * c_ref[...]
                      + lax.dot_general(k, vw, TN_DIMS, preferred_element_type=F32))
        return m_new

    lax.fori_loop(0, seq // L, chunk, jnp.zeros((1, 1), F32))


def _mlstm(q, k, v, o, gate_rows, gate_bias, batch, seq, heads, dh):
    t = q.shape[0]
    L = MLSTM_CHUNK
    col = pl.BlockSpec((seq, dh), lambda b, h: (b, h))
    return pl.pallas_call(
        functools.partial(_mlstm_kernel, L=L, seq=seq, dh=dh),
        grid=(batch, heads),
        in_specs=[pl.BlockSpec((1, 8, 1), lambda b, h: (h, 0, 0)),
                  pl.BlockSpec((1, 1, 8, seq), lambda b, h: (b, h, 0, 0)),
                  col, col, col, col],
        out_specs=col,
        out_shape=jax.ShapeDtypeStruct((t, heads * dh), BF16),
        scratch_shapes=[pltpu.VMEM((dh, 2 * dh), F32)],
        compiler_params=_cparams("parallel", "parallel"),
        name="mlstm",
    )(gate_bias, gate_rows, q, k, v, o)


def _out_even_kernel(h_ref, att_ref, ys_ref, wg_ref, bg_ref, wo_ref, g_ref, b_ref, o_ref, *, alpha):
    ys = ys_ref[...]
    half = att_ref.shape[1]
    gate = jax.nn.sigmoid(jnp.dot(ys.astype(BF16), wg_ref[...], preferred_element_type=F32) + bg_ref[...])
    mix = jnp.dot(att_ref[...], wo_ref[:half, :], preferred_element_type=F32)
    mix = mix + jnp.dot((ys * gate).astype(BF16), wo_ref[half:, :], preferred_element_type=F32)
    o_ref[...] = _layer_norm(alpha * h_ref[...] + mix, g_ref[...], b_ref[...])


def _out_odd_kernel(h_ref, hm_ref, wo_ref, g_ref, b_ref, o_ref, *, alpha):
    mix = jnp.dot(hm_ref[...], wo_ref[...], preferred_element_type=F32)
    o_ref[...] = _layer_norm(alpha * h_ref[...] + mix, g_ref[...], b_ref[...])


def _row_tiled_call(kern, row_args, fixed_args, out_dtype, name):
    t = row_args[0].shape[0]
    tm = TM_PROJ
    in_specs = [pl.BlockSpec((tm, a.shape[1]), lambda i: (i, 0)) for a in row_args]
    in_specs += [pl.BlockSpec(a.shape, lambda i: (0, 0)) for a in fixed_args]
    d = row_args[0].shape[1]
    return pl.pallas_call(
        kern,
        grid=(t // tm,),
        in_specs=in_specs,
        out_specs=pl.BlockSpec((tm, d), lambda i: (i, 0)),
        out_shape=jax.ShapeDtypeStruct((t, d), out_dtype),
        compiler_params=_cparams("parallel"),
        name=name,
    )(*row_args, *fixed_args)


def _residual_ln_kernel(h_ref, f_ref, g_ref, b_ref, o_ref, *, alpha):
    o_ref[...] = _layer_norm(alpha * h_ref[...] + f_ref[...], g_ref[...], b_ref[...])


def _router_kernel(h_ref, w_ref, b_ref, o_ref, hb_ref, *, n_groups, epg):
    h = h_ref[...]
    hb_ref[...] = h.astype(BF16)
    logits = jnp.dot(h, w_ref[...], precision=HIGHEST, preferred_element_type=F32) + b_ref[...]
    lane = _iota(logits.shape, 1)
    big = jnp.int32(LANES)
    lg = jnp.where(lane < n_groups, logits, NEG_INF)
    mg = jnp.max(lg, axis=1, keepdims=True)
    g_val = 1.0 / jnp.sum(jnp.exp(lg - mg), axis=1, keepdims=True)
    g_idx = jnp.min(jnp.where(lg == mg, lane, big), axis=1, keepdims=True)
    lo_lane = n_groups + g_idx * epg
    le = jnp.where((lane >= lo_lane) & (lane < lo_lane + epg), logits, NEG_INF)
    m1 = jnp.max(le, axis=1, keepdims=True)
    i1 = jnp.min(jnp.where(le == m1, lane, big), axis=1, keepdims=True)
    le2 = jnp.where(lane == i1, NEG_INF, le)
    m2 = jnp.max(le2, axis=1, keepdims=True)
    i2 = jnp.min(jnp.where(le2 == m2, lane, big), axis=1, keepdims=True)
    r = jnp.exp(m2 - m1)
    w1 = g_val / (1.0 + r)
    w2 = g_val * r / (1.0 + r)
    e1 = i1 - lo_lane
    e2 = i2 - lo_lane
    first_lo = e1 < e2
    lo = jnp.where(first_lo, e1, e2)
    hi = jnp.where(first_lo, e2, e1)
    w_lo = jnp.where(first_lo, w1, w2)
    w_hi = jnp.where(first_lo, w2, w1)
    pair = (lo * (2 * epg - 1 - lo)) // 2 + (hi - lo - 1)
    cls = (g_idx * (epg * (epg - 1) // 2) + pair).astype(F32)
    out = jnp.where(lane == 0, cls, jnp.where(lane == 1, w_lo, jnp.where(lane == 2, w_hi, 0.0)))
    o_ref[...] = out[:, :o_ref.shape[1]]


def _router(h, w_r, b_r, n_groups, epg):
    t, d = h.shape
    tm = TM_PROJ
    return pl.pallas_call(
        functools.partial(_router_kernel, n_groups=n_groups, epg=epg),
        grid=(t // tm,),
        in_specs=[pl.BlockSpec((tm, d), lambda i: (i, 0)),
                  pl.BlockSpec(w_r.shape, lambda i: (0, 0)),
                  pl.BlockSpec(b_r.shape, lambda i: (0, 0))],
        out_specs=[pl.BlockSpec((tm, 8), lambda i: (i, 0)), pl.BlockSpec((tm, d), lambda i: (i, 0))],
        out_shape=[jax.ShapeDtypeStruct((t, 8), F32), jax.ShapeDtypeStruct((t, d), BF16)],
        compiler_params=_cparams("parallel"),
        name="router",
    )(h, w_r, b_r)


def _moe_kernel(elo_ref, ehi_ref, valid_ref, x_ref, w_ref, g0_ref, u0_ref, d0_ref, g1_ref, u1_ref, d1_ref, y_ref):
    i = pl.program_id(0)

    @pl.when(valid_ref[i] == 0)
    def _():
        y_ref[...] = jnp.zeros_like(y_ref)

    @pl.when(valid_ref[i] != 0)
    def _():
        x = x_ref[...]
        w = w_ref[...]
        y = None
        for j, (g_ref, u_ref, d_ref) in enumerate(((g0_ref, u0_ref, d0_ref), (g1_ref, u1_ref, d1_ref))):
            gate = jnp.dot(x, g_ref[0], preferred_element_type=F32)
            up = jnp.dot(x, u_ref[0], preferred_element_type=F32)
            hid = gate * jax.nn.sigmoid(gate) * up * w[:, j:j + 1]
            part = jnp.dot(hid.astype(BF16), d_ref[0], preferred_element_type=F32)
            y = part if y is None else y + part
        y_ref[...] = y


def _moe_experts(x_sorted, w_sorted, e_lo, e_hi, valid, w_gate, w_up, w_down):
    p, d = x_sorted.shape
    tm = TM_MOE
    f = w_gate.shape[2]
    lo_in = pl.BlockSpec((1, d, f), lambda i, lo, hi, va: (lo[i], 0, 0))
    hi_in = pl.BlockSpec((1, d, f), lambda i, lo, hi, va: (hi[i], 0, 0))
    lo_dn = pl.BlockSpec((1, f, d), lambda i, lo, hi, va: (lo[i], 0, 0))
    hi_dn = pl.BlockSpec((1, f, d), lambda i, lo, hi, va: (hi[i], 0, 0))
    grid_spec = pltpu.PrefetchScalarGridSpec(
        num_scalar_prefetch=3,
        grid=(p // tm,),
        in_specs=[pl.BlockSpec((tm, d), lambda i, lo, hi, va: (i, 0)),
                  pl.BlockSpec((tm, 2), lambda i, lo, hi, va: (i, 0)),
                  lo_in, lo_in, lo_dn, hi_in, hi_in, hi_dn],
        out_specs=pl.BlockSpec((tm, d), lambda i, lo, hi, va: (i, 0)),
    )
    return pl.pallas_call(
        _moe_kernel,
        grid_spec=grid_spec,
        out_shape=jax.ShapeDtypeStruct((p, d), F32),
        compiler_params=_cparams("arbitrary"),
        name="moe_experts",
    )(e_lo, e_hi, valid, x_sorted, w_sorted, w_gate, w_up, w_down, w_gate, w_up, w_down)


def _hier_moe(h, w_group, b_group, w_expert, b_expert, w_gate, w_up, w_down):
    t, d = h.shape
    n_groups, _, epg = w_expert.shape
    npairs = epg * (epg - 1) // 2
    ncls = n_groups * npairs
    tm = TM_MOE
    w_r = jnp.concatenate([w_group, w_expert.transpose(1, 0, 2).reshape(d, n_groups * epg)], axis=1)
    b_r = jnp.concatenate([b_group, b_expert.reshape(-1)])
    pad = LANES - w_r.shape[1]
    w_r = jnp.pad(w_r, ((0, 0), (0, pad)))
    b_r = jnp.pad(b_r, (0, pad)).reshape(1, LANES)
    route, hb = _router(h, w_r, b_r, n_groups, epg)
    cls = route[:, 0].astype(jnp.int32)
    wts = route[:, 1:3]

    order = jnp.argsort(cls, stable=True).astype(jnp.int32)
    counts = jnp.sum(cls[:, None] == jnp.arange(ncls)[None, :], axis=0).astype(jnp.int32)
    padded = ((counts + tm - 1) // tm) * tm
    pend = jnp.cumsum(padded)
    pstart = pend - padded
    start = jnp.cumsum(counts) - counts
    cls_sorted = cls[order]
    pos_sorted = pstart[cls_sorted] + (jnp.arange(t, dtype=jnp.int32) - start[cls_sorted])
    p_rows = t + ncls * tm
    src = jnp.zeros((p_rows,), jnp.int32).at[pos_sorted].set(order)
    live = jnp.zeros((p_rows,), F32).at[pos_sorted].set(1.0)
    pos_of_token = jnp.zeros((t,), jnp.int32).at[order].set(pos_sorted)
    x_sorted = jnp.take(hb, src, axis=0)
    w_sorted = jnp.take(wts, src, axis=0) * live[:, None]

    tile_start = jnp.arange(p_rows // tm, dtype=jnp.int32) * tm
    tile_cls = jnp.minimum(jnp.searchsorted(pend, tile_start, side="right"), ncls - 1).astype(jnp.int32)
    valid = (tile_start < pend[-1]).astype(jnp.int32)
    lo_tab = jnp.asarray(PAIRS_LO, jnp.int32)
    hi_tab = jnp.asarray(PAIRS_HI, jnp.int32)
    e_lo = (tile_cls // npairs) * epg + lo_tab[tile_cls % npairs]
    e_hi = (tile_cls // npairs) * epg + hi_tab[tile_cls % npairs]

    y_sorted = _moe_experts(x_sorted, w_sorted, e_lo, e_hi, valid,
                            w_gate.astype(BF16), w_up.astype(BF16), w_down.astype(BF16))
    return jnp.take(y_sorted, pos_of_token, axis=0)


def _even_mixer(h, batch, seq, alpha, ln_g, ln_b, w_in, f_bias, s5_params, w_glu, b_glu, w_out):
    t, d = h.shape
    heads = f_bias.shape[0]
    groups, p_state = s5_params[0].shape
    hc = s5_params[3].shape[-1]
    s5_width = groups * hc
    fox_width = d - s5_width
    dh = fox_width // heads
    q_scale = dh ** -0.5
    w_q, w_k, w_v, w_f, w_u = jnp.split(w_in, [fox_width, 2 * fox_width, 3 * fox_width, 3 * fox_width + heads], axis=1)
    w_cat = jnp.concatenate([w_q * q_scale, w_k, w_v, w_u, jnp.pad(w_f, ((0, 0), (0, LANES - heads)))], axis=1)
    segs = ((0, 3 * fox_width), (3 * fox_width, s5_width), (3 * fox_width + s5_width, LANES))
    qkv, u, f = _proj(h, w_cat.astype(BF16), segs, (BF16, BF16, F32))
    f_rows = f[:, :heads].reshape(batch, seq, heads // 2, 2).transpose(0, 2, 3, 1)
    att = _fox_attention(qkv, f_rows, f_bias, batch, seq, heads, dh)
    ys = _s5(u, _s5_tables(*s5_params), batch, seq, groups, hc)
    kern = functools.partial(_out_even_kernel, alpha=alpha)
    return _row_tiled_call(kern, (h, att, ys),
                           (w_glu.astype(BF16), b_glu.reshape(1, -1), w_out.astype(BF16),
                            ln_g.reshape(1, -1), ln_b.reshape(1, -1)), F32, "out_even")


def _odd_mixer(h, batch, seq, alpha, ln_g, ln_b, w_in, conv_w, conv_b, i_bias, f_bias, w_out):
    t, d = h.shape
    heads = i_bias.shape[0]
    dmix = conv_w.shape[1] // 2
    dh = dmix // heads
    w_main, w_gates = w_in[:, :4 * dmix], w_in[:, 4 * dmix:]
    w_cat = jnp.concatenate([w_main, jnp.pad(w_gates, ((0, 0), (0, LANES - 2 * heads)))], axis=1)
    q, k, v, o, gates = _proj_odd(h, w_cat.astype(BF16), conv_w, conv_b, seq, dh)
    g = gates[:, :2 * heads].reshape(batch, seq, 2, heads).transpose(0, 3, 2, 1)
    gate_rows = jnp.pad(g, ((0, 0), (0, 0), (0, 6), (0, 0)))
    gate_bias = jnp.pad(jnp.stack([i_bias, f_bias], axis=1), ((0, 0), (0, 6))).reshape(heads, 8, 1)
    hm = _mlstm(q, k, v, o, gate_rows, gate_bias, batch, seq, heads, dh)
    kern = functools.partial(_out_odd_kernel, alpha=alpha)
    return _row_tiled_call(kern, (h, hm), (w_out.astype(BF16), ln_g.reshape(1, -1), ln_b.reshape(1, -1)),
                           F32, "out_odd")


def kernel(x, ln_g, ln_b, even_w_in, fox_f_bias, s5_a_re, s5_a_im, s5_log_dt, s5_b_re, s5_b_im, s5_c_re, s5_c_im, s5_d, s5_w_glu, s5_b_glu, even_w_out, odd_w_in, mlstm_conv_w, mlstm_conv_b, mlstm_i_bias, mlstm_f_bias, odd_w_out, moe_w_group, moe_b_group, moe_w_expert, moe_b_expert, moe_w_gate, moe_w_up, moe_w_down):
    batch, seq, d = x.shape
    depth = ln_g.shape[0]
    alpha = (2 * depth) ** 0.25
    h = x.reshape(batch * seq, d)
    for layer in range(depth):
        j = layer // 2
        if layer % 2 == 0:
            s5_params = (s5_a_re[j], s5_a_im[j], s5_log_dt[j], s5_b_re[j], s5_b_im[j],
                         s5_c_re[j], s5_c_im[j], s5_d[j])
            h = _even_mixer(h, batch, seq, alpha, ln_g[layer, 0], ln_b[layer, 0], even_w_in[j], fox_f_bias[j],
                            s5_params, s5_w_glu[j], s5_b_glu[j], even_w_out[j])
        else:
            h = _odd_mixer(h, batch, seq, alpha, ln_g[layer, 0], ln_b[layer, 0], odd_w_in[j], mlstm_conv_w[j],
                           mlstm_conv_b[j], mlstm_i_bias[j], mlstm_f_bias[j], odd_w_out[j])
        ffn = _hier_moe(h, moe_w_group[layer], moe_b_group[layer], moe_w_expert[layer], moe_b_expert[layer],
                        moe_w_gate[layer], moe_w_up[layer], moe_w_down[layer])
        kern = functools.partial(_residual_ln_kernel, alpha=alpha)
        h = _row_tiled_call(kern, (h, ffn), (ln_g[layer, 1].reshape(1, -1), ln_b[layer, 1].reshape(1, -1)),
                            F32, "moe_residual_ln")
    return h.reshape(batch, seq, d)
```

```python
import functools

import jax
import jax.numpy as jnp
from jax import lax
from jax.experimental import pallas as pl
from jax.experimental.pallas import tpu as pltpu

F32 = jnp.float32
BF16 = jnp.bfloat16
HIGHEST = lax.Precision.HIGHEST
LN_EPS = 1e-5
NEG_INF = float("-inf")

LANES = 128
VMEM_LIMIT = 56 * 1024 * 1024

TM_PROJ = 256
ATT_BLOCK = 256
S5_CHUNK = 16
MLSTM_CHUNK = 256
TM_MOE = 256
PAIRS_LO = (0, 0, 0, 1, 1, 2)
PAIRS_HI = (1, 2, 3, 2, 3, 3)

NT_DIMS = (((1,), (1,)), ((), ()))
TN_DIMS = (((0,), (0,)), ((), ()))


def _cparams(*sem):
    return pltpu.CompilerParams(dimension_semantics=sem, vmem_limit_bytes=VMEM_LIMIT)


def _log_sigmoid(x):
    return jnp.minimum(x, 0.0) - jnp.log1p(jnp.exp(-jnp.abs(x)))


def _layer_norm(x, g, b):
    mu = jnp.mean(x, axis=-1, keepdims=True)
    xc = x - mu
    var = jnp.mean(xc * xc, axis=-1, keepdims=True)
    return xc * lax.rsqrt(var + LN_EPS) * g + b


def _iota(shape, dim):
    return lax.broadcasted_iota(jnp.int32, shape, dim)


def _proj_kernel(x_ref, w_ref, *o_refs, segs):
    xb = x_ref[...].astype(BF16)
    for o_ref, (start, width) in zip(o_refs, segs):
        o_ref[...] = jnp.dot(xb, w_ref[:, start:start + width],
                             preferred_element_type=F32).astype(o_ref.dtype)


def _proj(x, w, segs, dtypes):
    t, d = x.shape
    tm = TM_PROJ
    return pl.pallas_call(
        functools.partial(_proj_kernel, segs=segs),
        grid=(t // tm,),
        in_specs=[pl.BlockSpec((tm, d), lambda i: (i, 0)),
                  pl.BlockSpec(w.shape, lambda i: (0, 0))],
        out_specs=[pl.BlockSpec((tm, wd), lambda i: (i, 0)) for _, wd in segs],
        out_shape=[jax.ShapeDtypeStruct((t, wd), dt) for (_, wd), dt in zip(segs, dtypes)],
        compiler_params=_cparams("parallel"),
        name="proj_even",
    )(x, w)


def _proj_odd_kernel(x_ref, w_ref, cw_ref, cb_ref, q_ref, k_ref, v_ref, o_ref, g_ref, zs_ref,
                     *, tm, dmix, k_scale, tiles_per_seq, conv_width):
    i = pl.program_id(0)
    xb = x_ref[...].astype(BF16)

    @pl.when(i % tiles_per_seq == 0)
    def _():
        zs_ref[0:8, :] = jnp.zeros((8, 2 * dmix), F32)

    cw = 512
    for c0 in range(0, 2 * dmix, cw):
        zs_ref[8:tm + 8, c0:c0 + cw] = jnp.dot(xb, w_ref[:, c0:c0 + cw], preferred_element_type=F32)
    for c0 in range(0, 2 * dmix, cw):
        acc = jnp.broadcast_to(cb_ref[:, c0:c0 + cw], (tm, cw))
        for j in range(conv_width):
            acc = acc + cw_ref[j:j + 1, c0:c0 + cw] * zs_ref[pl.ds(8 - (conv_width - 1) + j, tm), c0:c0 + cw]
        y = acc * jax.nn.sigmoid(acc)
        if c0 < dmix:
            q_ref[:, c0:c0 + cw] = y.astype(BF16)
        else:
            k_ref[:, c0 - dmix:c0 - dmix + cw] = (y * k_scale).astype(BF16)
    zs_ref[0:8, :] = zs_ref[tm:tm + 8, :]

    v_ref[...] = jnp.dot(xb, w_ref[:, 2 * dmix:3 * dmix], preferred_element_type=F32).astype(BF16)
    o_ref[...] = jnp.dot(xb, w_ref[:, 3 * dmix:4 * dmix], preferred_element_type=F32)
    g_ref[...] = jnp.dot(xb, w_ref[:, 4 * dmix:4 * dmix + LANES], preferred_element_type=F32)


def _proj_odd(x, w, conv_w, conv_b, seq, head_dim):
    t, d = x.shape
    dmix = conv_w.shape[1] // 2
    tm = TM_PROJ
    kern = functools.partial(_proj_odd_kernel, tm=tm, dmix=dmix, k_scale=head_dim ** -0.5,
                             tiles_per_seq=seq // tm, conv_width=conv_w.shape[0])
    row = lambda i: (i, 0)
    fixed = lambda i: (0, 0)
    return pl.pallas_call(
        kern,
        grid=(t // tm,),
        in_specs=[pl.BlockSpec((tm, d), row), pl.BlockSpec(w.shape, fixed),
                  pl.BlockSpec(conv_w.shape, fixed), pl.BlockSpec((1, 2 * dmix), fixed)],
        out_specs=[pl.BlockSpec((tm, dmix), row)] * 4 + [pl.BlockSpec((tm, LANES), row)],
        out_shape=[jax.ShapeDtypeStruct((t, dmix), BF16)] * 3
        + [jax.ShapeDtypeStruct((t, dmix), F32), jax.ShapeDtypeStruct((t, LANES), F32)],
        scratch_shapes=[pltpu.VMEM((tm + 8, 2 * dmix), F32)],
        compiler_params=_cparams("arbitrary"),
        name="proj_odd",
    )(x, w, conv_w, conv_b.reshape(1, -1))


def _fox_kernel(fb_ref, f_ref, q_ref, k_ref, v_ref, o_ref, c_ref, acc_ref, *, blk, seq, dh):
    qi = pl.program_id(2)

    @pl.when(qi == 0)
    def _():
        tri = (_iota((blk, blk), 0) <= _iota((blk, blk), 1)).astype(F32)
        carry = jnp.zeros((2, 1), F32)
        for j in range(seq // blk):
            ls = _log_sigmoid(f_ref[0, 0, :, j * blk:(j + 1) * blk] + fb_ref[0])
            cs = jnp.dot(ls, tri, precision=HIGHEST, preferred_element_type=F32) + carry
            c_ref[:, j * blk:(j + 1) * blk] = cs
            carry = cs[:, blk - 1:blk]

    lane = _iota((blk, LANES), 1)
    q = q_ref[...]
    zero = jnp.zeros_like(q)
    q_heads = (jnp.where(lane < dh, q, zero), jnp.where(lane >= dh, q, zero))
    qstart = pl.multiple_of(qi * blk, blk)
    eye = (_iota((blk, blk), 0) == _iota((blk, blk), 1)).astype(F32)
    c_col = lax.dot_general(eye, c_ref[:, pl.ds(qstart, blk)], NT_DIMS,
                            precision=HIGHEST, preferred_element_type=F32)
    causal = _iota((blk, blk), 1) <= _iota((blk, blk), 0)
    acc_ref[...] = jnp.zeros_like(acc_ref)

    def block(kb, carry, masked):
        kstart = pl.multiple_of(kb * blk, blk)
        kblk = k_ref[pl.ds(kstart, blk), :]
        vblk = v_ref[pl.ds(kstart, blk), :]
        c_row = c_ref[:, pl.ds(kstart, blk)]
        out = []
        for h in range(2):
            m, l = carry[2 * h], carry[2 * h + 1]
            s = lax.dot_general(q_heads[h], kblk, NT_DIMS, preferred_element_type=F32)
            s = s + (c_col[:, h:h + 1] - c_row[h:h + 1, :])
            if masked:
                s = jnp.where(causal, s, NEG_INF)
            m_new = jnp.maximum(m, jnp.max(s, axis=1, keepdims=True))
            alpha = jnp.exp(m - m_new)
            p = jnp.exp(s - m_new)
            l_new = alpha * l + jnp.sum(p, axis=1, keepdims=True)
            acc_ref[h] = alpha * acc_ref[h] + jnp.dot(p.astype(BF16), vblk, preferred_element_type=F32)
            out += [m_new, l_new]
        return tuple(out)

    m0 = jnp.full((blk, 1), NEG_INF, F32)
    l0 = jnp.zeros((blk, 1), F32)
    carry = lax.fori_loop(0, qi, lambda kb, c: block(kb, c, False), (m0, l0, m0, l0))
    carry = block(qi, carry, True)
    o_ref[...] = jnp.where(lane < dh, acc_ref[0] / carry[1], acc_ref[1] / carry[3]).astype(o_ref.dtype)


def _fox_attention(qkv, f_rows, f_bias, batch, seq, heads, dh):
    t = qkv.shape[0]
    blk = ATT_BLOCK
    nq = seq // blk
    hp = heads * dh // LANES
    kern = functools.partial(_fox_kernel, blk=blk, seq=seq, dh=dh)
    return pl.pallas_call(
        kern,
        grid=(batch, hp, nq),
        in_specs=[pl.BlockSpec((1, 2, 1), lambda b, p, i: (p, 0, 0)),
                  pl.BlockSpec((1, 1, 2, seq), lambda b, p, i: (b, p, 0, 0)),
                  pl.BlockSpec((blk, LANES), lambda b, p, i: (b * nq + i, p)),
                  pl.BlockSpec((seq, LANES), lambda b, p, i: (b, hp + p)),
                  pl.BlockSpec((seq, LANES), lambda b, p, i: (b, 2 * hp + p))],
        out_specs=pl.BlockSpec((blk, LANES), lambda b, p, i: (b * nq + i, p)),
        out_shape=jax.ShapeDtypeStruct((t, heads * dh), BF16),
        scratch_shapes=[pltpu.VMEM((2, seq), F32), pltpu.VMEM((2, blk, LANES), F32)],
        compiler_params=_cparams("parallel", "parallel", "arbitrary"),
        name="fox_attention",
    )(f_bias.reshape(hp, 2, 1), f_rows, qkv, qkv, qkv)


def _s5_tables(a_re, a_im, log_dt, b_re, b_im, c_re, c_im, d_skip):
    L = S5_CHUNK
    g, p = a_re.shape
    hc = b_re.shape[-1]
    dt = jnp.exp(log_dt)[:, None]
    mag = jnp.exp(a_re * dt)
    lb_re = mag * jnp.cos(a_im * dt)
    lb_im = mag * jnp.sin(a_im * dt)
    num_re = lb_re - 1.0
    num_im = lb_im
    den = a_re * a_re + a_im * a_im
    z_re = (num_re * a_re + num_im * a_im) / den
    z_im = (num_im * a_re - num_re * a_im) / den
    bb_re = z_re[..., None] * b_re - z_im[..., None] * b_im
    bb_im = z_re[..., None] * b_im + z_im[..., None] * b_re
    tau = jnp.arange(L + 1, dtype=F32)
    pmag = jnp.exp((a_re * dt)[..., None] * tau)
    pw_re = pmag * jnp.cos((a_im * dt)[..., None] * tau)
    pw_im = pmag * jnp.sin((a_im * dt)[..., None] * tau)
    cp_re = c_re[..., None] * pw_re[:, None] - c_im[..., None] * pw_im[:, None]
    cp_im = c_re[..., None] * pw_im[:, None] + c_im[..., None] * pw_re[:, None]
    kern = (jnp.einsum("gopt,gpi->gtoi", cp_re[..., :L], bb_re, precision=HIGHEST)
            - jnp.einsum("gopt,gpi->gtoi", cp_im[..., :L], bb_im, precision=HIGHEST))
    kern = kern.at[:, 0].add(d_skip[:, :, None] * jnp.eye(hc, dtype=F32))
    kern = jnp.concatenate([kern, jnp.zeros((g, 1, hc, hc), F32)], axis=1)
    s_idx = jnp.arange(L)[:, None]
    t_idx = jnp.arange(L)[None, :]
    lag = jnp.where(t_idx >= s_idx, t_idx - s_idx, L)
    toep = kern[:, lag]
    toep = toep.transpose(0, 1, 4, 2, 3).reshape(g, L * hc, L * hc)
    rev = (L - 1) - jnp.arange(L)
    e_re = pw_re[:, :, rev][..., None] * bb_re[:, :, None] - pw_im[:, :, rev][..., None] * bb_im[:, :, None]
    e_im = pw_re[:, :, rev][..., None] * bb_im[:, :, None] + pw_im[:, :, rev][..., None] * bb_re[:, :, None]
    e_re = e_re.transpose(0, 2, 3, 1).reshape(g, L * hc, p)
    e_im = e_im.transpose(0, 2, 3, 1).reshape(g, L * hc, p)
    o_re = cp_re[..., 1:].transpose(0, 2, 3, 1).reshape(g, p, L * hc)
    o_im = -cp_im[..., 1:].transpose(0, 2, 3, 1).reshape(g, p, L * hc)

    def pair_diag(m):
        m = m.reshape(g // 2, 2, *m.shape[1:])
        z = jnp.zeros_like(m[:, 0])
        top = jnp.concatenate([m[:, 0], z], axis=2)
        bot = jnp.concatenate([z, m[:, 1]], axis=2)
        return jnp.concatenate([top, bot], axis=1)

    toep_p = pair_diag(toep)
    w_end = jnp.concatenate([pair_diag(e_re), pair_diag(e_im)], axis=2)
    w_out = jnp.concatenate([pair_diag(o_re), pair_diag(o_im)], axis=1)
    lam_re = pw_re[..., L].reshape(g // 2, 2 * p)
    lam_im = pw_im[..., L].reshape(g // 2, 2 * p)
    return toep_p.astype(BF16), w_end.astype(BF16), w_out.astype(BF16), lam_re, lam_im


def _s5_end_kernel(u_ref, w_ref, ere_ref, eim_ref):
    e = jnp.dot(u_ref[0], w_ref[0], preferred_element_type=F32)
    half = e.shape[1] // 2
    ere_ref[0] = e[:, :half]
    eim_ref[0] = e[:, half:]


def _s5_scan_kernel(lre_ref, lim_ref, ere_ref, eim_ref, hre_ref, him_ref, *, batch, nchunk, rows):
    lr = lre_ref[...]
    li = lim_ref[...]
    npair = lr.shape[0]

    def step(j, carry):
        out = []
        for b in range(batch):
            hr, hi = carry[2 * b], carry[2 * b + 1]
            sl = pl.ds(b * nchunk + j, npair, stride=rows)
            hre_ref[sl, :] = hr
            him_ref[sl, :] = hi
            er = ere_ref[sl, :]
            ei = eim_ref[sl, :]
            out += [lr * hr - li * hi + er, lr * hi + li * hr + ei]
        return tuple(out)

    z = jnp.zeros_like(lr)
    lax.fori_loop(0, nchunk, step, (z,) * (2 * batch))


def _s5_out_kernel(u_ref, t_ref, hre_ref, him_ref, w_ref, y_ref):
    half = w_ref.shape[1] // 2
    y = jnp.dot(u_ref[0], t_ref[0], preferred_element_type=F32)
    y = y + jnp.dot(hre_ref[0].astype(BF16), w_ref[0, :half, :], preferred_element_type=F32)
    y = y + jnp.dot(him_ref[0].astype(BF16), w_ref[0, half:, :], preferred_element_type=F32)
    y_ref[0] = jax.nn.gelu(y)


def _s5(u, tables, batch, seq, groups, hc):
    toep, w_end, w_out, lam_re, lam_im = tables
    L = S5_CHUNK
    t = u.shape[0]
    rows = t // L
    nchunk = seq // L
    npair = groups // 2
    kc = 2 * L * hc
    pw = lam_re.shape[1]
    ug = u.reshape(rows, L, npair, 2, hc).transpose(2, 0, 3, 1, 4).reshape(npair, rows, kc)
    blk3 = lambda a, b: pl.BlockSpec((1, a, b), lambda i: (i, 0, 0))
    e_re, e_im = pl.pallas_call(
        _s5_end_kernel,
        grid=(npair,),
        in_specs=[blk3(rows, kc), blk3(kc, 2 * pw)],
        out_specs=[blk3(rows, pw)] * 2,
        out_shape=[jax.ShapeDtypeStruct((npair, rows, pw), F32)] * 2,
        compiler_params=_cparams("parallel"),
        name="s5_chunk_end",
    )(ug, w_end)
    pg = 8
    lam_spec = pl.BlockSpec((pg, pw), lambda i: (i, 0))
    st_spec = pl.BlockSpec((pg * rows, pw), lambda i: (i, 0))
    h_re, h_im = pl.pallas_call(
        functools.partial(_s5_scan_kernel, batch=batch, nchunk=nchunk, rows=rows),
        grid=(npair // pg,),
        in_specs=[lam_spec, lam_spec, st_spec, st_spec],
        out_specs=[st_spec] * 2,
        out_shape=[jax.ShapeDtypeStruct((npair * rows, pw), F32)] * 2,
        compiler_params=_cparams("parallel"),
        name="s5_chunk_scan",
    )(lam_re, lam_im, e_re.reshape(npair * rows, pw), e_im.reshape(npair * rows, pw))
    y = pl.pallas_call(
        _s5_out_kernel,
        grid=(npair,),
        in_specs=[blk3(rows, kc), blk3(kc, kc), blk3(rows, pw), blk3(rows, pw), blk3(2 * pw, kc)],
        out_specs=blk3(rows, kc),
        out_shape=jax.ShapeDtypeStruct((npair, rows, kc), F32),
        compiler_params=_cparams("parallel"),
        name="s5_out",
    )(ug, toep, h_re.reshape(npair, rows, pw), h_im.reshape(npair, rows, pw), w_out)
    return y.reshape(npair, rows, 2, L, hc).transpose(1, 3, 0, 2, 4).reshape(t, groups * hc)


def _mlstm_kernel(gb_ref, g_ref, q_ref, k_ref, v_ref, o_ref, h_ref, c_ref, *, L, seq, dh):
    tri = (_iota((L, L), 0) <= _iota((L, L), 1)).astype(F32)
    eye = (_iota((L, L), 0) == _iota((L, L), 1)).astype(F32)
    causal = _iota((L, L), 1) <= _iota((L, L), 0)
    lane = _iota((L, LANES), 1)
    one_col = (lane == 0).astype(BF16)
    c_ref[...] = jnp.zeros_like(c_ref)

    def chunk(c, m_prev):
        st = pl.multiple_of(c * L, L)
        gates = g_ref[0, 0, :, pl.ds(st, L)] + gb_ref[0]
        li_row = gates[0:1]
        b_row = jnp.dot(_log_sigmoid(gates[1:2]), tri, precision=HIGHEST, preferred_element_type=F32)
        sub = _iota((8, L), 0)
        rows = jnp.where(sub == 0, b_row, jnp.where(sub == 1, li_row, 0.0))
        cols = lax.dot_general(eye, rows, NT_DIMS, precision=HIGHEST, preferred_element_type=F32)
        b_col, li_col = cols[:, 0:1], cols[:, 1:2]
        b_last = b_row[:, L - 1:L]
        q = q_ref[pl.ds(st, L), :]
        k = k_ref[pl.ds(st, L), :]
        v = v_ref[pl.ds(st, L), :]

        log_d = jnp.where(causal, b_col - b_row + li_row, NEG_INF)
        m_inter = b_col + m_prev
        m_t = jnp.maximum(m_inter, jnp.max(log_d, axis=1, keepdims=True))
        s = lax.dot_general(q, k, NT_DIMS, preferred_element_type=F32) * jnp.exp(log_d - m_t)
        v_aug = jnp.concatenate([v, one_col], axis=1)
        tot = jnp.dot(s.astype(BF16), v_aug, preferred_element_type=F32)
        tot = tot + jnp.exp(m_inter - m_t) * jnp.dot(q, c_ref[...].astype(BF16), preferred_element_type=F32)
        den = jnp.maximum(jnp.abs(tot[:, dh:dh + 1]), jnp.exp(-m_t))
        h = tot[:, :dh] / den
        h_ref[pl.ds(st, L), :] = (h * jax.nn.sigmoid(o_ref[pl.ds(st, L), :])).astype(h_ref.dtype)

        g_row = b_last - b_row + li_row
        m_new = jnp.maximum(b_last + m_prev, jnp.max(g_row, axis=1, keepdims=True))
        w_col = jnp.exp(b_last - b_col + li_col - m_new)
        vw = jnp.concatenate([(v.astype(F32) * w_col).astype(BF16),
                              jnp.where(lane == 0, w_col, 0.0).astype(BF16)], axis=1)
        c_ref[...] = (jnp.exp(b_last + m_prev - m_new) * c_ref[...]
                      + lax.dot_general(k, vw, TN_DIMS, preferred_element_type=F32))
        return m_new

    lax.fori_loop(0, seq // L, chunk, jnp.zeros((1, 1), F32))


def _mlstm(q, k, v, o, gate_rows, gate_bias, batch, seq, heads, dh):
    t = q.shape[0]
    L = MLSTM_CHUNK
    col = pl.BlockSpec((seq, dh), lambda b, h: (b, h))
    return pl.pallas_call(
        functools.partial(_mlstm_kernel, L=L, seq=seq, dh=dh),
        grid=(batch, heads),
        in_specs=[pl.BlockSpec((1, 8, 1), lambda b, h: (h, 0, 0)),
                  pl.BlockSpec((1, 1, 8, seq), lambda b, h: (b, h, 0, 0)),
                  col, col, col, col],
        out_specs=col,
        out_shape=jax.ShapeDtypeStruct((t, heads * dh), BF16),
        scratch_shapes=[pltpu.VMEM((dh, 2 * dh), F32)],
        compiler_params=_cparams("parallel", "parallel"),
        name="mlstm",
    )(gate_bias, gate_rows, q, k, v, o)


def _out_even_kernel(h_ref, att_ref, ys_ref, wg_ref, bg_ref, wo_ref, g_ref, b_ref, o_ref, *, alpha):
    ys = ys_ref[...]
    half = att_ref.shape[1]
    gate = jax.nn.sigmoid(jnp.dot(ys.astype(BF16), wg_ref[...], preferred_element_type=F32) + bg_ref[...])
    mix = jnp.dot(att_ref[...], wo_ref[:half, :], preferred_element_type=F32)
    mix = mix + jnp.dot((ys * gate).astype(BF16), wo_ref[half:, :], preferred_element_type=F32)
    o_ref[...] = _layer_norm(alpha * h_ref[...] + mix, g_ref[...], b_ref[...])


def _out_odd_kernel(h_ref, hm_ref, wo_ref, g_ref, b_ref, o_ref, *, alpha):
    mix = jnp.dot(hm_ref[...], wo_ref[...], preferred_element_type=F32)
    o_ref[...] = _layer_norm(alpha * h_ref[...] + mix, g_ref[...], b_ref[...])


def _row_tiled_call(kern, row_args, fixed_args, out_dtype, name):
    t = row_args[0].shape[0]
    tm = TM_PROJ
    in_specs = [pl.BlockSpec((tm, a.shape[1]), lambda i: (i, 0)) for a in row_args]
    in_specs += [pl.BlockSpec(a.shape, lambda i: (0, 0)) for a in fixed_args]
    d = row_args[0].shape[1]
    return pl.pallas_call(
        kern,
        grid=(t // tm,),
        in_specs=in_specs,
        out_specs=pl.BlockSpec((tm, d), lambda i: (i, 0)),
        out_shape=jax.ShapeDtypeStruct((t, d), out_dtype),
        compiler_params=_cparams("parallel"),
        name=name,
    )(*row_args, *fixed_args)


def _pack_bf16_pair(a, b):
    a_bits = lax.bitcast_convert_type(a.astype(BF16).astype(F32), jnp.uint32)
    b_bits = lax.bitcast_convert_type(b.astype(BF16).astype(F32), jnp.uint32)
    return (a_bits >> 16) | (b_bits & jnp.uint32(0xFFFF0000))


def _unpack_bf16_pair(p):
    a = lax.bitcast_convert_type(p << 16, F32)
    b = lax.bitcast_convert_type(p & jnp.uint32(0xFFFF0000), F32)
    return a, b


def _router_kernel(h_ref, w_ref, b_ref, o_ref, xp_ref, cnt_ref, run_ref, *, n_groups, epg):
    @pl.when(pl.program_id(0) == 0)
    def _():
        run_ref[...] = jnp.zeros_like(run_ref)

    h = h_ref[...]
    half = h.shape[1] // 2
    xp_ref[...] = _pack_bf16_pair(h[:, :half], h[:, half:])
    logits = jnp.dot(h, w_ref[...], precision=HIGHEST, preferred_element_type=F32) + b_ref[...]
    lane = _iota(logits.shape, 1)
    big = jnp.int32(LANES)
    lg = jnp.where(lane < n_groups, logits, NEG_INF)
    mg = jnp.max(lg, axis=1, keepdims=True)
    g_val = 1.0 / jnp.sum(jnp.exp(lg - mg), axis=1, keepdims=True)
    g_idx = jnp.min(jnp.where(lg == mg, lane, big), axis=1, keepdims=True)
    lo_lane = n_groups + g_idx * epg
    le = jnp.where((lane >= lo_lane) & (lane < lo_lane + epg), logits, NEG_INF)
    m1 = jnp.max(le, axis=1, keepdims=True)
    i1 = jnp.min(jnp.where(le == m1, lane, big), axis=1, keepdims=True)
    le2 = jnp.where(lane == i1, NEG_INF, le)
    m2 = jnp.max(le2, axis=1, keepdims=True)
    i2 = jnp.min(jnp.where(le2 == m2, lane, big), axis=1, keepdims=True)
    r = jnp.exp(m2 - m1)
    w1 = g_val / (1.0 + r)
    w2 = g_val * r / (1.0 + r)
    e1 = i1 - lo_lane
    e2 = i2 - lo_lane
    first_lo = e1 < e2
    lo = jnp.where(first_lo, e1, e2)
    hi = jnp.where(first_lo, e2, e1)
    w_lo = jnp.where(first_lo, w1, w2)
    w_hi = jnp.where(first_lo, w2, w1)
    pair = (lo * (2 * epg - 1 - lo)) // 2 + (hi - lo - 1)
    cls = g_idx * (epg * (epg - 1) // 2) + pair
    tm = h.shape[0]
    onehot = lane == cls
    earlier = (_iota((tm, tm), 1) < _iota((tm, tm), 0)).astype(BF16)
    before = jnp.dot(earlier, onehot.astype(BF16), preferred_element_type=F32) + run_ref[...]
    rank = jnp.sum(jnp.where(onehot, before, 0.0), axis=1, keepdims=True)
    run = run_ref[...] + jnp.sum(onehot.astype(F32), axis=0, keepdims=True)
    run_ref[...] = run
    cnt_ref[...] = run
    out = jnp.where(lane == 0, cls.astype(F32),
                    jnp.where(lane == 1, w_lo, jnp.where(lane == 2, w_hi, jnp.where(lane == 3, rank, 0.0))))
    o_ref[...] = out[:, :o_ref.shape[1]]


def _router(h, w_r, b_r, n_groups, epg):
    t, d = h.shape
    tm = TM_PROJ
    return pl.pallas_call(
        functools.partial(_router_kernel, n_groups=n_groups, epg=epg),
        grid=(t // tm,),
        in_specs=[pl.BlockSpec((tm, d), lambda i: (i, 0)),
                  pl.BlockSpec(w_r.shape, lambda i: (0, 0)),
                  pl.BlockSpec(b_r.shape, lambda i: (0, 0))],
        out_specs=[pl.BlockSpec((tm, 8), lambda i: (i, 0)), pl.BlockSpec((tm, d // 2), lambda i: (i, 0)),
                   pl.BlockSpec((1, LANES), lambda i: (0, 0))],
        out_shape=[jax.ShapeDtypeStruct((t, 8), F32), jax.ShapeDtypeStruct((t, d // 2), jnp.uint32),
                   jax.ShapeDtypeStruct((1, LANES), F32)],
        scratch_shapes=[pltpu.VMEM((1, LANES), F32)],
        compiler_params=_cparams("arbitrary"),
        name="router",
    )(h, w_r, b_r)


def _row_wait(src_hbm, dst, sem, n):
    def body(r, c):
        pltpu.make_async_copy(src_hbm.at[pl.ds(0, 1)], dst.at[pl.ds(0, 1)], sem).wait()
        return c
    lax.fori_loop(0, n, body, 0)


def _dispatch_kernel(pos_ref, x_hbm, z_hbm, o_hbm, sem, *, tm, nsteps):
    del z_hbm
    i = pl.program_id(0)
    base = i * tm

    def body(r, c):
        pltpu.make_async_copy(x_hbm.at[pl.ds(base + r, 1)], o_hbm.at[pl.ds(pos_ref[0, 0, r], 1)], sem).start()
        return c
    lax.fori_loop(0, tm, body, 0, unroll=8)

    @pl.when(i > 0)
    def _():
        _row_wait(x_hbm, o_hbm, sem, tm)

    @pl.when(i == nsteps - 1)
    def _():
        _row_wait(x_hbm, o_hbm, sem, tm)


def _dispatch(xp, pos3, p_rows):
    t, w = xp.shape
    tm = pos3.shape[2]
    nsteps = t // tm
    return pl.pallas_call(
        functools.partial(_dispatch_kernel, tm=tm, nsteps=nsteps),
        grid=(nsteps,),
        in_specs=[pl.BlockSpec((1, 1, tm), lambda i: (i, 0, 0), memory_space=pltpu.SMEM),
                  pl.BlockSpec(memory_space=pl.ANY), pl.BlockSpec(memory_space=pl.ANY)],
        out_specs=pl.BlockSpec(memory_space=pl.ANY),
        out_shape=jax.ShapeDtypeStruct((p_rows, w), xp.dtype),
        scratch_shapes=[pltpu.SemaphoreType.DMA(())],
        input_output_aliases={2: 0},
        compiler_params=_cparams("arbitrary"),
        name="moe_dispatch",
    )(pos3, xp, jnp.zeros((p_rows, w), xp.dtype))


def _combine_kernel(pos_ref, nxt_ref, h_ref, r_ref, g_ref, b_ref, y_hbm, o_ref, ybuf, sem, *, alpha, tm, nsteps):
    i = pl.program_id(0)
    slot = i % 2

    def gather(p_ref, s):
        def body(r, c):
            pltpu.make_async_copy(y_hbm.at[pl.ds(p_ref[0, 0, r], 1)], ybuf.at[s, pl.ds(r, 1)], sem.at[s]).start()
            return c
        lax.fori_loop(0, tm, body, 0, unroll=8)

    @pl.when(i == 0)
    def _():
        gather(pos_ref, 0)

    @pl.when(i + 1 < nsteps)
    def _():
        gather(nxt_ref, 1 - slot)

    _row_wait(y_hbm, ybuf.at[slot], sem.at[slot], tm)
    y_lo, y_hi = _unpack_bf16_pair(ybuf[slot])
    r = r_ref[...]
    ffn = r[:, 1:2] * y_lo + r[:, 2:3] * y_hi
    o_ref[...] = _layer_norm(alpha * h_ref[...] + ffn, g_ref[...], b_ref[...])


def _combine(h, route, pos3, y_sorted, ln_g, ln_b, alpha):
    t, d = h.shape
    tm = pos3.shape[2]
    nsteps = t // tm
    row = lambda i: (i, 0)
    fixed = lambda i: (0, 0)
    return pl.pallas_call(
        functools.partial(_combine_kernel, alpha=alpha, tm=tm, nsteps=nsteps),
        grid=(nsteps,),
        in_specs=[pl.BlockSpec((1, 1, tm), lambda i: (i, 0, 0), memory_space=pltpu.SMEM),
                  pl.BlockSpec((1, 1, tm), lambda i: (jnp.minimum(i + 1, nsteps - 1), 0, 0), memory_space=pltpu.SMEM),
                  pl.BlockSpec((tm, d), row), pl.BlockSpec((tm, route.shape[1]), row),
                  pl.BlockSpec((1, d), fixed), pl.BlockSpec((1, d), fixed),
                  pl.BlockSpec(memory_space=pl.ANY)],
        out_specs=pl.BlockSpec((tm, d), row),
        out_shape=jax.ShapeDtypeStruct((t, d), F32),
        scratch_shapes=[pltpu.VMEM((2, tm, d), jnp.uint32), pltpu.SemaphoreType.DMA((2,))],
        compiler_params=_cparams("arbitrary"),
        name="moe_combine_ln",
    )(pos3, pos3, h, route, ln_g.reshape(1, -1), ln_b.reshape(1, -1), y_sorted)


def _moe_kernel(elo_ref, ehi_ref, valid_ref, x_ref, g0_ref, u0_ref, d0_ref, g1_ref, u1_ref, d1_ref, y_ref):
    i = pl.program_id(0)

    @pl.when(valid_ref[i] == 0)
    def _():
        y_ref[...] = jnp.zeros_like(y_ref)

    @pl.when(valid_ref[i] != 0)
    def _():
        x_a, x_b = _unpack_bf16_pair(x_ref[...])
        x = jnp.concatenate([x_a.astype(BF16), x_b.astype(BF16)], axis=1)
        ys = []
        for g_ref, u_ref, d_ref in ((g0_ref, u0_ref, d0_ref), (g1_ref, u1_ref, d1_ref)):
            gate = jnp.dot(x, g_ref[0], preferred_element_type=F32)
            up = jnp.dot(x, u_ref[0], preferred_element_type=F32)
            hid = gate * jax.nn.sigmoid(gate) * up
            ys.append(jnp.dot(hid.astype(BF16), d_ref[0], preferred_element_type=F32))
        y_ref[...] = _pack_bf16_pair(ys[0], ys[1])


def _moe_experts(x_sorted, e_lo, e_hi, valid, w_gate, w_up, w_down):
    p = x_sorted.shape[0]
    d = w_gate.shape[1]
    tm = TM_MOE
    f = w_gate.shape[2]
    lo_in = pl.BlockSpec((1, d, f), lambda i, lo, hi, va: (lo[i], 0, 0))
    hi_in = pl.BlockSpec((1, d, f), lambda i, lo, hi, va: (hi[i], 0, 0))
    lo_dn = pl.BlockSpec((1, f, d), lambda i, lo, hi, va: (lo[i], 0, 0))
    hi_dn = pl.BlockSpec((1, f, d), lambda i, lo, hi, va: (hi[i], 0, 0))
    grid_spec = pltpu.PrefetchScalarGridSpec(
        num_scalar_prefetch=3,
        grid=(p // tm,),
        in_specs=[pl.BlockSpec((tm, d // 2), lambda i, lo, hi, va: (i, 0)),
                  lo_in, lo_in, lo_dn, hi_in, hi_in, hi_dn],
        out_specs=pl.BlockSpec((tm, d), lambda i, lo, hi, va: (i, 0)),
    )
    return pl.pallas_call(
        _moe_kernel,
        grid_spec=grid_spec,
        out_shape=jax.ShapeDtypeStruct((p, d), jnp.uint32),
        compiler_params=_cparams("arbitrary"),
        name="moe_experts",
    )(e_lo, e_hi, valid, x_sorted, w_gate, w_up, w_down, w_gate, w_up, w_down)


def _hier_moe_ln(h, ln_g, ln_b, alpha, w_group, b_group, w_expert, b_expert, w_gate, w_up, w_down):
    t, d = h.shape
    n_groups, _, epg = w_expert.shape
    npairs = epg * (epg - 1) // 2
    ncls = n_groups * npairs
    tm = TM_MOE
    w_r = jnp.concatenate([w_group, w_expert.transpose(1, 0, 2).reshape(d, n_groups * epg)], axis=1)
    b_r = jnp.concatenate([b_group, b_expert.reshape(-1)])
    pad = LANES - w_r.shape[1]
    w_r = jnp.pad(w_r, ((0, 0), (0, pad)))
    b_r = jnp.pad(b_r, (0, pad)).reshape(1, LANES)
    route, xp, cnt = _router(h, w_r, b_r, n_groups, epg)

    cls = route[:, 0].astype(jnp.int32)
    rank = route[:, 3].astype(jnp.int32)
    counts = cnt[0, :ncls].astype(jnp.int32)
    padded = ((counts + tm - 1) // tm) * tm
    pend = jnp.cumsum(padded)
    pstart = pend - padded
    pos = jnp.sum(jnp.where(cls[:, None] == jnp.arange(ncls)[None, :], pstart[None, :], 0), axis=1) + rank
    pos3 = pos.astype(jnp.int32).reshape(t // TM_PROJ, 1, TM_PROJ)
    p_rows = t + ncls * tm
    x_sorted = _dispatch(xp, pos3, p_rows)

    tile_start = jnp.arange(p_rows // tm, dtype=jnp.int32) * tm
    tile_cls = jnp.minimum(jnp.searchsorted(pend, tile_start, side="right"), ncls - 1).astype(jnp.int32)
    valid = (tile_start < pend[-1]).astype(jnp.int32)
    lo_tab = jnp.asarray(PAIRS_LO, jnp.int32)
    hi_tab = jnp.asarray(PAIRS_HI, jnp.int32)
    e_lo = (tile_cls // npairs) * epg + lo_tab[tile_cls % npairs]
    e_hi = (tile_cls // npairs) * epg + hi_tab[tile_cls % npairs]

    y_sorted = _moe_experts(x_sorted, e_lo, e_hi, valid,
                            w_gate.astype(BF16), w_up.astype(BF16), w_down.astype(BF16))
    return _combine(h, route, pos3, y_sorted, ln_g, ln_b, alpha)


def _even_mixer(h, batch, seq, alpha, ln_g, ln_b, w_in, f_bias, s5_params, w_glu, b_glu, w_out):
    t, d = h.shape
    heads = f_bias.shape[0]
    groups, p_state = s5_params[0].shape
    hc = s5_params[3].shape[-1]
    s5_width = groups * hc
    fox_width = d - s5_width
    dh = fox_width // heads
    q_scale = dh ** -0.5
    w_q, w_k, w_v, w_f, w_u = jnp.split(w_in, [fox_width, 2 * fox_width, 3 * fox_width, 3 * fox_width + heads], axis=1)
    w_cat = jnp.concatenate([w_q * q_scale, w_k, w_v, w_u, jnp.pad(w_f, ((0, 0), (0, LANES - heads)))], axis=1)
    segs = ((0, 3 * fox_width), (3 * fox_width, s5_width), (3 * fox_width + s5_width, LANES))
    qkv, u, f = _proj(h, w_cat.astype(BF16), segs, (BF16, BF16, F32))
    f_rows = f[:, :heads].reshape(batch, seq, heads // 2, 2).transpose(0, 2, 3, 1)
    att = _fox_attention(qkv, f_rows, f_bias, batch, seq, heads, dh)
    ys = _s5(u, _s5_tables(*s5_params), batch, seq, groups, hc)
    kern = functools.partial(_out_even_kernel, alpha=alpha)
    return _row_tiled_call(kern, (h, att, ys),
                           (w_glu.astype(BF16), b_glu.reshape(1, -1), w_out.astype(BF16),
                            ln_g.reshape(1, -1), ln_b.reshape(1, -1)), F32, "out_even")


def _odd_mixer(h, batch, seq, alpha, ln_g, ln_b, w_in, conv_w, conv_b, i_bias, f_bias, w_out):
    t, d = h.shape
    heads = i_bias.shape[0]
    dmix = conv_w.shape[1] // 2
    dh = dmix // heads
    w_main, w_gates = w_in[:, :4 * dmix], w_in[:, 4 * dmix:]
    w_cat = jnp.concatenate([w_main, jnp.pad(w_gates, ((0, 0), (0, LANES - 2 * heads)))], axis=1)
    q, k, v, o, gates = _proj_odd(h, w_cat.astype(BF16), conv_w, conv_b, seq, dh)
    g = gates[:, :2 * heads].reshape(batch, seq, 2, heads).transpose(0, 3, 2, 1)
    gate_rows = jnp.pad(g, ((0, 0), (0, 0), (0, 6), (0, 0)))
    gate_bias = jnp.pad(jnp.stack([i_bias, f_bias], axis=1), ((0, 0), (0, 6))).reshape(heads, 8, 1)
    hm = _mlstm(q, k, v, o, gate_rows, gate_bias, batch, seq, heads, dh)
    kern = functools.partial(_out_odd_kernel, alpha=alpha)
    return _row_tiled_call(kern, (h, hm), (w_out.astype(BF16), ln_g.reshape(1, -1), ln_b.reshape(1, -1)),
                           F32, "out_odd")


def kernel(x, ln_g, ln_b, even_w_in, fox_f_bias, s5_a_re, s5_a_im, s5_log_dt, s5_b_re, s5_b_im, s5_c_re, s5_c_im, s5_d, s5_w_glu, s5_b_glu, even_w_out, odd_w_in, mlstm_conv_w, mlstm_conv_b, mlstm_i_bias, mlstm_f_bias, odd_w_out, moe_w_group, moe_b_group, moe_w_expert, moe_b_expert, moe_w_gate, moe_w_up, moe_w_down):
    batch, seq, d = x.shape
    depth = ln_g.shape[0]
    alpha = (2 * depth) ** 0.25
    h = x.reshape(batch * seq, d)
    for layer in range(depth):
        j = layer // 2
        if layer % 2 == 0:
            s5_params = (s5_a_re[j], s5_a_im[j], s5_log_dt[j], s5_b_re[j], s5_b_im[j],
                         s5_c_re[j], s5_c_im[j], s5_d[j])
            h = _even_mixer(h, batch, seq, alpha, ln_g[layer, 0], ln_b[layer, 0], even_w_in[j], fox_f_bias[j],
                            s5_params, s5_w_glu[j], s5_b_glu[j], even_w_out[j])
        else:
            h = _odd_mixer(h, batch, seq, alpha, ln_g[layer, 0], ln_b[layer, 0], odd_w_in[j], mlstm_conv_w[j],
                           mlstm_conv_b[j], mlstm_i_bias[j], mlstm_f_bias[j], odd_w_out[j])
        h = _hier_moe_ln(h, ln_g[layer, 1], ln_b[layer, 1], alpha, moe_w_group[layer], moe_b_group[layer],
                         moe_w_expert[layer], moe_b_expert[layer], moe_w_gate[layer], moe_w_up[layer],
                         moe_w_down[layer])
    return h.reshape(batch, seq, d)
```

```python
import functools

import jax
import jax.numpy as jnp
from jax import lax
from jax.experimental import pallas as pl
from jax.experimental.pallas import tpu as pltpu

F32 = jnp.float32
BF16 = jnp.bfloat16
HIGHEST = lax.Precision.HIGHEST
LN_EPS = 1e-5
NEG_INF = float("-inf")

LANES = 128
VMEM_LIMIT = 56 * 1024 * 1024

TM_PROJ = 256
ATT_BLOCK = 256
S5_CHUNK = 16
MLSTM_CHUNK = 256
MLSTM_HEADS_PER_STEP = 2
TM_MOE = 256
PAIRS_LO = (0, 0, 0, 1, 1, 2)
PAIRS_HI = (1, 2, 3, 2, 3, 3)

NT_DIMS = (((1,), (1,)), ((), ()))
TN_DIMS = (((0,), (0,)), ((), ()))


def _cparams(*sem):
    return pltpu.CompilerParams(dimension_semantics=sem, vmem_limit_bytes=VMEM_LIMIT)


def _log_sigmoid(x):
    return jnp.minimum(x, 0.0) - jnp.log1p(jnp.exp(-jnp.abs(x)))


def _layer_norm(x, g, b):
    mu = jnp.mean(x, axis=-1, keepdims=True)
    xc = x - mu
    var = jnp.mean(xc * xc, axis=-1, keepdims=True)
    return xc * lax.rsqrt(var + LN_EPS) * g + b


def _iota(shape, dim):
    return lax.broadcasted_iota(jnp.int32, shape, dim)


def _proj_kernel(x_ref, w_ref, *o_refs, segs):
    xb = x_ref[...].astype(BF16)
    for o_ref, (start, width) in zip(o_refs, segs):
        o_ref[...] = jnp.dot(xb, w_ref[:, start:start + width],
                             preferred_element_type=F32).astype(o_ref.dtype)


def _proj(x, w, segs, dtypes):
    t, d = x.shape
    tm = TM_PROJ
    return pl.pallas_call(
        functools.partial(_proj_kernel, segs=segs),
        grid=(t // tm,),
        in_specs=[pl.BlockSpec((tm, d), lambda i: (i, 0)),
                  pl.BlockSpec(w.shape, lambda i: (0, 0))],
        out_specs=[pl.BlockSpec((tm, wd), lambda i: (i, 0)) for _, wd in segs],
        out_shape=[jax.ShapeDtypeStruct((t, wd), dt) for (_, wd), dt in zip(segs, dtypes)],
        compiler_params=_cparams("parallel"),
        name="proj_even",
    )(x, w)


def _proj_odd_kernel(x_ref, w_ref, cw_ref, cb_ref, q_ref, k_ref, v_ref, o_ref, g_ref, zs_ref,
                     *, tm, dmix, k_scale, tiles_per_seq, conv_width):
    i = pl.program_id(0)
    xb = x_ref[...].astype(BF16)

    @pl.when(i % tiles_per_seq == 0)
    def _():
        zs_ref[0:8, :] = jnp.zeros((8, 2 * dmix), F32)

    cw = 512
    for c0 in range(0, 2 * dmix, cw):
        zs_ref[8:tm + 8, c0:c0 + cw] = jnp.dot(xb, w_ref[:, c0:c0 + cw], preferred_element_type=F32)
    for c0 in range(0, 2 * dmix, cw):
        acc = jnp.broadcast_to(cb_ref[:, c0:c0 + cw], (tm, cw))
        for j in range(conv_width):
            acc = acc + cw_ref[j:j + 1, c0:c0 + cw] * zs_ref[pl.ds(8 - (conv_width - 1) + j, tm), c0:c0 + cw]
        y = acc * jax.nn.sigmoid(acc)
        if c0 < dmix:
            q_ref[:, c0:c0 + cw] = y.astype(BF16)
        else:
            k_ref[:, c0 - dmix:c0 - dmix + cw] = (y * k_scale).astype(BF16)
    zs_ref[0:8, :] = zs_ref[tm:tm + 8, :]

    v_ref[...] = jnp.dot(xb, w_ref[:, 2 * dmix:3 * dmix], preferred_element_type=F32).astype(BF16)
    o_ref[...] = jnp.dot(xb, w_ref[:, 3 * dmix:4 * dmix], preferred_element_type=F32)
    g_ref[...] = jnp.dot(xb, w_ref[:, 4 * dmix:4 * dmix + LANES], preferred_element_type=F32)


def _proj_odd(x, w, conv_w, conv_b, seq, head_dim):
    t, d = x.shape
    dmix = conv_w.shape[1] // 2
    tm = TM_PROJ
    kern = functools.partial(_proj_odd_kernel, tm=tm, dmix=dmix, k_scale=head_dim ** -0.5,
                             tiles_per_seq=seq // tm, conv_width=conv_w.shape[0])
    row = lambda i: (i, 0)
    fixed = lambda i: (0, 0)
    return pl.pallas_call(
        kern,
        grid=(t // tm,),
        in_specs=[pl.BlockSpec((tm, d), row), pl.BlockSpec(w.shape, fixed),
                  pl.BlockSpec(conv_w.shape, fixed), pl.BlockSpec((1, 2 * dmix), fixed)],
        out_specs=[pl.BlockSpec((tm, dmix), row)] * 4 + [pl.BlockSpec((tm, LANES), row)],
        out_shape=[jax.ShapeDtypeStruct((t, dmix), BF16)] * 3
        + [jax.ShapeDtypeStruct((t, dmix), F32), jax.ShapeDtypeStruct((t, LANES), F32)],
        scratch_shapes=[pltpu.VMEM((tm + 8, 2 * dmix), F32)],
        compiler_params=_cparams("arbitrary"),
        name="proj_odd",
    )(x, w, conv_w, conv_b.reshape(1, -1))


def _split3(x):
    hi = x.astype(BF16).astype(F32)
    r = x - hi
    mid = r.astype(BF16).astype(F32)
    lo = (r - mid).astype(BF16).astype(F32)
    return hi, mid, lo


def _fox_kernel(fb_ref, f_ref, q_ref, k_ref, v_ref, o_ref, c_ref, kaug_ref, vt_ref, acc_ref, *, blk, seq, dh):
    qi = pl.program_id(2)
    lane = _iota((blk, LANES), 1)
    head_lanes = (lane < dh, lane >= dh)

    @pl.when(qi == 0)
    def _():
        tril = (_iota((blk, blk), 1) <= _iota((blk, blk), 0)).astype(F32)
        eye_b = (_iota((LANES, LANES), 0) == _iota((LANES, LANES), 1)).astype(BF16)

        def prep(jb, carry):
            rows = pl.ds(pl.multiple_of(jb * blk, blk), blk)
            ls = _log_sigmoid(f_ref[rows, :] + fb_ref[0])
            c = jnp.dot(tril, ls, precision=HIGHEST, preferred_element_type=F32) + carry
            c_ref[rows, :] = c
            kblk = k_ref[rows, :]
            for j in range(2):
                hi, mid, lo = _split3(c[:, j:j + 1])
                aug = jnp.where(lane < 3, 1.0, jnp.where(lane == 3, -hi, jnp.where(
                    lane == 4, -mid, jnp.where(lane == 5, -lo, 0.0))))
                kaug_ref[j, rows, :] = jnp.concatenate(
                    [jnp.where(head_lanes[j], kblk, jnp.zeros_like(kblk)), aug.astype(BF16)], axis=1)
            vt_ref[jb] = lax.dot_general(eye_b, v_ref[rows, :], NT_DIMS,
                                         preferred_element_type=F32).astype(BF16)
            return c[blk - 1:blk, :]

        lax.fori_loop(0, seq // blk, prep, jnp.zeros((1, LANES), F32))

    q = q_ref[...]
    c_q = c_ref[pl.ds(pl.multiple_of(qi * blk, blk), blk), :]
    q_aug = []
    for j in range(2):
        hi, mid, lo = _split3(c_q[:, j:j + 1])
        aug = jnp.where(lane == 0, hi, jnp.where(lane == 1, mid, jnp.where(
            lane == 2, lo, jnp.where(lane < 6, 1.0, 0.0))))
        q_aug.append(jnp.concatenate([jnp.where(head_lanes[j], q, jnp.zeros_like(q)), aug.astype(BF16)], axis=1))
    key_le_query = _iota((blk, blk), 0) <= _iota((blk, blk), 1)
    acc_ref[...] = jnp.zeros_like(acc_ref)

    def score(kb):
        krows = pl.ds(pl.multiple_of(kb * blk, blk), blk)
        return tuple(lax.dot_general(kaug_ref[j, krows, :], q_aug[j], NT_DIMS, preferred_element_type=F32)
                     for j in range(2))

    def block(kb, scores, carry, masked):
        vt = vt_ref[kb]
        out, probs, alphas = [], [], []
        for j in range(2):
            m, l = carry[2 * j], carry[2 * j + 1]
            s = jnp.where(key_le_query, scores[j], NEG_INF) if masked else scores[j]
            m_new = jnp.maximum(m, jnp.max(s, axis=0, keepdims=True))
            alpha = jnp.exp(m - m_new)
            p = jnp.exp(s - m_new)
            out += [m_new, alpha * l + jnp.sum(p, axis=0, keepdims=True)]
            probs.append(p.astype(BF16))
            alphas.append(alpha)
        for j in range(2):
            acc_ref[j] = alphas[j] * acc_ref[j] + jnp.dot(vt[j * dh:(j + 1) * dh, :], probs[j],
                                                          preferred_element_type=F32)
        return tuple(out)

    m0 = jnp.full((1, blk), NEG_INF, F32)
    l0 = jnp.zeros((1, blk), F32)

    def step(kb, state):
        nxt = score(kb + 1)
        return nxt, block(kb, state[0], state[1], False)

    scores, carry = lax.fori_loop(0, qi, step, (score(0), (m0, l0, m0, l0)))
    carry = block(qi, scores, carry, True)
    out_t = jnp.concatenate([acc_ref[0] / carry[1], acc_ref[1] / carry[3]], axis=0).astype(BF16)
    eye_q = (_iota((blk, blk), 0) == _iota((blk, blk), 1)).astype(BF16)
    o_ref[...] = lax.dot_general(eye_q, out_t, NT_DIMS, preferred_element_type=F32).astype(o_ref.dtype)


def _fox_attention(qkv, f_cols, f_bias, batch, seq, heads, dh):
    t = qkv.shape[0]
    blk = ATT_BLOCK
    nq = seq // blk
    hp = heads * dh // LANES
    kern = functools.partial(_fox_kernel, blk=blk, seq=seq, dh=dh)
    return pl.pallas_call(
        kern,
        grid=(batch, hp, nq),
        in_specs=[pl.BlockSpec((1, 1, LANES), lambda b, p, i: (p, 0, 0)),
                  pl.BlockSpec((seq, LANES), lambda b, p, i: (b, p)),
                  pl.BlockSpec((blk, LANES), lambda b, p, i: (b * nq + i, p)),
                  pl.BlockSpec((seq, LANES), lambda b, p, i: (b, hp + p)),
                  pl.BlockSpec((seq, LANES), lambda b, p, i: (b, 2 * hp + p))],
        out_specs=pl.BlockSpec((blk, LANES), lambda b, p, i: (b * nq + i, p)),
        out_shape=jax.ShapeDtypeStruct((t, heads * dh), BF16),
        scratch_shapes=[pltpu.VMEM((seq, LANES), F32), pltpu.VMEM((2, seq, 2 * LANES), BF16),
                        pltpu.VMEM((seq // blk, LANES, blk), BF16), pltpu.VMEM((2, dh, blk), F32)],
        compiler_params=_cparams("parallel", "parallel", "arbitrary"),
        name="fox_attention",
    )(f_bias, f_cols, qkv, qkv, qkv)


def _s5_tables(a_re, a_im, log_dt, b_re, b_im, c_re, c_im, d_skip):
    L = S5_CHUNK
    g, p = a_re.shape
    hc = b_re.shape[-1]
    dt = jnp.exp(log_dt)[:, None]
    mag = jnp.exp(a_re * dt)
    lb_re = mag * jnp.cos(a_im * dt)
    lb_im = mag * jnp.sin(a_im * dt)
    num_re = lb_re - 1.0
    num_im = lb_im
    den = a_re * a_re + a_im * a_im
    z_re = (num_re * a_re + num_im * a_im) / den
    z_im = (num_im * a_re - num_re * a_im) / den
    bb_re = z_re[..., None] * b_re - z_im[..., None] * b_im
    bb_im = z_re[..., None] * b_im + z_im[..., None] * b_re
    tau = jnp.arange(L + 1, dtype=F32)
    pmag = jnp.exp((a_re * dt)[..., None] * tau)
    pw_re = pmag * jnp.cos((a_im * dt)[..., None] * tau)
    pw_im = pmag * jnp.sin((a_im * dt)[..., None] * tau)
    cp_re = c_re[..., None] * pw_re[:, None] - c_im[..., None] * pw_im[:, None]
    cp_im = c_re[..., None] * pw_im[:, None] + c_im[..., None] * pw_re[:, None]
    kern = (jnp.einsum("gopt,gpi->gtoi", cp_re[..., :L], bb_re, precision=HIGHEST)
            - jnp.einsum("gopt,gpi->gtoi", cp_im[..., :L], bb_im, precision=HIGHEST))
    kern = kern.at[:, 0].add(d_skip[:, :, None] * jnp.eye(hc, dtype=F32))
    kern = jnp.concatenate([kern, jnp.zeros((g, 1, hc, hc), F32)], axis=1)
    s_idx = jnp.arange(L)[:, None]
    t_idx = jnp.arange(L)[None, :]
    lag = jnp.where(t_idx >= s_idx, t_idx - s_idx, L)
    toep = kern[:, lag]
    toep = toep.transpose(0, 1, 4, 2, 3).reshape(g, L * hc, L * hc)
    rev = (L - 1) - jnp.arange(L)
    e_re = pw_re[:, :, rev][..., None] * bb_re[:, :, None] - pw_im[:, :, rev][..., None] * bb_im[:, :, None]
    e_im = pw_re[:, :, rev][..., None] * bb_im[:, :, None] + pw_im[:, :, rev][..., None] * bb_re[:, :, None]
    e_re = e_re.transpose(0, 2, 3, 1).reshape(g, L * hc, p)
    e_im = e_im.transpose(0, 2, 3, 1).reshape(g, L * hc, p)
    o_re = cp_re[..., 1:].transpose(0, 2, 3, 1).reshape(g, p, L * hc)
    o_im = -cp_im[..., 1:].transpose(0, 2, 3, 1).reshape(g, p, L * hc)

    def pair_diag(m):
        m = m.reshape(g // 2, 2, *m.shape[1:])
        z = jnp.zeros_like(m[:, 0])
        top = jnp.concatenate([m[:, 0], z], axis=2)
        bot = jnp.concatenate([z, m[:, 1]], axis=2)
        return jnp.concatenate([top, bot], axis=1)

    toep_p = pair_diag(toep)
    w_end = jnp.concatenate([pair_diag(e_re), pair_diag(e_im)], axis=2)
    w_out = jnp.concatenate([pair_diag(o_re), pair_diag(o_im)], axis=1)
    lam_re = pw_re[..., L].reshape(g // 2, 2 * p)
    lam_im = pw_im[..., L].reshape(g // 2, 2 * p)
    return toep_p.astype(BF16), w_end.astype(BF16), w_out.astype(BF16), lam_re, lam_im


def _s5_end_kernel(u_ref, w_ref, ere_ref, eim_ref):
    e = jnp.dot(u_ref[0], w_ref[0], preferred_element_type=F32)
    half = e.shape[1] // 2
    ere_ref[0] = e[:, :half]
    eim_ref[0] = e[:, half:]


def _s5_scan_kernel(lre_ref, lim_ref, ere_ref, eim_ref, hre_ref, him_ref, *, batch, nchunk, rows):
    lr = lre_ref[...]
    li = lim_ref[...]
    npair = lr.shape[0]

    def step(j, carry):
        out = []
        for b in range(batch):
            hr, hi = carry[2 * b], carry[2 * b + 1]
            sl = pl.ds(b * nchunk + j, npair, stride=rows)
            hre_ref[sl, :] = hr
            him_ref[sl, :] = hi
            er = ere_ref[sl, :]
            ei = eim_ref[sl, :]
            out += [lr * hr - li * hi + er, lr * hi + li * hr + ei]
        return tuple(out)

    z = jnp.zeros_like(lr)
    lax.fori_loop(0, nchunk, step, (z,) * (2 * batch))


def _s5_out_kernel(u_ref, t_ref, hre_ref, him_ref, w_ref, y_ref):
    half = w_ref.shape[1] // 2
    y = jnp.dot(u_ref[0], t_ref[0], preferred_element_type=F32)
    y = y + jnp.dot(hre_ref[0].astype(BF16), w_ref[0, :half, :], preferred_element_type=F32)
    y = y + jnp.dot(him_ref[0].astype(BF16), w_ref[0, half:, :], preferred_element_type=F32)
    y_ref[0] = jax.nn.gelu(y)


def _s5(u, tables, batch, seq, groups, hc):
    toep, w_end, w_out, lam_re, lam_im = tables
    L = S5_CHUNK
    t = u.shape[0]
    rows = t // L
    nchunk = seq // L
    npair = groups // 2
    kc = 2 * L * hc
    pw = lam_re.shape[1]
    ug = u.reshape(rows, L, npair, 2, hc).transpose(2, 0, 3, 1, 4).reshape(npair, rows, kc)
    blk3 = lambda a, b: pl.BlockSpec((1, a, b), lambda i: (i, 0, 0))
    e_re, e_im = pl.pallas_call(
        _s5_end_kernel,
        grid=(npair,),
        in_specs=[blk3(rows, kc), blk3(kc, 2 * pw)],
        out_specs=[blk3(rows, pw)] * 2,
        out_shape=[jax.ShapeDtypeStruct((npair, rows, pw), F32)] * 2,
        compiler_params=_cparams("parallel"),
        name="s5_chunk_end",
    )(ug, w_end)
    pg = 8
    lam_spec = pl.BlockSpec((pg, pw), lambda i: (i, 0))
    st_spec = pl.BlockSpec((pg * rows, pw), lambda i: (i, 0))
    h_re, h_im = pl.pallas_call(
        functools.partial(_s5_scan_kernel, batch=batch, nchunk=nchunk, rows=rows),
        grid=(npair // pg,),
        in_specs=[lam_spec, lam_spec, st_spec, st_spec],
        out_specs=[st_spec] * 2,
        out_shape=[jax.ShapeDtypeStruct((npair * rows, pw), F32)] * 2,
        compiler_params=_cparams("parallel"),
        name="s5_chunk_scan",
    )(lam_re, lam_im, e_re.reshape(npair * rows, pw), e_im.reshape(npair * rows, pw))
    y = pl.pallas_call(
        _s5_out_kernel,
        grid=(npair,),
        in_specs=[blk3(rows, kc), blk3(kc, kc), blk3(rows, pw), blk3(rows, pw), blk3(2 * pw, kc)],
        out_specs=blk3(rows, kc),
        out_shape=jax.ShapeDtypeStruct((npair, rows, kc), F32),
        compiler_params=_cparams("parallel"),
        name="s5_out",
    )(ug, toep, h_re.reshape(npair, rows, pw), h_im.reshape(npair, rows, pw), w_out)
    return y.reshape(npair, rows, 2, L, hc).transpose(1, 3, 0, 2, 4).reshape(t, groups * hc)


def _rows_to_cols(eye3, sub, a_row, b_row):
    a3 = jnp.concatenate(_split3(a_row), axis=1)
    b3 = jnp.concatenate(_split3(b_row), axis=1)
    rows = jnp.where(sub == 0, a3, jnp.where(sub == 1, b3, 0.0)).astype(BF16)
    cols = lax.dot_general(eye3, rows, NT_DIMS, preferred_element_type=F32)
    return cols[:, 0:1], cols[:, 1:2]


def _mlstm_kernel(gb_ref, g_ref, q_ref, k_ref, v_ref, o_ref, h_ref, c_ref, b_scr, i_scr, *, L, seq, dh, nh):
    tri = (_iota((L, L), 0) <= _iota((L, L), 1)).astype(F32)
    eye = jnp.concatenate([(_iota((L, L), 0) == _iota((L, L), 1)).astype(BF16)] * 3, axis=1)
    causal = _iota((L, L), 1) <= _iota((L, L), 0)
    lane = _iota((L, LANES), 1)
    one_col = (lane == 0).astype(BF16)
    sub = _iota((8, 3 * L), 0)
    c_ref[...] = jnp.zeros_like(c_ref)
    for hh in range(nh):
        log_f = _log_sigmoid(g_ref[0, 0, hh, 1] + gb_ref[0, hh, 1])
        b_scr[hh] = jnp.dot(log_f, tri, precision=HIGHEST, preferred_element_type=F32)
        i_scr[hh] = g_ref[0, 0, hh, 0] + gb_ref[0, hh, 0]

    def chunk(c, carry):
        st = pl.multiple_of(c * L, L)
        heads = range(nh)
        cols_h = [slice(hh * dh, (hh + 1) * dh) for hh in heads]
        b_row = [b_scr[hh, pl.ds(c, 1), :] for hh in heads]
        li_row = [i_scr[hh, pl.ds(c, 1), :] for hh in heads]
        cols = [_rows_to_cols(eye, sub, b_row[hh], li_row[hh]) for hh in heads]
        q = [q_ref[pl.ds(st, L), cols_h[hh]] for hh in heads]
        k = [k_ref[pl.ds(st, L), cols_h[hh]] for hh in heads]
        v = [v_ref[pl.ds(st, L), cols_h[hh]] for hh in heads]
        qk = [lax.dot_general(q[hh], k[hh], NT_DIMS, preferred_element_type=F32) for hh in heads]
        qc = [jnp.dot(q[hh], c_ref[hh].astype(BF16), preferred_element_type=F32) for hh in heads]

        s, m_t, m_inter, m_new, w_col, decay = [], [], [], [], [], []
        for hh in heads:
            m_prev = carry[hh]
            b_col, li_col = cols[hh]
            b_last = b_row[hh][:, L - 1:L]
            log_d = jnp.where(causal, b_col - b_row[hh] + li_row[hh], NEG_INF)
            m_inter.append(b_col + m_prev)
            m_t.append(jnp.maximum(m_inter[hh], jnp.max(log_d, axis=1, keepdims=True)))
            s.append((qk[hh] * jnp.exp(log_d - m_t[hh])).astype(BF16))
            g_row = b_last - b_row[hh] + li_row[hh]
            m_new.append(jnp.maximum(b_last + m_prev, jnp.max(g_row, axis=1, keepdims=True)))
            w_col.append(jnp.exp(b_last - b_col + li_col - m_new[hh]))
            decay.append(jnp.exp(b_last + m_prev - m_new[hh]))

        tot = [jnp.dot(s[hh], jnp.concatenate([v[hh], one_col], axis=1), preferred_element_type=F32)
               for hh in heads]
        upd = [lax.dot_general(k[hh], jnp.concatenate(
            [(v[hh].astype(F32) * w_col[hh]).astype(BF16), jnp.where(lane == 0, w_col[hh], 0.0).astype(BF16)],
            axis=1), TN_DIMS, preferred_element_type=F32) for hh in heads]
        for hh in heads:
            t_h = tot[hh] + jnp.exp(m_inter[hh] - m_t[hh]) * qc[hh]
            den = jnp.maximum(jnp.abs(t_h[:, dh:dh + 1]), jnp.exp(-m_t[hh]))
            h = t_h[:, :dh] / den
            h_ref[pl.ds(st, L), cols_h[hh]] = (h * jax.nn.sigmoid(o_ref[pl.ds(st, L), cols_h[hh]])).astype(h_ref.dtype)
            c_ref[hh] = decay[hh] * c_ref[hh] + upd[hh]
        return tuple(m_new)

    lax.fori_loop(0, seq // L, chunk, (jnp.zeros((1, 1), F32),) * nh)


def _mlstm(q, k, v, o, gate_rows, gate_bias, batch, seq, heads, dh):
    t = q.shape[0]
    L = MLSTM_CHUNK
    nh = MLSTM_HEADS_PER_STEP
    nc = seq // L
    col = pl.BlockSpec((seq, nh * dh), lambda b, h: (b, h))
    return pl.pallas_call(
        functools.partial(_mlstm_kernel, L=L, seq=seq, dh=dh, nh=nh),
        grid=(batch, heads // nh),
        in_specs=[pl.BlockSpec((1, nh, 2, 1, 1), lambda b, h: (h, 0, 0, 0, 0)),
                  pl.BlockSpec((1, 1, nh, 2, nc, L), lambda b, h: (b, h, 0, 0, 0, 0)),
                  col, col, col, col],
        out_specs=col,
        out_shape=jax.ShapeDtypeStruct((t, heads * dh), BF16),
        scratch_shapes=[pltpu.VMEM((nh, dh, 2 * dh), F32), pltpu.VMEM((nh, nc, L), F32),
                        pltpu.VMEM((nh, nc, L), F32)],
        compiler_params=_cparams("parallel", "parallel"),
        name="mlstm",
    )(gate_bias, gate_rows, q, k, v, o)


def _out_even_kernel(h_ref, att_ref, ys_ref, wg_ref, bg_ref, wo_ref, g_ref, b_ref, o_ref, *, alpha):
    ys = ys_ref[...]
    half = att_ref.shape[1]
    gate = jax.nn.sigmoid(jnp.dot(ys.astype(BF16), wg_ref[...], preferred_element_type=F32) + bg_ref[...])
    mix = jnp.dot(att_ref[...], wo_ref[:half, :], preferred_element_type=F32)
    mix = mix + jnp.dot((ys * gate).astype(BF16), wo_ref[half:, :], preferred_element_type=F32)
    o_ref[...] = _layer_norm(alpha * h_ref[...] + mix, g_ref[...], b_ref[...])


def _out_odd_kernel(h_ref, hm_ref, wo_ref, g_ref, b_ref, o_ref, *, alpha):
    mix = jnp.dot(hm_ref[...], wo_ref[...], preferred_element_type=F32)
    o_ref[...] = _layer_norm(alpha * h_ref[...] + mix, g_ref[...], b_ref[...])


def _row_tiled_call(kern, row_args, fixed_args, out_dtype, name):
    t = row_args[0].shape[0]
    tm = TM_PROJ
    in_specs = [pl.BlockSpec((tm, a.shape[1]), lambda i: (i, 0)) for a in row_args]
    in_specs += [pl.BlockSpec(a.shape, lambda i: (0, 0)) for a in fixed_args]
    d = row_args[0].shape[1]
    return pl.pallas_call(
        kern,
        grid=(t // tm,),
        in_specs=in_specs,
        out_specs=pl.BlockSpec((tm, d), lambda i: (i, 0)),
        out_shape=jax.ShapeDtypeStruct((t, d), out_dtype),
        compiler_params=_cparams("parallel"),
        name=name,
    )(*row_args, *fixed_args)


def _router_kernel(h_ref, w_ref, b_ref, o_ref, cnt_ref, run_ref, *, n_groups, epg):
    @pl.when(pl.program_id(0) == 0)
    def _():
        run_ref[...] = jnp.zeros_like(run_ref)

    h = h_ref[...]
    logits = jnp.dot(h, w_ref[...], precision=HIGHEST, preferred_element_type=F32) + b_ref[...]
    lane = _iota(logits.shape, 1)
    big = jnp.int32(LANES)
    lg = jnp.where(lane < n_groups, logits, NEG_INF)
    mg = jnp.max(lg, axis=1, keepdims=True)
    g_val = 1.0 / jnp.sum(jnp.exp(lg - mg), axis=1, keepdims=True)
    g_idx = jnp.min(jnp.where(lg == mg, lane, big), axis=1, keepdims=True)
    lo_lane = n_groups + g_idx * epg
    le = jnp.where((lane >= lo_lane) & (lane < lo_lane + epg), logits, NEG_INF)
    m1 = jnp.max(le, axis=1, keepdims=True)
    i1 = jnp.min(jnp.where(le == m1, lane, big), axis=1, keepdims=True)
    le2 = jnp.where(lane == i1, NEG_INF, le)
    m2 = jnp.max(le2, axis=1, keepdims=True)
    i2 = jnp.min(jnp.where(le2 == m2, lane, big), axis=1, keepdims=True)
    r = jnp.exp(m2 - m1)
    w1 = g_val / (1.0 + r)
    w2 = g_val * r / (1.0 + r)
    e1 = i1 - lo_lane
    e2 = i2 - lo_lane
    first_lo = e1 < e2
    lo = jnp.where(first_lo, e1, e2)
    hi = jnp.where(first_lo, e2, e1)
    w_lo = jnp.where(first_lo, w1, w2)
    w_hi = jnp.where(first_lo, w2, w1)
    pair = (lo * (2 * epg - 1 - lo)) // 2 + (hi - lo - 1)
    cls = g_idx * (epg * (epg - 1) // 2) + pair
    tm = h.shape[0]
    onehot = lane == cls
    earlier = (_iota((tm, tm), 1) < _iota((tm, tm), 0)).astype(BF16)
    before = jnp.dot(earlier, onehot.astype(BF16), preferred_element_type=F32) + run_ref[...]
    rank = jnp.sum(jnp.where(onehot, before, 0.0), axis=1, keepdims=True)
    run = run_ref[...] + jnp.sum(onehot.astype(F32), axis=0, keepdims=True)
    run_ref[...] = run
    cnt_ref[...] = run
    out = jnp.where(lane == 0, cls.astype(F32),
                    jnp.where(lane == 1, w_lo, jnp.where(lane == 2, w_hi, jnp.where(lane == 3, rank, 0.0))))
    o_ref[...] = out


def _router(h, w_r, b_r, n_groups, epg):
    t, d = h.shape
    tm = TM_PROJ
    return pl.pallas_call(
        functools.partial(_router_kernel, n_groups=n_groups, epg=epg),
        grid=(t // tm,),
        in_specs=[pl.BlockSpec((tm, d), lambda i: (i, 0)),
                  pl.BlockSpec(w_r.shape, lambda i: (0, 0)),
                  pl.BlockSpec(b_r.shape, lambda i: (0, 0))],
        out_specs=[pl.BlockSpec((tm, LANES), lambda i: (i, 0)), pl.BlockSpec((1, LANES), lambda i: (0, 0))],
        out_shape=[jax.ShapeDtypeStruct((t, LANES), F32), jax.ShapeDtypeStruct((1, LANES), F32)],
        scratch_shapes=[pltpu.VMEM((1, LANES), F32)],
        compiler_params=_cparams("arbitrary"),
        name="router",
    )(h, w_r, b_r)


def _row_wait(src, dst, sem, n):
    def body(r, c):
        pltpu.make_async_copy(src.at[pl.ds(0, 1)], dst.at[pl.ds(0, 1)], sem).wait()
        return c
    lax.fori_loop(0, n, body, 0)


def _dispatch_kernel(pos_ref, x_ref, r_ref, z_hbm, o_hbm, xbuf, sem, *, tm, nsteps):
    del z_hbm
    i = pl.program_id(0)
    slot = i % 2
    d = x_ref.shape[1]

    @pl.when(i >= 2)
    def _():
        _row_wait(xbuf.at[slot], o_hbm, sem.at[slot], tm)

    xbuf[slot, :, :d] = x_ref[...]
    xbuf[slot, :, d:] = r_ref[...]

    def body(r, c):
        pltpu.make_async_copy(xbuf.at[slot, pl.ds(r, 1)], o_hbm.at[pl.ds(pos_ref[0, 0, r], 1)], sem.at[slot]).start()
        return c
    lax.fori_loop(0, tm, body, 0, unroll=8)

    @pl.when(i == nsteps - 1)
    def _():
        _row_wait(xbuf.at[slot], o_hbm, sem.at[slot], tm)
        if nsteps > 1:
            _row_wait(xbuf.at[1 - slot], o_hbm, sem.at[1 - slot], tm)


def _dispatch(h, route, pos3, p_rows):
    t, d = h.shape
    w = d + route.shape[1]
    tm = pos3.shape[2]
    nsteps = t // tm
    return pl.pallas_call(
        functools.partial(_dispatch_kernel, tm=tm, nsteps=nsteps),
        grid=(nsteps,),
        in_specs=[pl.BlockSpec((1, 1, tm), lambda i: (i, 0, 0), memory_space=pltpu.SMEM),
                  pl.BlockSpec((tm, d), lambda i: (i, 0)), pl.BlockSpec((tm, route.shape[1]), lambda i: (i, 0)),
                  pl.BlockSpec(memory_space=pl.ANY)],
        out_specs=pl.BlockSpec(memory_space=pl.ANY),
        out_shape=jax.ShapeDtypeStruct((p_rows, w), F32),
        scratch_shapes=[pltpu.VMEM((2, tm, w), F32), pltpu.SemaphoreType.DMA((2,))],
        input_output_aliases={3: 0},
        compiler_params=_cparams("arbitrary"),
        name="moe_dispatch",
    )(pos3, h, route, jnp.zeros((p_rows, w), F32))


def _combine_kernel(pos_ref, nxt_ref, h_ref, g_ref, b_ref, y_hbm, o_ref, ybuf, sem, *, alpha, tm, nsteps):
    i = pl.program_id(0)
    slot = i % 2

    def gather(p_ref, s):
        def body(r, c):
            pltpu.make_async_copy(y_hbm.at[pl.ds(p_ref[0, 0, r], 1)], ybuf.at[s, pl.ds(r, 1)], sem.at[s]).start()
            return c
        lax.fori_loop(0, tm, body, 0, unroll=8)

    @pl.when(i == 0)
    def _():
        gather(pos_ref, 0)

    @pl.when(i + 1 < nsteps)
    def _():
        gather(nxt_ref, 1 - slot)

    _row_wait(y_hbm, ybuf.at[slot], sem.at[slot], tm)
    o_ref[...] = _layer_norm(alpha * h_ref[...] + ybuf[slot], g_ref[...], b_ref[...])


def _combine(h, pos3, y_sorted, ln_g, ln_b, alpha):
    t, d = h.shape
    tm = pos3.shape[2]
    nsteps = t // tm
    row = lambda i: (i, 0)
    fixed = lambda i: (0, 0)
    return pl.pallas_call(
        functools.partial(_combine_kernel, alpha=alpha, tm=tm, nsteps=nsteps),
        grid=(nsteps,),
        in_specs=[pl.BlockSpec((1, 1, tm), lambda i: (i, 0, 0), memory_space=pltpu.SMEM),
                  pl.BlockSpec((1, 1, tm), lambda i: (jnp.minimum(i + 1, nsteps - 1), 0, 0), memory_space=pltpu.SMEM),
                  pl.BlockSpec((tm, d), row),
                  pl.BlockSpec((1, d), fixed), pl.BlockSpec((1, d), fixed),
                  pl.BlockSpec(memory_space=pl.ANY)],
        out_specs=pl.BlockSpec((tm, d), row),
        out_shape=jax.ShapeDtypeStruct((t, d), F32),
        scratch_shapes=[pltpu.VMEM((2, tm, d), F32), pltpu.SemaphoreType.DMA((2,))],
        compiler_params=_cparams("arbitrary"),
        name="moe_combine_ln",
    )(pos3, pos3, h, ln_g.reshape(1, -1), ln_b.reshape(1, -1), y_sorted)


def _moe_kernel(elo_ref, ehi_ref, valid_ref, x_ref, g0_ref, u0_ref, d0_ref, g1_ref, u1_ref, d1_ref, y_ref):
    i = pl.program_id(0)
    d = y_ref.shape[1]

    @pl.when(valid_ref[i] == 0)
    def _():
        y_ref[...] = jnp.zeros_like(y_ref)

    @pl.when(valid_ref[i] != 0)
    def _():
        x = x_ref[:, :d].astype(BF16)
        route = x_ref[:, d:]
        y = None
        for j, (g_ref, u_ref, d_ref) in enumerate(((g0_ref, u0_ref, d0_ref), (g1_ref, u1_ref, d1_ref))):
            gate = jnp.dot(x, g_ref[0], preferred_element_type=F32)
            up = jnp.dot(x, u_ref[0], preferred_element_type=F32)
            hid = gate * jax.nn.sigmoid(gate) * up * route[:, j + 1:j + 2]
            part = jnp.dot(hid.astype(BF16), d_ref[0], preferred_element_type=F32)
            y = part if y is None else y + part
        y_ref[...] = y


def _moe_experts(x_sorted, e_lo, e_hi, valid, w_gate, w_up, w_down):
    p, xw = x_sorted.shape
    d = w_gate.shape[1]
    tm = TM_MOE
    f = w_gate.shape[2]
    lo_in = pl.BlockSpec((1, d, f), lambda i, lo, hi, va: (lo[i], 0, 0))
    hi_in = pl.BlockSpec((1, d, f), lambda i, lo, hi, va: (hi[i], 0, 0))
    lo_dn = pl.BlockSpec((1, f, d), lambda i, lo, hi, va: (lo[i], 0, 0))
    hi_dn = pl.BlockSpec((1, f, d), lambda i, lo, hi, va: (hi[i], 0, 0))
    grid_spec = pltpu.PrefetchScalarGridSpec(
        num_scalar_prefetch=3,
        grid=(p // tm,),
        in_specs=[pl.BlockSpec((tm, xw), lambda i, lo, hi, va: (i, 0)),
                  lo_in, lo_in, lo_dn, hi_in, hi_in, hi_dn],
        out_specs=pl.BlockSpec((tm, d), lambda i, lo, hi, va: (i, 0)),
    )
    return pl.pallas_call(
        _moe_kernel,
        grid_spec=grid_spec,
        out_shape=jax.ShapeDtypeStruct((p, d), F32),
        compiler_params=_cparams("arbitrary"),
        name="moe_experts",
    )(e_lo, e_hi, valid, x_sorted, w_gate, w_up, w_down, w_gate, w_up, w_down)


def _hier_moe_ln(h, ln_g, ln_b, alpha, w_group, b_group, w_expert, b_expert, w_gate, w_up, w_down):
    t, d = h.shape
    n_groups, _, epg = w_expert.shape
    npairs = epg * (epg - 1) // 2
    ncls = n_groups * npairs
    tm = TM_MOE
    w_r = jnp.concatenate([w_group, w_expert.transpose(1, 0, 2).reshape(d, n_groups * epg)], axis=1)
    b_r = jnp.concatenate([b_group, b_expert.reshape(-1)])
    pad = LANES - w_r.shape[1]
    w_r = jnp.pad(w_r, ((0, 0), (0, pad)))
    b_r = jnp.pad(b_r, (0, pad)).reshape(1, LANES)
    route, cnt = _router(h, w_r, b_r, n_groups, epg)

    cls = route[:, 0].astype(jnp.int32)
    rank = route[:, 3].astype(jnp.int32)
    counts = cnt[0, :ncls].astype(jnp.int32)
    padded = ((counts + tm - 1) // tm) * tm
    pend = jnp.cumsum(padded)
    pstart = pend - padded
    pos = jnp.sum(jnp.where(cls[:, None] == jnp.arange(ncls)[None, :], pstart[None, :], 0), axis=1) + rank
    pos3 = pos.astype(jnp.int32).reshape(t // TM_PROJ, 1, TM_PROJ)
    p_rows = t + ncls * tm
    x_sorted = _dispatch(h, route, pos3, p_rows)

    tile_start = jnp.arange(p_rows // tm, dtype=jnp.int32) * tm
    tile_cls = jnp.minimum(jnp.sum(pend[None, :] <= tile_start[:, None], axis=1), ncls - 1).astype(jnp.int32)
    valid = (tile_start < pend[-1]).astype(jnp.int32)
    lo_tab = jnp.asarray(PAIRS_LO, jnp.int32)
    hi_tab = jnp.asarray(PAIRS_HI, jnp.int32)
    e_lo = (tile_cls // npairs) * epg + lo_tab[tile_cls % npairs]
    e_hi = (tile_cls // npairs) * epg + hi_tab[tile_cls % npairs]

    y_sorted = _moe_experts(x_sorted, e_lo, e_hi, valid,
                            w_gate.astype(BF16), w_up.astype(BF16), w_down.astype(BF16))
    return _combine(h, pos3, y_sorted, ln_g, ln_b, alpha)


def _even_mixer(h, batch, seq, alpha, ln_g, ln_b, w_in, f_bias, s5_params, w_glu, b_glu, w_out):
    t, d = h.shape
    heads = f_bias.shape[0]
    groups, p_state = s5_params[0].shape
    hc = s5_params[3].shape[-1]
    s5_width = groups * hc
    fox_width = d - s5_width
    dh = fox_width // heads
    q_scale = dh ** -0.5
    w_q, w_k, w_v, w_f, w_u = jnp.split(w_in, [fox_width, 2 * fox_width, 3 * fox_width, 3 * fox_width + heads], axis=1)
    hp = heads // 2
    w_f = jnp.pad(w_f.reshape(d, hp, 2), ((0, 0), (0, 0), (0, LANES - 2))).reshape(d, hp * LANES)
    fb = jnp.pad(f_bias.reshape(hp, 1, 2), ((0, 0), (0, 0), (0, LANES - 2)))
    w_cat = jnp.concatenate([w_q * q_scale, w_k, w_v, w_u, w_f], axis=1)
    segs = ((0, 3 * fox_width), (3 * fox_width, s5_width), (3 * fox_width + s5_width, hp * LANES))
    qkv, u, f_cols = _proj(h, w_cat.astype(BF16), segs, (BF16, BF16, F32))
    att = _fox_attention(qkv, f_cols, fb, batch, seq, heads, dh)
    ys = _s5(u, _s5_tables(*s5_params), batch, seq, groups, hc)
    kern = functools.partial(_out_even_kernel, alpha=alpha)
    return _row_tiled_call(kern, (h, att, ys),
                           (w_glu.astype(BF16), b_glu.reshape(1, -1), w_out.astype(BF16),
                            ln_g.reshape(1, -1), ln_b.reshape(1, -1)), F32, "out_even")


def _odd_mixer(h, batch, seq, alpha, ln_g, ln_b, w_in, conv_w, conv_b, i_bias, f_bias, w_out):
    t, d = h.shape
    heads = i_bias.shape[0]
    dmix = conv_w.shape[1] // 2
    dh = dmix // heads
    w_main, w_gates = w_in[:, :4 * dmix], w_in[:, 4 * dmix:]
    w_cat = jnp.concatenate([w_main, jnp.pad(w_gates, ((0, 0), (0, LANES - 2 * heads)))], axis=1)
    q, k, v, o, gates = _proj_odd(h, w_cat.astype(BF16), conv_w, conv_b, seq, dh)
    nh = MLSTM_HEADS_PER_STEP
    g = gates[:, :2 * heads].reshape(batch, seq, 2, heads).transpose(0, 3, 2, 1)
    gate_rows = g.reshape(batch, heads // nh, nh, 2, seq // MLSTM_CHUNK, MLSTM_CHUNK)
    gate_bias = jnp.stack([i_bias, f_bias], axis=1).reshape(heads // nh, nh, 2, 1, 1)
    hm = _mlstm(q, k, v, o, gate_rows, gate_bias, batch, seq, heads, dh)
    kern = functools.partial(_out_odd_kernel, alpha=alpha)
    return _row_tiled_call(kern, (h, hm), (w_out.astype(BF16), ln_g.reshape(1, -1), ln_b.reshape(1, -1)),
                           F32, "out_odd")


def kernel(x, ln_g, ln_b, even_w_in, fox_f_bias, s5_a_re, s5_a_im, s5_log_dt, s5_b_re, s5_b_im, s5_c_re, s5_c_im, s5_d, s5_w_glu, s5_b_glu, even_w_out, odd_w_in, mlstm_conv_w, mlstm_conv_b, mlstm_i_bias, mlstm_f_bias, odd_w_out, moe_w_group, moe_b_group, moe_w_expert, moe_b_expert, moe_w_gate, moe_w_up, moe_w_down):
    batch, seq, d = x.shape
    depth = ln_g.shape[0]
    alpha = (2 * depth) ** 0.25
    h = x.reshape(batch * seq, d)
    for layer in range(depth):
        j = layer // 2
        if layer % 2 == 0:
            s5_params = (s5_a_re[j], s5_a_im[j], s5_log_dt[j], s5_b_re[j], s5_b_im[j],
                         s5_c_re[j], s5_c_im[j], s5_d[j])
            h = _even_mixer(h, batch, seq, alpha, ln_g[layer, 0], ln_b[layer, 0], even_w_in[j], fox_f_bias[j],
                            s5_params, s5_w_glu[j], s5_b_glu[j], even_w_out[j])
        else:
            h = _odd_mixer(h, batch, seq, alpha, ln_g[layer, 0], ln_b[layer, 0], odd_w_in[j], mlstm_conv_w[j],
                           mlstm_conv_b[j], mlstm_i_bias[j], mlstm_f_bias[j], odd_w_out[j])
        h = _hier_moe_ln(h, ln_g[layer, 1], ln_b[layer, 1], alpha, moe_w_group[layer], moe_b_group[layer],
                         moe_w_expert[layer], moe_b_expert[layer], moe_w_gate[layer], moe_w_up[layer],
                         moe_w_down[layer])
    return h.reshape(batch, seq, d)
```

```python
import functools

import jax
import jax.numpy as jnp
from jax import lax
from jax.experimental import pallas as pl
from jax.experimental.pallas import tpu as pltpu

F32 = jnp.float32
BF16 = jnp.bfloat16
HIGHEST = lax.Precision.HIGHEST
LN_EPS = 1e-5
NEG_INF = float("-inf")

LANES = 128
VMEM_LIMIT = 56 * 1024 * 1024

TM_PROJ = 256
ATT_BLOCK = 256
S5_CHUNK = 16
MLSTM_CHUNK = 256
MLSTM_HEADS_PER_STEP = 2
TM_MOE = 256
PAIRS_LO = (0, 0, 0, 1, 1, 2)
PAIRS_HI = (1, 2, 3, 2, 3, 3)

NT_DIMS = (((1,), (1,)), ((), ()))
TN_DIMS = (((0,), (0,)), ((), ()))


def _cparams(*sem):
    return pltpu.CompilerParams(dimension_semantics=sem, vmem_limit_bytes=VMEM_LIMIT)


def _log_sigmoid(x):
    return jnp.minimum(x, 0.0) - jnp.log1p(jnp.exp(-jnp.abs(x)))


def _layer_norm(x, g, b):
    mu = jnp.mean(x, axis=-1, keepdims=True)
    xc = x - mu
    var = jnp.mean(xc * xc, axis=-1, keepdims=True)
    return xc * lax.rsqrt(var + LN_EPS) * g + b


def _iota(shape, dim):
    return lax.broadcasted_iota(jnp.int32, shape, dim)


def _proj_even_kernel(x_ref, w_ref, qkv_ref, u_ref, f_ref, zs_ref, *, n_qkv, n_u, n_f):
    xb = x_ref[...].astype(BF16)
    qkv_ref[...] = jnp.dot(xb, w_ref[:, :n_qkv], preferred_element_type=F32).astype(BF16)
    f_ref[...] = jnp.dot(xb, w_ref[:, n_qkv + n_u:n_qkv + n_u + n_f], preferred_element_type=F32)
    z = jnp.dot(xb, w_ref[:, n_qkv:n_qkv + n_u], preferred_element_type=F32)
    L = S5_CHUNK
    nchunk = x_ref.shape[0] // L
    for q in range(n_u // LANES):
        zs_ref[q] = z[:, q * LANES:(q + 1) * LANES]
        for s in range(L):
            u_ref[q, :, s * LANES:(s + 1) * LANES] = zs_ref[q, pl.ds(s, nchunk, stride=L), :].astype(BF16)


def _proj_even(x, w, n_qkv, n_u, n_f):
    t, d = x.shape
    tm = TM_PROJ
    L = S5_CHUNK
    nq = n_u // LANES
    return pl.pallas_call(
        functools.partial(_proj_even_kernel, n_qkv=n_qkv, n_u=n_u, n_f=n_f),
        grid=(t // tm,),
        in_specs=[pl.BlockSpec((tm, d), lambda i: (i, 0)),
                  pl.BlockSpec(w.shape, lambda i: (0, 0))],
        out_specs=[pl.BlockSpec((tm, n_qkv), lambda i: (i, 0)),
                   pl.BlockSpec((nq, tm // L, L * LANES), lambda i: (0, i, 0)),
                   pl.BlockSpec((tm, n_f), lambda i: (i, 0))],
        out_shape=[jax.ShapeDtypeStruct((t, n_qkv), BF16),
                   jax.ShapeDtypeStruct((nq, t // L, L * LANES), BF16),
                   jax.ShapeDtypeStruct((t, n_f), F32)],
        scratch_shapes=[pltpu.VMEM((nq, tm, LANES), F32)],
        compiler_params=_cparams("parallel"),
        name="proj_even",
    )(x, w)


def _proj_odd_kernel(x_ref, w_ref, cw_ref, cb_ref, q_ref, k_ref, v_ref, o_ref, g_ref, zs_ref,
                     *, tm, dmix, k_scale, tiles_per_seq, conv_width):
    i = pl.program_id(0)
    xb = x_ref[...].astype(BF16)

    @pl.when(i % tiles_per_seq == 0)
    def _():
        zs_ref[0:8, :] = jnp.zeros((8, 2 * dmix), F32)

    cw = 512
    g_ref[...] = jnp.dot(xb, w_ref[:, 4 * dmix:4 * dmix + LANES], preferred_element_type=F32)
    for c0 in range(0, 2 * dmix, cw):
        zs_ref[8:tm + 8, c0:c0 + cw] = jnp.dot(xb, w_ref[:, c0:c0 + cw], preferred_element_type=F32)
        vo = jnp.dot(xb, w_ref[:, 2 * dmix + c0:2 * dmix + c0 + cw], preferred_element_type=F32)
        if c0 < dmix:
            v_ref[:, c0:c0 + cw] = vo.astype(BF16)
        else:
            o_ref[:, c0 - dmix:c0 - dmix + cw] = vo
        cols = slice(c0, c0 + cw)
        acc = jnp.broadcast_to(cb_ref[:, cols], (tm, cw))
        for j in range(conv_width):
            acc = acc + cw_ref[j:j + 1, cols] * zs_ref[pl.ds(8 - (conv_width - 1) + j, tm), cols]
        y = acc * jax.nn.sigmoid(acc)
        if c0 < dmix:
            q_ref[:, cols] = y.astype(BF16)
        else:
            k_ref[:, c0 - dmix:c0 - dmix + cw] = (y * k_scale).astype(BF16)
        zs_ref[0:8, cols] = zs_ref[tm:tm + 8, cols]


def _proj_odd(x, w, conv_w, conv_b, seq, head_dim):
    t, d = x.shape
    dmix = conv_w.shape[1] // 2
    tm = TM_PROJ
    kern = functools.partial(_proj_odd_kernel, tm=tm, dmix=dmix, k_scale=head_dim ** -0.5,
                             tiles_per_seq=seq // tm, conv_width=conv_w.shape[0])
    row = lambda i: (i, 0)
    fixed = lambda i: (0, 0)
    return pl.pallas_call(
        kern,
        grid=(t // tm,),
        in_specs=[pl.BlockSpec((tm, d), row), pl.BlockSpec(w.shape, fixed),
                  pl.BlockSpec(conv_w.shape, fixed), pl.BlockSpec((1, 2 * dmix), fixed)],
        out_specs=[pl.BlockSpec((tm, dmix), row)] * 4 + [pl.BlockSpec((tm, LANES), row)],
        out_shape=[jax.ShapeDtypeStruct((t, dmix), BF16)] * 3
        + [jax.ShapeDtypeStruct((t, dmix), F32), jax.ShapeDtypeStruct((t, LANES), F32)],
        scratch_shapes=[pltpu.VMEM((tm + 8, 2 * dmix), F32)],
        compiler_params=_cparams("arbitrary"),
        name="proj_odd",
    )(x, w, conv_w, conv_b.reshape(1, -1))


def _split3(x):
    hi = x.astype(BF16).astype(F32)
    r = x - hi
    mid = r.astype(BF16).astype(F32)
    lo = (r - mid).astype(BF16).astype(F32)
    return hi, mid, lo


def _fox_kernel(fb_ref, f_ref, q_ref, k_ref, v_ref, o_ref, c_ref, kaug_ref, vt_ref, acc_ref, *, blk, seq, dh):
    qi = pl.program_id(2)
    lane = _iota((blk, LANES), 1)
    head_lanes = (lane < dh, lane >= dh)

    @pl.when(qi == 0)
    def _():
        tril = (_iota((blk, blk), 1) <= _iota((blk, blk), 0)).astype(F32)
        eye_b = (_iota((LANES, LANES), 0) == _iota((LANES, LANES), 1)).astype(BF16)

        def prep(jb, carry):
            rows = pl.ds(pl.multiple_of(jb * blk, blk), blk)
            ls = _log_sigmoid(f_ref[rows, :] + fb_ref[0])
            c = jnp.dot(tril, ls, precision=HIGHEST, preferred_element_type=F32) + carry
            c_ref[rows, :] = c
            kblk = k_ref[rows, :]
            for j in range(2):
                hi, mid, lo = _split3(c[:, j:j + 1])
                aug = jnp.where(lane < 3, 1.0, jnp.where(lane == 3, -hi, jnp.where(
                    lane == 4, -mid, jnp.where(lane == 5, -lo, 0.0))))
                kaug_ref[j, rows, :] = jnp.concatenate(
                    [jnp.where(head_lanes[j], kblk, jnp.zeros_like(kblk)), aug.astype(BF16)], axis=1)
            vt_ref[jb] = lax.dot_general(eye_b, v_ref[rows, :], NT_DIMS,
                                         preferred_element_type=F32).astype(BF16)
            return c[blk - 1:blk, :]

        lax.fori_loop(0, seq // blk, prep, jnp.zeros((1, LANES), F32))

    q = q_ref[...]
    c_q = c_ref[pl.ds(pl.multiple_of(qi * blk, blk), blk), :]
    q_aug = []
    for j in range(2):
        hi, mid, lo = _split3(c_q[:, j:j + 1])
        aug = jnp.where(lane == 0, hi, jnp.where(lane == 1, mid, jnp.where(
            lane == 2, lo, jnp.where(lane < 6, 1.0, 0.0))))
        q_aug.append(jnp.concatenate([jnp.where(head_lanes[j], q, jnp.zeros_like(q)), aug.astype(BF16)], axis=1))
    key_le_query = _iota((blk, blk), 0) <= _iota((blk, blk), 1)
    acc_ref[...] = jnp.zeros_like(acc_ref)

    def score(kb):
        krows = pl.ds(pl.multiple_of(kb * blk, blk), blk)
        return tuple(lax.dot_general(kaug_ref[j, krows, :], q_aug[j], NT_DIMS, preferred_element_type=F32)
                     for j in range(2))

    def block(kb, scores, carry, masked):
        vt = vt_ref[kb]
        out, probs, alphas = [], [], []
        for j in range(2):
            m, l = carry[2 * j], carry[2 * j + 1]
            s = jnp.where(key_le_query, scores[j], NEG_INF) if masked else scores[j]
            m_new = jnp.maximum(m, jnp.max(s, axis=0, keepdims=True))
            alpha = jnp.exp(m - m_new)
            p = jnp.exp(s - m_new)
            out += [m_new, alpha * l + jnp.sum(p, axis=0, keepdims=True)]
            probs.append(p.astype(BF16))
            alphas.append(alpha)
        for j in range(2):
            acc_ref[j] = alphas[j] * acc_ref[j] + jnp.dot(vt[j * dh:(j + 1) * dh, :], probs[j],
                                                          preferred_element_type=F32)
        return tuple(out)

    m0 = jnp.full((1, blk), NEG_INF, F32)
    l0 = jnp.zeros((1, blk), F32)

    def step(kb, state):
        nxt = score(kb + 1)
        return nxt, block(kb, state[0], state[1], False)

    scores, carry = lax.fori_loop(0, qi, step, (score(0), (m0, l0, m0, l0)))
    carry = block(qi, scores, carry, True)
    out_t = jnp.concatenate([acc_ref[0] / carry[1], acc_ref[1] / carry[3]], axis=0).astype(BF16)
    eye_q = (_iota((blk, blk), 0) == _iota((blk, blk), 1)).astype(BF16)
    o_ref[...] = lax.dot_general(eye_q, out_t, NT_DIMS, preferred_element_type=F32).astype(o_ref.dtype)


def _fox_attention(qkv, f_cols, f_bias, batch, seq, heads, dh):
    t = qkv.shape[0]
    blk = ATT_BLOCK
    nq = seq // blk
    hp = heads * dh // LANES
    kern = functools.partial(_fox_kernel, blk=blk, seq=seq, dh=dh)
    return pl.pallas_call(
        kern,
        grid=(batch, hp, nq),
        in_specs=[pl.BlockSpec((1, 1, LANES), lambda b, p, i: (p, 0, 0)),
                  pl.BlockSpec((seq, LANES), lambda b, p, i: (b, p)),
                  pl.BlockSpec((blk, LANES), lambda b, p, i: (b * nq + i, p)),
                  pl.BlockSpec((seq, LANES), lambda b, p, i: (b, hp + p)),
                  pl.BlockSpec((seq, LANES), lambda b, p, i: (b, 2 * hp + p))],
        out_specs=pl.BlockSpec((blk, LANES), lambda b, p, i: (b * nq + i, p)),
        out_shape=jax.ShapeDtypeStruct((t, heads * dh), BF16),
        scratch_shapes=[pltpu.VMEM((seq, LANES), F32), pltpu.VMEM((2, seq, 2 * LANES), BF16),
                        pltpu.VMEM((seq // blk, LANES, blk), BF16), pltpu.VMEM((2, dh, blk), F32)],
        compiler_params=_cparams("parallel", "parallel", "arbitrary"),
        name="fox_attention",
    )(f_bias, f_cols, qkv, qkv, qkv)


def _s5_tables(a_re, a_im, log_dt, b_re, b_im, c_re, c_im, d_skip):
    L = S5_CHUNK
    g, p = a_re.shape
    hc = b_re.shape[-1]
    dt = jnp.exp(log_dt)[:, None]
    mag = jnp.exp(a_re * dt)
    lb_re = mag * jnp.cos(a_im * dt)
    lb_im = mag * jnp.sin(a_im * dt)
    num_re = lb_re - 1.0
    num_im = lb_im
    den = a_re * a_re + a_im * a_im
    z_re = (num_re * a_re + num_im * a_im) / den
    z_im = (num_im * a_re - num_re * a_im) / den
    bb_re = z_re[..., None] * b_re - z_im[..., None] * b_im
    bb_im = z_re[..., None] * b_im + z_im[..., None] * b_re
    tau = jnp.arange(L + 1, dtype=F32)
    pmag = jnp.exp((a_re * dt)[..., None] * tau)
    pw_re = pmag * jnp.cos((a_im * dt)[..., None] * tau)
    pw_im = pmag * jnp.sin((a_im * dt)[..., None] * tau)
    cp_re = c_re[..., None] * pw_re[:, None] - c_im[..., None] * pw_im[:, None]
    cp_im = c_re[..., None] * pw_im[:, None] + c_im[..., None] * pw_re[:, None]
    kern = (jnp.einsum("gopt,gpi->gtoi", cp_re[..., :L], bb_re, precision=HIGHEST)
            - jnp.einsum("gopt,gpi->gtoi", cp_im[..., :L], bb_im, precision=HIGHEST))
    kern = kern.at[:, 0].add(d_skip[:, :, None] * jnp.eye(hc, dtype=F32))
    kern = jnp.concatenate([kern, jnp.zeros((g, 1, hc, hc), F32)], axis=1)
    s_idx = jnp.arange(L)[:, None]
    t_idx = jnp.arange(L)[None, :]
    lag = jnp.where(t_idx >= s_idx, t_idx - s_idx, L)
    toep = kern[:, lag].transpose(0, 1, 4, 2, 3)
    rev = (L - 1) - jnp.arange(L)
    e_re = pw_re[:, :, rev][..., None] * bb_re[:, :, None] - pw_im[:, :, rev][..., None] * bb_im[:, :, None]
    e_im = pw_re[:, :, rev][..., None] * bb_im[:, :, None] + pw_im[:, :, rev][..., None] * bb_re[:, :, None]
    e_re = e_re.transpose(0, 2, 3, 1)
    e_im = e_im.transpose(0, 2, 3, 1)
    o_re = cp_re[..., 1:].transpose(0, 2, 3, 1)
    o_im = -cp_im[..., 1:].transpose(0, 2, 3, 1)

    gq = LANES // hc
    nq = g // gq
    eye = jnp.eye(gq, dtype=F32)
    tile = lambda m: m.reshape(nq, gq, *m.shape[1:])
    toep_q = jnp.einsum("qgsito,gh->qsgitho", tile(toep), eye).reshape(nq, L * LANES, L * LANES)
    w_end = jnp.concatenate(
        [jnp.einsum("qgsip,gh->qsgihp", tile(e), eye).reshape(nq, L * LANES, gq * p) for e in (e_re, e_im)], axis=2)
    w_out = jnp.concatenate(
        [jnp.einsum("qgpto,gh->qgptho", tile(o), eye).reshape(nq, gq * p, L * LANES) for o in (o_re, o_im)], axis=1)
    lam_re = pw_re[..., L].reshape(1, g * p)
    lam_im = pw_im[..., L].reshape(1, g * p)
    return toep_q.astype(BF16), w_end.astype(BF16), w_out.astype(BF16), lam_re, lam_im


def _s5_end_kernel(u_ref, w_ref, ere_ref, eim_ref):
    e = jnp.dot(u_ref[0], w_ref[0], preferred_element_type=F32)
    ns = ere_ref.shape[0]
    for k in range(ns):
        ere_ref[k] = e[:, k * LANES:(k + 1) * LANES]
        eim_ref[k] = e[:, (ns + k) * LANES:(ns + k + 1) * LANES]


def _s5_scan_kernel(lre_ref, lim_ref, ere_ref, eim_ref, hre_ref, him_ref, *, batch, nchunk):
    ns = ere_ref.shape[0]
    lr = [lre_ref[k] for k in range(ns)]
    li = [lim_ref[k] for k in range(ns)]

    def step(j, carry):
        sl = pl.ds(j, batch, stride=nchunk)
        out = []
        for k in range(ns):
            hr, hi = carry[2 * k], carry[2 * k + 1]
            hre_ref[k, sl, :] = hr
            him_ref[k, sl, :] = hi
            out += [lr[k] * hr - li[k] * hi + ere_ref[k, sl, :], lr[k] * hi + li[k] * hr + eim_ref[k, sl, :]]
        return tuple(out)

    z = jnp.zeros((batch, LANES), F32)
    lax.fori_loop(0, nchunk, step, (z,) * (2 * ns))


def _s5_out_kernel(u_ref, t_ref, hre_ref, him_ref, w_ref, y_ref):
    half = w_ref.shape[1] // 2
    ns = hre_ref.shape[0]
    h_re = jnp.concatenate([hre_ref[k] for k in range(ns)], axis=1).astype(BF16)
    h_im = jnp.concatenate([him_ref[k] for k in range(ns)], axis=1).astype(BF16)
    inter = jnp.dot(h_re, w_ref[0, :half, :], preferred_element_type=F32)
    inter = inter + jnp.dot(h_im, w_ref[0, half:, :], preferred_element_type=F32)
    ct = 2 * LANES
    for c0 in range(0, t_ref.shape[2], ct):
        k_hi = c0 + ct
        y = jnp.dot(u_ref[0, :, :k_hi], t_ref[0, :k_hi, c0:c0 + ct], preferred_element_type=F32)
        y_ref[0, :, c0:c0 + ct] = jax.nn.gelu(y + inter[:, c0:c0 + ct])


def _s5(u4, tables, batch, seq):
    toep, w_end, w_out, lam_re, lam_im = tables
    L = S5_CHUNK
    nq, rows, kc = u4.shape
    nchunk = seq // L
    sw = w_end.shape[2] // 2
    ns = sw // LANES
    rb = rows // 2
    e_re, e_im = pl.pallas_call(
        _s5_end_kernel,
        grid=(nq,),
        in_specs=[pl.BlockSpec((1, rows, kc), lambda q: (q, 0, 0)),
                  pl.BlockSpec((1, kc, 2 * sw), lambda q: (q, 0, 0))],
        out_specs=[pl.BlockSpec((ns, rows, LANES), lambda q: (q, 0, 0))] * 2,
        out_shape=[jax.ShapeDtypeStruct((nq * ns, rows, LANES), F32)] * 2,
        compiler_params=_cparams("parallel"),
        name="s5_chunk_end",
    )(u4, w_end)
    lam_spec = pl.BlockSpec((ns, 1, LANES), lambda q: (q, 0, 0))
    st_spec = pl.BlockSpec((ns, rows, LANES), lambda q: (q, 0, 0))
    h_re, h_im = pl.pallas_call(
        functools.partial(_s5_scan_kernel, batch=batch, nchunk=nchunk),
        grid=(nq,),
        in_specs=[lam_spec, lam_spec, st_spec, st_spec],
        out_specs=[st_spec] * 2,
        out_shape=[jax.ShapeDtypeStruct(e_re.shape, F32)] * 2,
        compiler_params=_cparams("parallel"),
        name="s5_chunk_scan",
    )(lam_re.reshape(nq * ns, 1, LANES), lam_im.reshape(nq * ns, 1, LANES), e_re, e_im)
    hs_spec = pl.BlockSpec((ns, rb, LANES), lambda q, r: (q, r, 0))
    return pl.pallas_call(
        _s5_out_kernel,
        grid=(nq, rows // rb),
        in_specs=[pl.BlockSpec((1, rb, kc), lambda q, r: (q, r, 0)),
                  pl.BlockSpec((1, kc, kc), lambda q, r: (q, 0, 0)),
                  hs_spec, hs_spec,
                  pl.BlockSpec((1, 2 * sw, kc), lambda q, r: (q, 0, 0))],
        out_specs=pl.BlockSpec((1, rb, kc), lambda q, r: (q, r, 0)),
        out_shape=jax.ShapeDtypeStruct((nq, rows, kc), F32),
        compiler_params=_cparams("parallel", "arbitrary"),
        name="s5_out",
    )(u4, toep, h_re, h_im, w_out)


def _rows_to_cols(eye3, sub, a_row, b_row):
    a3 = jnp.concatenate(_split3(a_row), axis=1)
    b3 = jnp.concatenate(_split3(b_row), axis=1)
    rows = jnp.where(sub == 0, a3, jnp.where(sub == 1, b3, 0.0)).astype(BF16)
    cols = lax.dot_general(eye3, rows, NT_DIMS, preferred_element_type=F32)
    return cols[:, 0:1], cols[:, 1:2]


def _mlstm_kernel(gb_ref, g_ref, q_ref, k_ref, v_ref, o_ref, h_ref, c_ref, b_scr, i_scr, *, L, seq, dh, nh):
    tri = (_iota((L, L), 0) <= _iota((L, L), 1)).astype(F32)
    eye = jnp.concatenate([(_iota((L, L), 0) == _iota((L, L), 1)).astype(BF16)] * 3, axis=1)
    causal = _iota((L, L), 1) <= _iota((L, L), 0)
    lane = _iota((L, LANES), 1)
    one_col = (lane == 0).astype(BF16)
    sub = _iota((8, 3 * L), 0)
    c_ref[...] = jnp.zeros_like(c_ref)
    for hh in range(nh):
        log_f = _log_sigmoid(g_ref[0, 0, hh, 1] + gb_ref[0, hh, 1])
        b_scr[hh] = jnp.dot(log_f, tri, precision=HIGHEST, preferred_element_type=F32)
        i_scr[hh] = g_ref[0, 0, hh, 0] + gb_ref[0, hh, 0]

    def chunk(c, carry):
        st = pl.multiple_of(c * L, L)
        heads = range(nh)
        cols_h = [slice(hh * dh, (hh + 1) * dh) for hh in heads]
        b_row = [b_scr[hh, pl.ds(c, 1), :] for hh in heads]
        li_row = [i_scr[hh, pl.ds(c, 1), :] for hh in heads]
        cols = [_rows_to_cols(eye, sub, b_row[hh], li_row[hh]) for hh in heads]
        q = [q_ref[pl.ds(st, L), cols_h[hh]] for hh in heads]
        k = [k_ref[pl.ds(st, L), cols_h[hh]] for hh in heads]
        v = [v_ref[pl.ds(st, L), cols_h[hh]] for hh in heads]
        qk = [lax.dot_general(q[hh], k[hh], NT_DIMS, preferred_element_type=F32) for hh in heads]
        qc = [jnp.dot(q[hh], c_ref[hh].astype(BF16), preferred_element_type=F32) for hh in heads]

        s, m_t, m_inter, m_new, w_col, decay = [], [], [], [], [], []
        for hh in heads:
            m_prev = carry[hh]
            b_col, li_col = cols[hh]
            b_last = b_row[hh][:, L - 1:L]
            log_d = jnp.where(causal, b_col - b_row[hh] + li_row[hh], NEG_INF)
            m_inter.append(b_col + m_prev)
            m_t.append(jnp.maximum(m_inter[hh], jnp.max(log_d, axis=1, keepdims=True)))
            s.append((qk[hh] * jnp.exp(log_d - m_t[hh])).astype(BF16))
            g_row = b_last - b_row[hh] + li_row[hh]
            m_new.append(jnp.maximum(b_last + m_prev, jnp.max(g_row, axis=1, keepdims=True)))
            w_col.append(jnp.exp(b_last - b_col + li_col - m_new[hh]))
            decay.append(jnp.exp(b_last + m_prev - m_new[hh]))

        tot = [jnp.dot(s[hh], jnp.concatenate([v[hh], one_col], axis=1), preferred_element_type=F32)
               for hh in heads]
        upd = [lax.dot_general(k[hh], jnp.concatenate(
            [(v[hh].astype(F32) * w_col[hh]).astype(BF16), jnp.where(lane == 0, w_col[hh], 0.0).astype(BF16)],
            axis=1), TN_DIMS, preferred_element_type=F32) for hh in heads]
        for hh in heads:
            t_h = tot[hh] + jnp.exp(m_inter[hh] - m_t[hh]) * qc[hh]
            den = jnp.maximum(jnp.abs(t_h[:, dh:dh + 1]), jnp.exp(-m_t[hh]))
            h = t_h[:, :dh] / den
            h_ref[pl.ds(st, L), cols_h[hh]] = (h * jax.nn.sigmoid(o_ref[pl.ds(st, L), cols_h[hh]])).astype(h_ref.dtype)
            c_ref[hh] = decay[hh] * c_ref[hh] + upd[hh]
        return tuple(m_new)

    lax.fori_loop(0, seq // L, chunk, (jnp.zeros((1, 1), F32),) * nh)


def _mlstm(q, k, v, o, gate_rows, gate_bias, batch, seq, heads, dh):
    t = q.shape[0]
    L = MLSTM_CHUNK
    nh = MLSTM_HEADS_PER_STEP
    nc = seq // L
    col = pl.BlockSpec((seq, nh * dh), lambda b, h: (b, h))
    return pl.pallas_call(
        functools.partial(_mlstm_kernel, L=L, seq=seq, dh=dh, nh=nh),
        grid=(batch, heads // nh),
        in_specs=[pl.BlockSpec((1, nh, 2, 1, 1), lambda b, h: (h, 0, 0, 0, 0)),
                  pl.BlockSpec((1, 1, nh, 2, nc, L), lambda b, h: (b, h, 0, 0, 0, 0)),
                  col, col, col, col],
        out_specs=col,
        out_shape=jax.ShapeDtypeStruct((t, heads * dh), BF16),
        scratch_shapes=[pltpu.VMEM((nh, dh, 2 * dh), F32), pltpu.VMEM((nh, nc, L), F32),
                        pltpu.VMEM((nh, nc, L), F32)],
        compiler_params=_cparams("parallel", "parallel"),
        name="mlstm",
    )(gate_bias, gate_rows, q, k, v, o)


def _out_even_kernel(h_ref, att_ref, ys4_ref, wg_ref, bg_ref, wo_ref, g_ref, b_ref, o_ref, ys_ref, *, alpha):
    L = S5_CHUNK
    nq = ys4_ref.shape[0]
    nchunk = ys_ref.shape[1] // L
    for q in range(nq):
        for t in range(L):
            ys_ref[q, pl.ds(t, nchunk, stride=L), :] = ys4_ref[q, :, t * LANES:(t + 1) * LANES]
    ys = jnp.concatenate([ys_ref[q] for q in range(nq)], axis=1)
    half = att_ref.shape[1]
    gate = jax.nn.sigmoid(jnp.dot(ys.astype(BF16), wg_ref[...], preferred_element_type=F32) + bg_ref[...])
    mix = jnp.dot(att_ref[...], wo_ref[:half, :], preferred_element_type=F32)
    mix = mix + jnp.dot((ys * gate).astype(BF16), wo_ref[half:, :], preferred_element_type=F32)
    o_ref[...] = _layer_norm(alpha * h_ref[...] + mix, g_ref[...], b_ref[...])


def _out_odd_kernel(h_ref, hm_ref, wo_ref, g_ref, b_ref, o_ref, *, alpha):
    mix = jnp.dot(hm_ref[...], wo_ref[...], preferred_element_type=F32)
    o_ref[...] = _layer_norm(alpha * h_ref[...] + mix, g_ref[...], b_ref[...])


def _row_tiled_call(kern, row_args, fixed_args, out_dtype, name):
    t = row_args[0].shape[0]
    tm = TM_PROJ
    in_specs = [pl.BlockSpec((tm, a.shape[1]), lambda i: (i, 0)) for a in row_args]
    in_specs += [pl.BlockSpec(a.shape, lambda i: (0, 0)) for a in fixed_args]
    d = row_args[0].shape[1]
    return pl.pallas_call(
        kern,
        grid=(t // tm,),
        in_specs=in_specs,
        out_specs=pl.BlockSpec((tm, d), lambda i: (i, 0)),
        out_shape=jax.ShapeDtypeStruct((t, d), out_dtype),
        compiler_params=_cparams("parallel"),
        name=name,
    )(*row_args, *fixed_args)


def _router_kernel(h_ref, w_ref, b_ref, o_ref, cnt_ref, run_ref, *, n_groups, epg):
    @pl.when(pl.program_id(0) == 0)
    def _():
        run_ref[...] = jnp.zeros_like(run_ref)

    h = h_ref[...]
    logits = jnp.dot(h, w_ref[...], precision=HIGHEST, preferred_element_type=F32) + b_ref[...]
    lane = _iota(logits.shape, 1)
    big = jnp.int32(LANES)
    lg = jnp.where(lane < n_groups, logits, NEG_INF)
    mg = jnp.max(lg, axis=1, keepdims=True)
    g_val = 1.0 / jnp.sum(jnp.exp(lg - mg), axis=1, keepdims=True)
    g_idx = jnp.min(jnp.where(lg == mg, lane, big), axis=1, keepdims=True)
    lo_lane = n_groups + g_idx * epg
    le = jnp.where((lane >= lo_lane) & (lane < lo_lane + epg), logits, NEG_INF)
    m1 = jnp.max(le, axis=1, keepdims=True)
    i1 = jnp.min(jnp.where(le == m1, lane, big), axis=1, keepdims=True)
    le2 = jnp.where(lane == i1, NEG_INF, le)
    m2 = jnp.max(le2, axis=1, keepdims=True)
    i2 = jnp.min(jnp.where(le2 == m2, lane, big), axis=1, keepdims=True)
    r = jnp.exp(m2 - m1)
    w1 = g_val / (1.0 + r)
    w2 = g_val * r / (1.0 + r)
    e1 = i1 - lo_lane
    e2 = i2 - lo_lane
    first_lo = e1 < e2
    lo = jnp.where(first_lo, e1, e2)
    hi = jnp.where(first_lo, e2, e1)
    w_lo = jnp.where(first_lo, w1, w2)
    w_hi = jnp.where(first_lo, w2, w1)
    pair = (lo * (2 * epg - 1 - lo)) // 2 + (hi - lo - 1)
    cls = g_idx * (epg * (epg - 1) // 2) + pair
    tm = h.shape[0]
    onehot = lane == cls
    earlier = (_iota((tm, tm), 1) < _iota((tm, tm), 0)).astype(BF16)
    before = jnp.dot(earlier, onehot.astype(BF16), preferred_element_type=F32) + run_ref[...]
    rank = jnp.sum(jnp.where(onehot, before, 0.0), axis=1, keepdims=True)
    run = run_ref[...] + jnp.sum(onehot.astype(F32), axis=0, keepdims=True)
    run_ref[...] = run
    cnt_ref[...] = run
    out = jnp.where(lane == 0, cls.astype(F32),
                    jnp.where(lane == 1, w_lo, jnp.where(lane == 2, w_hi, jnp.where(lane == 3, rank, 0.0))))
    o_ref[...] = out


def _router(h, w_r, b_r, n_groups, epg):
    t, d = h.shape
    tm = TM_PROJ
    return pl.pallas_call(
        functools.partial(_router_kernel, n_groups=n_groups, epg=epg),
        grid=(t // tm,),
        in_specs=[pl.BlockSpec((tm, d), lambda i: (i, 0)),
                  pl.BlockSpec(w_r.shape, lambda i: (0, 0)),
                  pl.BlockSpec(b_r.shape, lambda i: (0, 0))],
        out_specs=[pl.BlockSpec((tm, LANES), lambda i: (i, 0)), pl.BlockSpec((1, LANES), lambda i: (0, 0))],
        out_shape=[jax.ShapeDtypeStruct((t, LANES), F32), jax.ShapeDtypeStruct((1, LANES), F32)],
        scratch_shapes=[pltpu.VMEM((1, LANES), F32)],
        compiler_params=_cparams("arbitrary"),
        name="router",
    )(h, w_r, b_r)


def _rows_wait(buf, sem):
    pltpu.make_async_copy(buf, buf, sem).wait()


def _dispatch_kernel(pos_ref, x_ref, r_ref, z_hbm, o_hbm, xbuf, sem, *, tm, nsteps):
    del z_hbm
    i = pl.program_id(0)
    slot = i % 2
    d = x_ref.shape[1]

    @pl.when(i >= 2)
    def _():
        _rows_wait(xbuf.at[slot], sem.at[slot])

    xbuf[slot, :, :d] = x_ref[...]
    xbuf[slot, :, d:] = r_ref[...]

    def body(r, c):
        pltpu.make_async_copy(xbuf.at[slot, pl.ds(r, 1)], o_hbm.at[pl.ds(pos_ref[0, 0, r], 1)], sem.at[slot]).start()
        return c
    lax.fori_loop(0, tm, body, 0, unroll=8)

    @pl.when(i == nsteps - 1)
    def _():
        _rows_wait(xbuf.at[slot], sem.at[slot])
        if nsteps > 1:
            _rows_wait(xbuf.at[1 - slot], sem.at[1 - slot])


def _dispatch(h, route, pos3, p_rows):
    t, d = h.shape
    w = d + route.shape[1]
    tm = pos3.shape[2]
    nsteps = t // tm
    return pl.pallas_call(
        functools.partial(_dispatch_kernel, tm=tm, nsteps=nsteps),
        grid=(nsteps,),
        in_specs=[pl.BlockSpec((1, 1, tm), lambda i: (i, 0, 0), memory_space=pltpu.SMEM),
                  pl.BlockSpec((tm, d), lambda i: (i, 0)), pl.BlockSpec((tm, route.shape[1]), lambda i: (i, 0)),
                  pl.BlockSpec(memory_space=pl.ANY)],
        out_specs=pl.BlockSpec(memory_space=pl.ANY),
        out_shape=jax.ShapeDtypeStruct((p_rows, w), F32),
        scratch_shapes=[pltpu.VMEM((2, tm, w), F32), pltpu.SemaphoreType.DMA((2,))],
        input_output_aliases={3: 0},
        compiler_params=_cparams("arbitrary"),
        name="moe_dispatch",
    )(pos3, h, route, jnp.zeros((p_rows, w), F32))


def _combine_kernel(pos_ref, nxt_ref, h_ref, g_ref, b_ref, y_hbm, o_ref, ybuf, sem, *, alpha, tm, nsteps):
    i = pl.program_id(0)
    slot = i % 2

    def gather(p_ref, s):
        def body(r, c):
            pltpu.make_async_copy(y_hbm.at[pl.ds(p_ref[0, 0, r], 1)], ybuf.at[s, pl.ds(r, 1)], sem.at[s]).start()
            return c
        lax.fori_loop(0, tm, body, 0, unroll=8)

    @pl.when(i == 0)
    def _():
        gather(pos_ref, 0)

    @pl.when(i + 1 < nsteps)
    def _():
        gather(nxt_ref, 1 - slot)

    _rows_wait(ybuf.at[slot], sem.at[slot])
    o_ref[...] = _layer_norm(alpha * h_ref[...] + ybuf[slot], g_ref[...], b_ref[...])


def _combine(h, pos3, y_sorted, ln_g, ln_b, alpha):
    t, d = h.shape
    tm = pos3.shape[2]
    nsteps = t // tm
    row = lambda i: (i, 0)
    fixed = lambda i: (0, 0)
    return pl.pallas_call(
        functools.partial(_combine_kernel, alpha=alpha, tm=tm, nsteps=nsteps),
        grid=(nsteps,),
        in_specs=[pl.BlockSpec((1, 1, tm), lambda i: (i, 0, 0), memory_space=pltpu.SMEM),
                  pl.BlockSpec((1, 1, tm), lambda i: (jnp.minimum(i + 1, nsteps - 1), 0, 0), memory_space=pltpu.SMEM),
                  pl.BlockSpec((tm, d), row),
                  pl.BlockSpec((1, d), fixed), pl.BlockSpec((1, d), fixed),
                  pl.BlockSpec(memory_space=pl.ANY)],
        out_specs=pl.BlockSpec((tm, d), row),
        out_shape=jax.ShapeDtypeStruct((t, d), F32),
        scratch_shapes=[pltpu.VMEM((2, tm, d), F32), pltpu.SemaphoreType.DMA((2,))],
        compiler_params=_cparams("arbitrary"),
        name="moe_combine_ln",
    )(pos3, pos3, h, ln_g.reshape(1, -1), ln_b.reshape(1, -1), y_sorted)


def _moe_kernel(elo_ref, ehi_ref, nlive_ref, x_ref, g0_ref, u0_ref, d0_ref, g1_ref, u1_ref, d1_ref, y_ref,
                wg_ref, wu_ref, wd_ref):
    i = pl.program_id(0)
    d = y_ref.shape[1]
    live = i < nlive_ref[0]
    prev = jnp.maximum(i - 1, 0)
    experts = ((elo_ref, g0_ref, u0_ref, d0_ref), (ehi_ref, g1_ref, u1_ref, d1_ref))

    for j, (e_ref, g_ref, u_ref, d_ref) in enumerate(experts):
        @pl.when(live & ((i == 0) | (e_ref[i] != e_ref[prev])))
        def _():
            wg_ref[j] = g_ref[0].astype(BF16)
            wu_ref[j] = u_ref[0].astype(BF16)
            wd_ref[j] = d_ref[0].astype(BF16)

    @pl.when(jnp.logical_not(live))
    def _():
        y_ref[...] = jnp.zeros_like(y_ref)

    @pl.when(live)
    def _():
        x = x_ref[:, :d].astype(BF16)
        route = x_ref[:, d:]
        y = None
        for j in range(2):
            gate = jnp.dot(x, wg_ref[j], preferred_element_type=F32)
            up = jnp.dot(x, wu_ref[j], preferred_element_type=F32)
            hid = gate * jax.nn.sigmoid(gate) * up * route[:, j + 1:j + 2]
            part = jnp.dot(hid.astype(BF16), wd_ref[j], preferred_element_type=F32)
            y = part if y is None else y + part
        y_ref[...] = y


def _moe_experts(x_sorted, e_lo, e_hi, n_live, w_gate, w_up, w_down):
    p, xw = x_sorted.shape
    d = w_gate.shape[1]
    tm = TM_MOE
    f = w_gate.shape[2]
    lo_in = pl.BlockSpec((1, d, f), lambda i, lo, hi, nl: (lo[i], 0, 0))
    hi_in = pl.BlockSpec((1, d, f), lambda i, lo, hi, nl: (hi[i], 0, 0))
    lo_dn = pl.BlockSpec((1, f, d), lambda i, lo, hi, nl: (lo[i], 0, 0))
    hi_dn = pl.BlockSpec((1, f, d), lambda i, lo, hi, nl: (hi[i], 0, 0))
    grid_spec = pltpu.PrefetchScalarGridSpec(
        num_scalar_prefetch=3,
        grid=(p // tm,),
        in_specs=[pl.BlockSpec((tm, xw), lambda i, lo, hi, nl: (jnp.minimum(i, nl[0] - 1), 0)),
                  lo_in, lo_in, lo_dn, hi_in, hi_in, hi_dn],
        out_specs=pl.BlockSpec((tm, d), lambda i, lo, hi, nl: (i, 0)),
        scratch_shapes=[pltpu.VMEM((2, d, f), BF16), pltpu.VMEM((2, d, f), BF16), pltpu.VMEM((2, f, d), BF16)],
    )
    return pl.pallas_call(
        _moe_kernel,
        grid_spec=grid_spec,
        out_shape=jax.ShapeDtypeStruct((p, d), F32),
        compiler_params=_cparams("arbitrary"),
        name="moe_experts",
    )(e_lo, e_hi, n_live, x_sorted, w_gate, w_up, w_down, w_gate, w_up, w_down)


def _hier_moe_ln(h, ln_g, ln_b, alpha, w_group, b_group, w_expert, b_expert, w_gate, w_up, w_down):
    t, d = h.shape
    n_groups, _, epg = w_expert.shape
    npairs = epg * (epg - 1) // 2
    ncls = n_groups * npairs
    tm = TM_MOE
    w_r = jnp.concatenate([w_group, w_expert.transpose(1, 0, 2).reshape(d, n_groups * epg)], axis=1)
    b_r = jnp.concatenate([b_group, b_expert.reshape(-1)])
    pad = LANES - w_r.shape[1]
    w_r = jnp.pad(w_r, ((0, 0), (0, pad)))
    b_r = jnp.pad(b_r, (0, pad)).reshape(1, LANES)
    route, cnt = _router(h, w_r, b_r, n_groups, epg)

    cls = route[:, 0].astype(jnp.int32)
    rank = route[:, 3].astype(jnp.int32)
    counts = cnt[0, :ncls].astype(jnp.int32)
    padded = ((counts + tm - 1) // tm) * tm
    pend = jnp.cumsum(padded)
    pstart = pend - padded
    pos = jnp.sum(jnp.where(cls[:, None] == jnp.arange(ncls)[None, :], pstart[None, :], 0), axis=1) + rank
    pos3 = pos.astype(jnp.int32).reshape(t // TM_PROJ, 1, TM_PROJ)
    p_rows = t + ncls * tm
    x_sorted = _dispatch(h, route, pos3, p_rows)

    tile_start = jnp.arange(p_rows // tm, dtype=jnp.int32) * tm
    tile_start = jnp.minimum(tile_start, pend[-1] - tm)
    tile_cls = jnp.sum(pend[None, :] <= tile_start[:, None], axis=1).astype(jnp.int32)
    n_live = (pend[-1:] // tm).astype(jnp.int32)
    pair = tile_cls % npairs
    pair_lo = sum(jnp.where(pair == k, v, 0) for k, v in enumerate(PAIRS_LO))
    pair_hi = sum(jnp.where(pair == k, v, 0) for k, v in enumerate(PAIRS_HI))
    e_lo = ((tile_cls // npairs) * epg + pair_lo).astype(jnp.int32)
    e_hi = ((tile_cls // npairs) * epg + pair_hi).astype(jnp.int32)

    y_sorted = _moe_experts(x_sorted, e_lo, e_hi, n_live, w_gate, w_up, w_down)
    return _combine(h, pos3, y_sorted, ln_g, ln_b, alpha)


def _even_mixer(h, batch, seq, alpha, ln_g, ln_b, w_in, f_bias, s5_params, w_glu, b_glu, w_out):
    t, d = h.shape
    heads = f_bias.shape[0]
    groups, p_state = s5_params[0].shape
    hc = s5_params[3].shape[-1]
    s5_width = groups * hc
    fox_width = d - s5_width
    dh = fox_width // heads
    q_scale = dh ** -0.5
    w_q, w_k, w_v, w_f, w_u = jnp.split(w_in, [fox_width, 2 * fox_width, 3 * fox_width, 3 * fox_width + heads], axis=1)
    hp = heads // 2
    w_f = jnp.pad(w_f.reshape(d, hp, 2), ((0, 0), (0, 0), (0, LANES - 2))).reshape(d, hp * LANES)
    fb = jnp.pad(f_bias.reshape(hp, 1, 2), ((0, 0), (0, 0), (0, LANES - 2)))
    w_cat = jnp.concatenate([w_q * q_scale, w_k, w_v, w_u, w_f], axis=1)
    qkv, u4, f_cols = _proj_even(h, w_cat.astype(BF16), 3 * fox_width, s5_width, hp * LANES)
    att = _fox_attention(qkv, f_cols, fb, batch, seq, heads, dh)
    ys4 = _s5(u4, _s5_tables(*s5_params), batch, seq)
    tm = TM_PROJ
    row = lambda i: (i, 0)
    fixed = lambda i: (0, 0)
    fixed_args = (w_glu.astype(BF16), b_glu.reshape(1, -1), w_out.astype(BF16), ln_g.reshape(1, -1), ln_b.reshape(1, -1))
    return pl.pallas_call(
        functools.partial(_out_even_kernel, alpha=alpha),
        grid=(t // tm,),
        in_specs=[pl.BlockSpec((tm, d), row), pl.BlockSpec((tm, fox_width), row),
                  pl.BlockSpec((ys4.shape[0], tm // S5_CHUNK, ys4.shape[2]), lambda i: (0, i, 0))]
        + [pl.BlockSpec(a.shape, fixed) for a in fixed_args],
        out_specs=pl.BlockSpec((tm, d), row),
        out_shape=jax.ShapeDtypeStruct((t, d), F32),
        scratch_shapes=[pltpu.VMEM((ys4.shape[0], tm, LANES), F32)],
        compiler_params=_cparams("parallel"),
        name="out_even",
    )(h, att, ys4, *fixed_args)


def _odd_mixer(h, batch, seq, alpha, ln_g, ln_b, w_in, conv_w, conv_b, i_bias, f_bias, w_out):
    t, d = h.shape
    heads = i_bias.shape[0]
    dmix = conv_w.shape[1] // 2
    dh = dmix // heads
    w_main, w_gates = w_in[:, :4 * dmix], w_in[:, 4 * dmix:]
    w_cat = jnp.concatenate([w_main, jnp.pad(w_gates, ((0, 0), (0, LANES - 2 * heads)))], axis=1)
    q, k, v, o, gates = _proj_odd(h, w_cat.astype(BF16), conv_w, conv_b, seq, dh)
    nh = MLSTM_HEADS_PER_STEP
    g = gates[:, :2 * heads].reshape(batch, seq, 2, heads).transpose(0, 3, 2, 1)
    gate_rows = g.reshape(batch, heads // nh, nh, 2, seq // MLSTM_CHUNK, MLSTM_CHUNK)
    gate_bias = jnp.stack([i_bias, f_bias], axis=1).reshape(heads // nh, nh, 2, 1, 1)
    hm = _mlstm(q, k, v, o, gate_rows, gate_bias, batch, seq, heads, dh)
    kern = functools.partial(_out_odd_kernel, alpha=alpha)
    return _row_tiled_call(kern, (h, hm), (w_out.astype(BF16), ln_g.reshape(1, -1), ln_b.reshape(1, -1)),
                           F32, "out_odd")


def kernel(x, ln_g, ln_b, even_w_in, fox_f_bias, s5_a_re, s5_a_im, s5_log_dt, s5_b_re, s5_b_im, s5_c_re, s5_c_im, s5_d, s5_w_glu, s5_b_glu, even_w_out, odd_w_in, mlstm_conv_w, mlstm_conv_b, mlstm_i_bias, mlstm_f_bias, odd_w_out, moe_w_group, moe_b_group, moe_w_expert, moe_b_expert, moe_w_gate, moe_w_up, moe_w_down):
    batch, seq, d = x.shape
    depth = ln_g.shape[0]
    alpha = (2 * depth) ** 0.25
    h = x.reshape(batch * seq, d)
    for layer in range(depth):
        j = layer // 2
        if layer % 2 == 0:
            s5_params = (s5_a_re[j], s5_a_im[j], s5_log_dt[j], s5_b_re[j], s5_b_im[j],
                         s5_c_re[j], s5_c_im[j], s5_d[j])
            h = _even_mixer(h, batch, seq, alpha, ln_g[layer, 0], ln_b[layer, 0], even_w_in[j], fox_f_bias[j],
                            s5_params, s5_w_glu[j], s5_b_glu[j], even_w_out[j])
        else:
            h = _odd_mixer(h, batch, seq, alpha, ln_g[layer, 0], ln_b[layer, 0], odd_w_in[j], mlstm_conv_w[j],
                           mlstm_conv_b[j], mlstm_i_bias[j], mlstm_f_bias[j], odd_w_out[j])
        h = _hier_moe_ln(h, ln_g[layer, 1], ln_b[layer, 1], alpha, moe_w_group[layer], moe_b_group[layer],
                         moe_w_expert[layer], moe_b_expert[layer], moe_w_gate[layer], moe_w_up[layer],
                         moe_w_down[layer])
    return h.reshape(batch, seq, d)
```

```python
import functools

import jax
import jax.numpy as jnp
from jax import lax
from jax.experimental import pallas as pl
from jax.experimental.pallas import tpu as pltpu

F32 = jnp.float32
BF16 = jnp.bfloat16
HIGHEST = lax.Precision.HIGHEST
LN_EPS = 1e-5
NEG_INF = float("-inf")

LANES = 128
VMEM_LIMIT = 56 * 1024 * 1024

TM_PROJ = 256
ATT_BLOCK = 256
S5_CHUNK = 16
MLSTM_CHUNK = 256
MLSTM_HEADS_PER_STEP = 4
TM_MOE = 256
PAIRS_LO = (0, 0, 0, 1, 1, 2)
PAIRS_HI = (1, 2, 3, 2, 3, 3)

NT_DIMS = (((1,), (1,)), ((), ()))
TN_DIMS = (((0,), (0,)), ((), ()))


def _cparams(*sem):
    return pltpu.CompilerParams(dimension_semantics=sem, vmem_limit_bytes=VMEM_LIMIT)


def _log_sigmoid(x):
    return jnp.minimum(x, 0.0) - jnp.log1p(jnp.exp(-jnp.abs(x)))


def _layer_norm(x, g, b):
    mu = jnp.mean(x, axis=-1, keepdims=True)
    xc = x - mu
    var = jnp.mean(xc * xc, axis=-1, keepdims=True)
    return xc * lax.rsqrt(var + LN_EPS) * g + b


def _iota(shape, dim):
    return lax.broadcasted_iota(jnp.int32, shape, dim)


def _proj_even_kernel(x_ref, w_ref, qkv_ref, u_ref, f_ref, zs_ref, *, n_qkv, n_u, n_f):
    xb = x_ref[...].astype(BF16)
    qkv_ref[...] = jnp.dot(xb, w_ref[:, :n_qkv], preferred_element_type=F32).astype(BF16)
    f_ref[...] = jnp.dot(xb, w_ref[:, n_qkv + n_u:n_qkv + n_u + n_f], preferred_element_type=F32)
    z = jnp.dot(xb, w_ref[:, n_qkv:n_qkv + n_u], preferred_element_type=F32)
    L = S5_CHUNK
    nchunk = x_ref.shape[0] // L
    for q in range(n_u // LANES):
        zs_ref[q] = z[:, q * LANES:(q + 1) * LANES]
        for s in range(L):
            u_ref[q, :, s * LANES:(s + 1) * LANES] = zs_ref[q, pl.ds(s, nchunk, stride=L), :].astype(BF16)


def _proj_even(x, w, n_qkv, n_u, n_f):
    t, d = x.shape
    tm = TM_PROJ
    L = S5_CHUNK
    nq = n_u // LANES
    return pl.pallas_call(
        functools.partial(_proj_even_kernel, n_qkv=n_qkv, n_u=n_u, n_f=n_f),
        grid=(t // tm,),
        in_specs=[pl.BlockSpec((tm, d), lambda i: (i, 0)),
                  pl.BlockSpec(w.shape, lambda i: (0, 0))],
        out_specs=[pl.BlockSpec((tm, n_qkv), lambda i: (i, 0)),
                   pl.BlockSpec((nq, tm // L, L * LANES), lambda i: (0, i, 0)),
                   pl.BlockSpec((tm, n_f), lambda i: (i, 0))],
        out_shape=[jax.ShapeDtypeStruct((t, n_qkv), BF16),
                   jax.ShapeDtypeStruct((nq, t // L, L * LANES), BF16),
                   jax.ShapeDtypeStruct((t, n_f), F32)],
        scratch_shapes=[pltpu.VMEM((nq, tm, LANES), F32)],
        compiler_params=_cparams("parallel"),
        name="proj_even",
    )(x, w)


def _proj_odd_kernel(x_ref, w_ref, cw_ref, cb_ref, q_ref, k_ref, v_ref, o_ref, g_ref, zs_ref,
                     *, tm, dmix, k_scale, tiles_per_seq, conv_width):
    i = pl.program_id(0)
    xb = x_ref[...].astype(BF16)

    @pl.when(i % tiles_per_seq == 0)
    def _():
        zs_ref[0:8, :] = jnp.zeros((8, 2 * dmix), F32)

    cw = 512
    g_ref[...] = jnp.dot(xb, w_ref[:, 4 * dmix:4 * dmix + LANES], preferred_element_type=F32)
    for c0 in range(0, 2 * dmix, cw):
        zs_ref[8:tm + 8, c0:c0 + cw] = jnp.dot(xb, w_ref[:, c0:c0 + cw], preferred_element_type=F32)
        vo = jnp.dot(xb, w_ref[:, 2 * dmix + c0:2 * dmix + c0 + cw], preferred_element_type=F32)
        if c0 < dmix:
            v_ref[:, c0:c0 + cw] = vo.astype(BF16)
        else:
            o_ref[:, c0 - dmix:c0 - dmix + cw] = vo
        cols = slice(c0, c0 + cw)
        acc = jnp.broadcast_to(cb_ref[:, cols], (tm, cw))
        for j in range(conv_width):
            acc = acc + cw_ref[j:j + 1, cols] * zs_ref[pl.ds(8 - (conv_width - 1) + j, tm), cols]
        y = acc * jax.nn.sigmoid(acc)
        if c0 < dmix:
            q_ref[:, cols] = y.astype(BF16)
        else:
            k_ref[:, c0 - dmix:c0 - dmix + cw] = (y * k_scale).astype(BF16)
        zs_ref[0:8, cols] = zs_ref[tm:tm + 8, cols]


def _proj_odd(x, w, conv_w, conv_b, seq, head_dim):
    t, d = x.shape
    dmix = conv_w.shape[1] // 2
    tm = TM_PROJ
    kern = functools.partial(_proj_odd_kernel, tm=tm, dmix=dmix, k_scale=head_dim ** -0.5,
                             tiles_per_seq=seq // tm, conv_width=conv_w.shape[0])
    row = lambda i: (i, 0)
    fixed = lambda i: (0, 0)
    return pl.pallas_call(
        kern,
        grid=(t // tm,),
        in_specs=[pl.BlockSpec((tm, d), row), pl.BlockSpec(w.shape, fixed),
                  pl.BlockSpec(conv_w.shape, fixed), pl.BlockSpec((1, 2 * dmix), fixed)],
        out_specs=[pl.BlockSpec((tm, dmix), row)] * 4 + [pl.BlockSpec((tm, LANES), row)],
        out_shape=[jax.ShapeDtypeStruct((t, dmix), BF16)] * 3
        + [jax.ShapeDtypeStruct((t, dmix), F32), jax.ShapeDtypeStruct((t, LANES), F32)],
        scratch_shapes=[pltpu.VMEM((tm + 8, 2 * dmix), F32)],
        compiler_params=_cparams("arbitrary"),
        name="proj_odd",
    )(x, w, conv_w, conv_b.reshape(1, -1))


def _split3(x):
    hi = x.astype(BF16).astype(F32)
    r = x - hi
    mid = r.astype(BF16).astype(F32)
    lo = (r - mid).astype(BF16).astype(F32)
    return hi, mid, lo


def _fox_kernel(fb_ref, f_ref, q_ref, k_ref, v_ref, o_ref, c_ref, kaug_ref, vt_ref, acc_ref, *, blk, seq, dh):
    qi = pl.program_id(2)
    lane = _iota((blk, LANES), 1)
    head_lanes = (lane < dh, lane >= dh)

    @pl.when(qi == 0)
    def _():
        tril = (_iota((blk, blk), 1) <= _iota((blk, blk), 0)).astype(BF16)
        eye_b = (_iota((LANES, LANES), 0) == _iota((LANES, LANES), 1)).astype(BF16)

        def prep(jb, carry):
            rows = pl.ds(pl.multiple_of(jb * blk, blk), blk)
            ls = _log_sigmoid(f_ref[rows, :] + fb_ref[0])
            c3 = jnp.dot(tril, jnp.concatenate(_split3(ls), axis=1).astype(BF16), preferred_element_type=F32)
            c = c3[:, :LANES] + c3[:, LANES:2 * LANES] + c3[:, 2 * LANES:] + carry
            c_ref[rows, :] = c
            kblk = k_ref[rows, :]
            for j in range(2):
                hi, mid, lo = _split3(c[:, j:j + 1])
                aug = jnp.where(lane < 3, 1.0, jnp.where(lane == 3, -hi, jnp.where(
                    lane == 4, -mid, jnp.where(lane == 5, -lo, 0.0))))
                kaug_ref[j, rows, :] = jnp.concatenate(
                    [jnp.where(head_lanes[j], kblk, jnp.zeros_like(kblk)), aug.astype(BF16)], axis=1)
            vt_ref[jb] = lax.dot_general(eye_b, v_ref[rows, :], NT_DIMS,
                                         preferred_element_type=F32).astype(BF16)
            return c[blk - 1:blk, :]

        lax.fori_loop(0, seq // blk, prep, jnp.zeros((1, LANES), F32))

    q = q_ref[...]
    c_q = c_ref[pl.ds(pl.multiple_of(qi * blk, blk), blk), :]
    q_aug = []
    for j in range(2):
        hi, mid, lo = _split3(c_q[:, j:j + 1])
        aug = jnp.where(lane == 0, hi, jnp.where(lane == 1, mid, jnp.where(
            lane == 2, lo, jnp.where(lane < 6, 1.0, 0.0))))
        q_aug.append(jnp.concatenate([jnp.where(head_lanes[j], q, jnp.zeros_like(q)), aug.astype(BF16)], axis=1))
    key_le_query = _iota((blk, blk), 0) <= _iota((blk, blk), 1)
    acc_ref[...] = jnp.zeros_like(acc_ref)

    def score(kb):
        krows = pl.ds(pl.multiple_of(kb * blk, blk), blk)
        return tuple(lax.dot_general(kaug_ref[j, krows, :], q_aug[j], NT_DIMS, preferred_element_type=F32)
                     for j in range(2))

    def accumulate(kb, probs, alphas):
        vt = vt_ref[kb]
        for j in range(2):
            acc_ref[j] = alphas[j] * acc_ref[j] + jnp.dot(vt[j * dh:(j + 1) * dh, :], probs[j],
                                                          preferred_element_type=F32)

    def softmax(scores, stats, masked):
        out, probs, alphas = [], [], []
        for j in range(2):
            m, l = stats[2 * j], stats[2 * j + 1]
            s = jnp.where(key_le_query, scores[j], NEG_INF) if masked else scores[j]
            m_new = jnp.maximum(m, jnp.max(s, axis=0, keepdims=True))
            alpha = jnp.exp(m - m_new)
            p = jnp.exp(s - m_new)
            out += [m_new, alpha * l + jnp.sum(p, axis=0, keepdims=True)]
            probs.append(p.astype(BF16))
            alphas.append(alpha)
        return tuple(out), tuple(probs), tuple(alphas)

    m0 = jnp.full((1, blk), NEG_INF, F32)
    l0 = jnp.zeros((1, blk), F32)

    def step(kb, state):
        scores, stats = state
        nxt = score(kb + 1)
        stats, probs, alphas = softmax(scores, stats, False)
        accumulate(kb, probs, alphas)
        return nxt, stats

    scores, stats = lax.fori_loop(0, qi, step, (score(0), (m0, l0, m0, l0)))
    carry, probs, alphas = softmax(scores, stats, True)
    accumulate(qi, probs, alphas)
    out_t = jnp.concatenate([acc_ref[0] / carry[1], acc_ref[1] / carry[3]], axis=0).astype(BF16)
    eye_q = (_iota((blk, blk), 0) == _iota((blk, blk), 1)).astype(BF16)
    o_ref[...] = lax.dot_general(eye_q, out_t, NT_DIMS, preferred_element_type=F32).astype(o_ref.dtype)


def _fox_attention(qkv, f_cols, f_bias, batch, seq, heads, dh):
    t = qkv.shape[0]
    blk = ATT_BLOCK
    nq = seq // blk
    hp = heads * dh // LANES
    kern = functools.partial(_fox_kernel, blk=blk, seq=seq, dh=dh)
    return pl.pallas_call(
        kern,
        grid=(batch, hp, nq),
        in_specs=[pl.BlockSpec((1, 1, LANES), lambda b, p, i: (p, 0, 0)),
                  pl.BlockSpec((seq, LANES), lambda b, p, i: (b, p)),
                  pl.BlockSpec((blk, LANES), lambda b, p, i: (b * nq + i, p)),
                  pl.BlockSpec((seq, LANES), lambda b, p, i: (b, hp + p)),
                  pl.BlockSpec((seq, LANES), lambda b, p, i: (b, 2 * hp + p))],
        out_specs=pl.BlockSpec((blk, LANES), lambda b, p, i: (b * nq + i, p)),
        out_shape=jax.ShapeDtypeStruct((t, heads * dh), BF16),
        scratch_shapes=[pltpu.VMEM((seq, LANES), F32), pltpu.VMEM((2, seq, 2 * LANES), BF16),
                        pltpu.VMEM((seq // blk, LANES, blk), BF16), pltpu.VMEM((2, dh, blk), F32)],
        compiler_params=_cparams("parallel", "parallel", "arbitrary"),
        name="fox_attention",
    )(f_bias, f_cols, qkv, qkv, qkv)


def _s5_tables(a_re, a_im, log_dt, b_re, b_im, c_re, c_im, d_skip):
    L = S5_CHUNK
    g, p = a_re.shape
    hc = b_re.shape[-1]
    dt = jnp.exp(log_dt)[:, None]
    mag = jnp.exp(a_re * dt)
    lb_re = mag * jnp.cos(a_im * dt)
    lb_im = mag * jnp.sin(a_im * dt)
    num_re = lb_re - 1.0
    num_im = lb_im
    den = a_re * a_re + a_im * a_im
    z_re = (num_re * a_re + num_im * a_im) / den
    z_im = (num_im * a_re - num_re * a_im) / den
    bb_re = z_re[..., None] * b_re - z_im[..., None] * b_im
    bb_im = z_re[..., None] * b_im + z_im[..., None] * b_re
    tau = jnp.arange(L + 1, dtype=F32)
    pmag = jnp.exp((a_re * dt)[..., None] * tau)
    pw_re = pmag * jnp.cos((a_im * dt)[..., None] * tau)
    pw_im = pmag * jnp.sin((a_im * dt)[..., None] * tau)
    cp_re = c_re[..., None] * pw_re[:, None] - c_im[..., None] * pw_im[:, None]
    cp_im = c_re[..., None] * pw_im[:, None] + c_im[..., None] * pw_re[:, None]
    kern = (jnp.einsum("gopt,gpi->gtoi", cp_re[..., :L], bb_re, precision=HIGHEST)
            - jnp.einsum("gopt,gpi->gtoi", cp_im[..., :L], bb_im, precision=HIGHEST))
    kern = kern.at[:, 0].add(d_skip[:, :, None] * jnp.eye(hc, dtype=F32))
    rev = (L - 1) - jnp.arange(L)
    e_re = pw_re[:, :, rev][..., None] * bb_re[:, :, None] - pw_im[:, :, rev][..., None] * bb_im[:, :, None]
    e_im = pw_re[:, :, rev][..., None] * bb_im[:, :, None] + pw_im[:, :, rev][..., None] * bb_re[:, :, None]
    o_re = cp_re[..., 1:]
    o_im = -cp_im[..., 1:]

    gq = LANES // hc
    nq = g // gq
    eye = jnp.eye(gq, dtype=BF16)

    def tile(m, perm):
        m = m.astype(BF16).reshape(nq, gq, *m.shape[1:])
        return jnp.moveaxis(m, 1, perm)

    k_t = tile(kern.transpose(0, 1, 3, 2), 2)
    lag_blocks = (k_t[:, :, :, :, None, :] * eye[None, None, :, None, :, None]).reshape(nq, L, LANES, LANES)
    w_end = jnp.concatenate(
        [(tile(e.transpose(0, 2, 3, 1), 2)[:, :, :, :, None, :]
          * eye[None, None, :, None, :, None]).reshape(nq, L * LANES, gq * p) for e in (e_re, e_im)], axis=2)
    w_out = jnp.concatenate(
        [(tile(o.transpose(0, 2, 3, 1), 1)[:, :, :, :, None, :]
          * eye[None, :, None, None, :, None]).reshape(nq, gq * p, L * LANES) for o in (o_re, o_im)], axis=1)
    lam_re = pw_re[..., L].reshape(1, g * p)
    lam_im = pw_im[..., L].reshape(1, g * p)
    return lag_blocks, w_end, w_out, lam_re, lam_im


def _s5_end_kernel(u_ref, w_ref, ere_ref, eim_ref):
    e = jnp.dot(u_ref[0], w_ref[0], preferred_element_type=F32)
    ns = ere_ref.shape[0]
    for k in range(ns):
        ere_ref[k] = e[:, k * LANES:(k + 1) * LANES]
        eim_ref[k] = e[:, (ns + k) * LANES:(ns + k + 1) * LANES]


def _s5_scan_kernel(lre_ref, lim_ref, ere_ref, eim_ref, hre_ref, him_ref, *, batch, nchunk):
    ns = ere_ref.shape[0]
    lr = [lre_ref[k] for k in range(ns)]
    li = [lim_ref[k] for k in range(ns)]

    def step(j, carry):
        sl = pl.ds(j, batch, stride=nchunk)
        out = []
        for k in range(ns):
            hr, hi = carry[2 * k], carry[2 * k + 1]
            hre_ref[k, sl, :] = hr
            him_ref[k, sl, :] = hi
            out += [lr[k] * hr - li[k] * hi + ere_ref[k, sl, :], lr[k] * hi + li[k] * hr + eim_ref[k, sl, :]]
        return tuple(out)

    z = jnp.zeros((batch, LANES), F32)
    lax.fori_loop(0, nchunk, step, (z,) * (2 * ns))


def _s5_out_kernel(u_ref, lag_ref, hre_ref, him_ref, w_ref, y_ref, t_ref):
    L = lag_ref.shape[1]

    @pl.when(pl.program_id(1) == 0)
    def _():
        for s in range(L):
            for t in range(s, L):
                t_ref[s * LANES:(s + 1) * LANES, t * LANES:(t + 1) * LANES] = lag_ref[0, t - s]
            if s % 2 == 1:
                t_ref[s * LANES:(s + 1) * LANES, (s - 1) * LANES:s * LANES] = jnp.zeros((LANES, LANES), BF16)

    half = w_ref.shape[1] // 2
    ns = hre_ref.shape[0]
    h_re = jnp.concatenate([hre_ref[k] for k in range(ns)], axis=1).astype(BF16)
    h_im = jnp.concatenate([him_ref[k] for k in range(ns)], axis=1).astype(BF16)
    inter = jnp.dot(h_re, w_ref[0, :half, :], preferred_element_type=F32)
    inter = inter + jnp.dot(h_im, w_ref[0, half:, :], preferred_element_type=F32)
    ct = 2 * LANES
    for c0 in range(0, t_ref.shape[1], ct):
        k_hi = c0 + ct
        y = jnp.dot(u_ref[0, :, :k_hi], t_ref[:k_hi, c0:c0 + ct], preferred_element_type=F32)
        y_ref[0, :, c0:c0 + ct] = jax.nn.gelu(y + inter[:, c0:c0 + ct])


def _s5(u4, tables, batch, seq):
    lag_blocks, w_end, w_out, lam_re, lam_im = tables
    L = S5_CHUNK
    nq, rows, kc = u4.shape
    nchunk = seq // L
    sw = w_end.shape[2] // 2
    ns = sw // LANES
    rb = rows // 2
    e_re, e_im = pl.pallas_call(
        _s5_end_kernel,
        grid=(nq,),
        in_specs=[pl.BlockSpec((1, rows, kc), lambda q: (q, 0, 0)),
                  pl.BlockSpec((1, kc, 2 * sw), lambda q: (q, 0, 0))],
        out_specs=[pl.BlockSpec((ns, rows, LANES), lambda q: (q, 0, 0))] * 2,
        out_shape=[jax.ShapeDtypeStruct((nq * ns, rows, LANES), F32)] * 2,
        compiler_params=_cparams("parallel"),
        name="s5_chunk_end",
    )(u4, w_end)
    lam_spec = pl.BlockSpec((ns, 1, LANES), lambda q: (q, 0, 0))
    st_spec = pl.BlockSpec((ns, rows, LANES), lambda q: (q, 0, 0))
    h_re, h_im = pl.pallas_call(
        functools.partial(_s5_scan_kernel, batch=batch, nchunk=nchunk),
        grid=(nq,),
        in_specs=[lam_spec, lam_spec, st_spec, st_spec],
        out_specs=[st_spec] * 2,
        out_shape=[jax.ShapeDtypeStruct(e_re.shape, F32)] * 2,
        compiler_params=_cparams("parallel"),
        name="s5_chunk_scan",
    )(lam_re.reshape(nq * ns, 1, LANES), lam_im.reshape(nq * ns, 1, LANES), e_re, e_im)
    hs_spec = pl.BlockSpec((ns, rb, LANES), lambda q, r: (q, r, 0))
    return pl.pallas_call(
        _s5_out_kernel,
        grid=(nq, rows // rb),
        in_specs=[pl.BlockSpec((1, rb, kc), lambda q, r: (q, r, 0)),
                  pl.BlockSpec((1, L, LANES, LANES), lambda q, r: (q, 0, 0, 0)),
                  hs_spec, hs_spec,
                  pl.BlockSpec((1, 2 * sw, kc), lambda q, r: (q, 0, 0))],
        out_specs=pl.BlockSpec((1, rb, kc), lambda q, r: (q, r, 0)),
        out_shape=jax.ShapeDtypeStruct((nq, rows, kc), F32),
        scratch_shapes=[pltpu.VMEM((kc, kc), BF16)],
        compiler_params=_cparams("parallel", "arbitrary"),
        name="s5_out",
    )(u4, lag_blocks, h_re, h_im, w_out)


def _rows_to_cols(eye3, sub, a_row, b_row):
    a3 = jnp.concatenate(_split3(a_row), axis=1)
    b3 = jnp.concatenate(_split3(b_row), axis=1)
    rows = jnp.where(sub == 0, a3, jnp.where(sub == 1, b3, 0.0)).astype(BF16)
    cols = lax.dot_general(eye3, rows, NT_DIMS, preferred_element_type=F32)
    return cols[:, 0:1], cols[:, 1:2]


def _mlstm_kernel(gb_ref, g_ref, q_ref, k_ref, v_ref, o_ref, h_ref, c_ref, b_scr, i_scr, *, L, seq, dh, nh):
    tri = (_iota((L, L), 0) <= _iota((L, L), 1)).astype(F32)
    eye = jnp.concatenate([(_iota((L, L), 0) == _iota((L, L), 1)).astype(BF16)] * 3, axis=1)
    causal = _iota((L, L), 1) <= _iota((L, L), 0)
    lane = _iota((L, LANES), 1)
    one_col = (lane == 0).astype(BF16)
    sub = _iota((8, 3 * L), 0)
    c_ref[...] = jnp.zeros_like(c_ref)
    for hh in range(nh):
        log_f = _log_sigmoid(g_ref[0, 0, hh, 1] + gb_ref[0, hh, 1])
        b_scr[hh] = jnp.dot(log_f, tri, precision=HIGHEST, preferred_element_type=F32)
        i_scr[hh] = g_ref[0, 0, hh, 0] + gb_ref[0, hh, 0]

    def chunk(c, carry):
        st = pl.multiple_of(c * L, L)
        heads = range(nh)
        cols_h = [slice(hh * dh, (hh + 1) * dh) for hh in heads]
        b_row = [b_scr[hh, pl.ds(c, 1), :] for hh in heads]
        li_row = [i_scr[hh, pl.ds(c, 1), :] for hh in heads]
        cols = [_rows_to_cols(eye, sub, b_row[hh], li_row[hh]) for hh in heads]
        q = [q_ref[pl.ds(st, L), cols_h[hh]] for hh in heads]
        k = [k_ref[pl.ds(st, L), cols_h[hh]] for hh in heads]
        v = [v_ref[pl.ds(st, L), cols_h[hh]] for hh in heads]
        qk = [lax.dot_general(q[hh], k[hh], NT_DIMS, preferred_element_type=F32) for hh in heads]
        qc = [jnp.dot(q[hh], c_ref[hh].astype(BF16), preferred_element_type=F32) for hh in heads]

        s, m_t, m_inter, m_new, w_col, decay = [], [], [], [], [], []
        for hh in heads:
            m_prev = carry[hh]
            b_col, li_col = cols[hh]
            b_last = b_row[hh][:, L - 1:L]
            log_d = jnp.where(causal, b_col - b_row[hh] + li_row[hh], NEG_INF)
            m_inter.append(b_col + m_prev)
            m_t.append(jnp.maximum(m_inter[hh], jnp.max(log_d, axis=1, keepdims=True)))
            s.append((qk[hh] * jnp.exp(log_d - m_t[hh])).astype(BF16))
            g_row = b_last - b_row[hh] + li_row[hh]
            m_new.append(jnp.maximum(b_last + m_prev, jnp.max(g_row, axis=1, keepdims=True)))
            w_col.append(jnp.exp(b_last - b_col + li_col - m_new[hh]))
            decay.append(jnp.exp(b_last + m_prev - m_new[hh]))

        tot = [jnp.dot(s[hh], jnp.concatenate([v[hh], one_col], axis=1), preferred_element_type=F32)
               for hh in heads]
        upd = [lax.dot_general(k[hh], jnp.concatenate(
            [(v[hh].astype(F32) * w_col[hh]).astype(BF16), jnp.where(lane == 0, w_col[hh], 0.0).astype(BF16)],
            axis=1), TN_DIMS, preferred_element_type=F32) for hh in heads]
        for hh in heads:
            t_h = tot[hh] + jnp.exp(m_inter[hh] - m_t[hh]) * qc[hh]
            den = jnp.maximum(jnp.abs(t_h[:, dh:dh + 1]), jnp.exp(-m_t[hh]))
            h = t_h[:, :dh] / den
            h_ref[pl.ds(st, L), cols_h[hh]] = (h * jax.nn.sigmoid(o_ref[pl.ds(st, L), cols_h[hh]])).astype(h_ref.dtype)
            c_ref[hh] = decay[hh] * c_ref[hh] + upd[hh]
        return tuple(m_new)

    lax.fori_loop(0, seq // L, chunk, (jnp.zeros((1, 1), F32),) * nh)


def _mlstm(q, k, v, o, gate_rows, gate_bias, batch, seq, heads, dh):
    t = q.shape[0]
    L = MLSTM_CHUNK
    nh = MLSTM_HEADS_PER_STEP
    nc = seq // L
    col = pl.BlockSpec((seq, nh * dh), lambda b, h: (b, h))
    return pl.pallas_call(
        functools.partial(_mlstm_kernel, L=L, seq=seq, dh=dh, nh=nh),
        grid=(batch, heads // nh),
        in_specs=[pl.BlockSpec((1, nh, 2, 1, 1), lambda b, h: (h, 0, 0, 0, 0)),
                  pl.BlockSpec((1, 1, nh, 2, nc, L), lambda b, h: (b, h, 0, 0, 0, 0)),
                  col, col, col, col],
        out_specs=col,
        out_shape=jax.ShapeDtypeStruct((t, heads * dh), BF16),
        scratch_shapes=[pltpu.VMEM((nh, dh, 2 * dh), F32), pltpu.VMEM((nh, nc, L), F32),
                        pltpu.VMEM((nh, nc, L), F32)],
        compiler_params=_cparams("parallel", "parallel"),
        name="mlstm",
    )(gate_bias, gate_rows, q, k, v, o)


def _out_even_kernel(h_ref, att_ref, ys4_ref, wg_ref, bg_ref, wo_ref, g_ref, b_ref, o_ref, ys_ref, *, alpha):
    L = S5_CHUNK
    nq = ys4_ref.shape[0]
    nchunk = ys_ref.shape[1] // L
    for q in range(nq):
        for t in range(L):
            ys_ref[q, pl.ds(t, nchunk, stride=L), :] = ys4_ref[q, :, t * LANES:(t + 1) * LANES]
    ys = jnp.concatenate([ys_ref[q] for q in range(nq)], axis=1)
    half = att_ref.shape[1]
    gate = jax.nn.sigmoid(jnp.dot(ys.astype(BF16), wg_ref[...], preferred_element_type=F32) + bg_ref[...])
    mix = jnp.dot(att_ref[...], wo_ref[:half, :], preferred_element_type=F32)
    mix = mix + jnp.dot((ys * gate).astype(BF16), wo_ref[half:, :], preferred_element_type=F32)
    o_ref[...] = _layer_norm(alpha * h_ref[...] + mix, g_ref[...], b_ref[...])


def _out_odd_kernel(h_ref, hm_ref, wo_ref, g_ref, b_ref, o_ref, *, alpha):
    mix = jnp.dot(hm_ref[...], wo_ref[...], preferred_element_type=F32)
    o_ref[...] = _layer_norm(alpha * h_ref[...] + mix, g_ref[...], b_ref[...])


def _row_tiled_call(kern, row_args, fixed_args, out_dtype, name):
    t = row_args[0].shape[0]
    tm = TM_PROJ
    in_specs = [pl.BlockSpec((tm, a.shape[1]), lambda i: (i, 0)) for a in row_args]
    in_specs += [pl.BlockSpec(a.shape, lambda i: (0, 0)) for a in fixed_args]
    d = row_args[0].shape[1]
    return pl.pallas_call(
        kern,
        grid=(t // tm,),
        in_specs=in_specs,
        out_specs=pl.BlockSpec((tm, d), lambda i: (i, 0)),
        out_shape=jax.ShapeDtypeStruct((t, d), out_dtype),
        compiler_params=_cparams("parallel"),
        name=name,
    )(*row_args, *fixed_args)


def _router_kernel(h_ref, w_ref, b_ref, o_ref, cnt_ref, run_ref, *, n_groups, epg):
    @pl.when(pl.program_id(0) == 0)
    def _():
        run_ref[...] = jnp.zeros_like(run_ref)

    h = h_ref[...]
    logits = jnp.dot(h, w_ref[...], precision=HIGHEST, preferred_element_type=F32) + b_ref[...]
    lane = _iota(logits.shape, 1)
    big = jnp.int32(LANES)
    lg = jnp.where(lane < n_groups, logits, NEG_INF)
    mg = jnp.max(lg, axis=1, keepdims=True)
    g_val = 1.0 / jnp.sum(jnp.exp(lg - mg), axis=1, keepdims=True)
    g_idx = jnp.min(jnp.where(lg == mg, lane, big), axis=1, keepdims=True)
    lo_lane = n_groups + g_idx * epg
    le = jnp.where((lane >= lo_lane) & (lane < lo_lane + epg), logits, NEG_INF)
    m1 = jnp.max(le, axis=1, keepdims=True)
    i1 = jnp.min(jnp.where(le == m1, lane, big), axis=1, keepdims=True)
    le2 = jnp.where(lane == i1, NEG_INF, le)
    m2 = jnp.max(le2, axis=1, keepdims=True)
    i2 = jnp.min(jnp.where(le2 == m2, lane, big), axis=1, keepdims=True)
    r = jnp.exp(m2 - m1)
    w1 = g_val / (1.0 + r)
    w2 = g_val * r / (1.0 + r)
    e1 = i1 - lo_lane
    e2 = i2 - lo_lane
    first_lo = e1 < e2
    lo = jnp.where(first_lo, e1, e2)
    hi = jnp.where(first_lo, e2, e1)
    w_lo = jnp.where(first_lo, w1, w2)
    w_hi = jnp.where(first_lo, w2, w1)
    pair = (lo * (2 * epg - 1 - lo)) // 2 + (hi - lo - 1)
    cls = g_idx * (epg * (epg - 1) // 2) + pair
    tm = h.shape[0]
    onehot = lane == cls
    earlier = (_iota((tm, tm), 1) < _iota((tm, tm), 0)).astype(BF16)
    before = jnp.dot(earlier, onehot.astype(BF16), preferred_element_type=F32) + run_ref[...]
    rank = jnp.sum(jnp.where(onehot, before, 0.0), axis=1, keepdims=True)
    run = run_ref[...] + jnp.sum(onehot.astype(F32), axis=0, keepdims=True)
    run_ref[...] = run
    cnt_ref[...] = run
    out = jnp.where(lane == 0, cls.astype(F32),
                    jnp.where(lane == 1, w_lo, jnp.where(lane == 2, w_hi, jnp.where(lane == 3, rank, 0.0))))
    o_ref[...] = out


def _router(h, w_r, b_r, n_groups, epg):
    t, d = h.shape
    tm = TM_PROJ
    return pl.pallas_call(
        functools.partial(_router_kernel, n_groups=n_groups, epg=epg),
        grid=(t // tm,),
        in_specs=[pl.BlockSpec((tm, d), lambda i: (i, 0)),
                  pl.BlockSpec(w_r.shape, lambda i: (0, 0)),
                  pl.BlockSpec(b_r.shape, lambda i: (0, 0))],
        out_specs=[pl.BlockSpec((tm, LANES), lambda i: (i, 0)), pl.BlockSpec((1, LANES), lambda i: (0, 0))],
        out_shape=[jax.ShapeDtypeStruct((t, LANES), F32), jax.ShapeDtypeStruct((1, LANES), F32)],
        scratch_shapes=[pltpu.VMEM((1, LANES), F32)],
        compiler_params=_cparams("arbitrary"),
        name="router",
    )(h, w_r, b_r)


def _rows_wait(buf, sem):
    pltpu.make_async_copy(buf, buf, sem).wait()


def _dispatch_kernel(pos_ref, cnt_ref, pend_ref, x_ref, r_ref, o_hbm, xbuf, sem, zsem, *, tm, nsteps, ztile):
    i = pl.program_id(0)
    slot = i % 2
    d = x_ref.shape[1]

    @pl.when(i == 0)
    def _():
        zeros = xbuf.at[1, pl.ds(0, ztile)]
        xbuf[1] = jnp.zeros(xbuf.shape[1:], F32)
        ncls = cnt_ref.shape[0]
        used = pend_ref[ncls - 1]
        firsts = [(cnt_ref[c] > 0, pend_ref[c] - ztile) for c in range(ncls)]
        firsts += [(used + k * ztile < o_hbm.shape[0], used + k * ztile) for k in range(ncls)]
        for cond, first in firsts:
            @pl.when(cond)
            def _():
                pltpu.make_async_copy(zeros, o_hbm.at[pl.ds(pl.multiple_of(first, ztile), ztile)], zsem).start()
        for cond, _ in firsts:
            @pl.when(cond)
            def _():
                pltpu.make_async_copy(zeros, o_hbm.at[pl.ds(0, ztile)], zsem).wait()

    @pl.when(i >= 2)
    def _():
        _rows_wait(xbuf.at[slot], sem.at[slot])

    xbuf[slot, :, :d] = x_ref[...]
    xbuf[slot, :, d:] = r_ref[...]

    def body(r, c):
        pltpu.make_async_copy(xbuf.at[slot, pl.ds(r, 1)], o_hbm.at[pl.ds(pos_ref[0, 0, r], 1)], sem.at[slot]).start()
        return c
    lax.fori_loop(0, tm, body, 0, unroll=8)

    @pl.when(i == nsteps - 1)
    def _():
        _rows_wait(xbuf.at[slot], sem.at[slot])
        if nsteps > 1:
            _rows_wait(xbuf.at[1 - slot], sem.at[1 - slot])


def _dispatch(h, route, pos3, counts, pend, p_rows):
    t, d = h.shape
    w = d + route.shape[1]
    tm = pos3.shape[2]
    assert TM_MOE <= tm
    nsteps = t // tm
    smem = pl.BlockSpec(memory_space=pltpu.SMEM)
    return pl.pallas_call(
        functools.partial(_dispatch_kernel, tm=tm, nsteps=nsteps, ztile=TM_MOE),
        grid=(nsteps,),
        in_specs=[pl.BlockSpec((1, 1, tm), lambda i: (i, 0, 0), memory_space=pltpu.SMEM), smem, smem,
                  pl.BlockSpec((tm, d), lambda i: (i, 0)), pl.BlockSpec((tm, route.shape[1]), lambda i: (i, 0))],
        out_specs=pl.BlockSpec(memory_space=pl.ANY),
        out_shape=jax.ShapeDtypeStruct((p_rows, w), F32),
        scratch_shapes=[pltpu.VMEM((2, tm, w), F32), pltpu.SemaphoreType.DMA((2,)), pltpu.SemaphoreType.DMA(())],
        compiler_params=_cparams("arbitrary"),
        name="moe_dispatch",
    )(pos3, counts, pend, h, route)


def _combine_kernel(pos_ref, nxt_ref, h_ref, g_ref, b_ref, y_hbm, o_ref, ybuf, sem, *, alpha, tm, nsteps):
    i = pl.program_id(0)
    slot = i % 2

    def gather(p_ref, s):
        def body(r, c):
            pltpu.make_async_copy(y_hbm.at[pl.ds(p_ref[0, 0, r], 1)], ybuf.at[s, pl.ds(r, 1)], sem.at[s]).start()
            return c
        lax.fori_loop(0, tm, body, 0, unroll=8)

    @pl.when(i == 0)
    def _():
        gather(pos_ref, 0)

    @pl.when(i + 1 < nsteps)
    def _():
        gather(nxt_ref, 1 - slot)

    _rows_wait(ybuf.at[slot], sem.at[slot])
    o_ref[...] = _layer_norm(alpha * h_ref[...] + ybuf[slot], g_ref[...], b_ref[...])


def _combine(h, pos3, y_sorted, ln_g, ln_b, alpha):
    t, d = h.shape
    tm = pos3.shape[2]
    nsteps = t // tm
    row = lambda i: (i, 0)
    fixed = lambda i: (0, 0)
    return pl.pallas_call(
        functools.partial(_combine_kernel, alpha=alpha, tm=tm, nsteps=nsteps),
        grid=(nsteps,),
        in_specs=[pl.BlockSpec((1, 1, tm), lambda i: (i, 0, 0), memory_space=pltpu.SMEM),
                  pl.BlockSpec((1, 1, tm), lambda i: (jnp.minimum(i + 1, nsteps - 1), 0, 0), memory_space=pltpu.SMEM),
                  pl.BlockSpec((tm, d), row),
                  pl.BlockSpec((1, d), fixed), pl.BlockSpec((1, d), fixed),
                  pl.BlockSpec(memory_space=pl.ANY)],
        out_specs=pl.BlockSpec((tm, d), row),
        out_shape=jax.ShapeDtypeStruct((t, d), F32),
        scratch_shapes=[pltpu.VMEM((2, tm, d), F32), pltpu.SemaphoreType.DMA((2,))],
        compiler_params=_cparams("arbitrary"),
        name="moe_combine_ln",
    )(pos3, pos3, h, ln_g.reshape(1, -1), ln_b.reshape(1, -1), y_sorted)


def _moe_kernel(elo_ref, ehi_ref, nlive_ref, x_ref, g0_ref, u0_ref, d0_ref, g1_ref, u1_ref, d1_ref, y_ref,
                wg_ref, wu_ref, wd_ref):
    i = pl.program_id(0)
    d = y_ref.shape[1]
    live = i < nlive_ref[0]
    prev = jnp.maximum(i - 1, 0)
    experts = ((elo_ref, g0_ref, u0_ref, d0_ref), (ehi_ref, g1_ref, u1_ref, d1_ref))

    for j, (e_ref, g_ref, u_ref, d_ref) in enumerate(experts):
        @pl.when(live & ((i == 0) | (e_ref[i] != e_ref[prev])))
        def _():
            wg_ref[j] = g_ref[0].astype(BF16)
            wu_ref[j] = u_ref[0].astype(BF16)
            wd_ref[j] = d_ref[0].astype(BF16)

    @pl.when(jnp.logical_not(live))
    def _():
        y_ref[...] = jnp.zeros_like(y_ref)

    @pl.when(live)
    def _():
        x = x_ref[:, :d].astype(BF16)
        route = x_ref[:, d:]
        y = None
        for j in range(2):
            gate = jnp.dot(x, wg_ref[j], preferred_element_type=F32)
            up = jnp.dot(x, wu_ref[j], preferred_element_type=F32)
            hid = gate * jax.nn.sigmoid(gate) * up * route[:, j + 1:j + 2]
            part = jnp.dot(hid.astype(BF16), wd_ref[j], preferred_element_type=F32)
            y = part if y is None else y + part
        y_ref[...] = y


def _moe_experts(x_sorted, e_lo, e_hi, n_live, w_gate, w_up, w_down):
    p, xw = x_sorted.shape
    d = w_gate.shape[1]
    tm = TM_MOE
    f = w_gate.shape[2]
    lo_in = pl.BlockSpec((1, d, f), lambda i, lo, hi, nl: (lo[i], 0, 0))
    hi_in = pl.BlockSpec((1, d, f), lambda i, lo, hi, nl: (hi[i], 0, 0))
    lo_dn = pl.BlockSpec((1, f, d), lambda i, lo, hi, nl: (lo[i], 0, 0))
    hi_dn = pl.BlockSpec((1, f, d), lambda i, lo, hi, nl: (hi[i], 0, 0))
    grid_spec = pltpu.PrefetchScalarGridSpec(
        num_scalar_prefetch=3,
        grid=(p // tm,),
        in_specs=[pl.BlockSpec((tm, xw), lambda i, lo, hi, nl: (jnp.minimum(i, nl[0] - 1), 0)),
                  lo_in, lo_in, lo_dn, hi_in, hi_in, hi_dn],
        out_specs=pl.BlockSpec((tm, d), lambda i, lo, hi, nl: (i, 0)),
        scratch_shapes=[pltpu.VMEM((2, d, f), BF16), pltpu.VMEM((2, d, f), BF16), pltpu.VMEM((2, f, d), BF16)],
    )
    return pl.pallas_call(
        _moe_kernel,
        grid_spec=grid_spec,
        out_shape=jax.ShapeDtypeStruct((p, d), F32),
        compiler_params=_cparams("arbitrary"),
        name="moe_experts",
    )(e_lo, e_hi, n_live, x_sorted, w_gate, w_up, w_down, w_gate, w_up, w_down)


def _hier_moe_ln(h, ln_g, ln_b, alpha, w_group, b_group, w_expert, b_expert, w_gate, w_up, w_down):
    t, d = h.shape
    n_groups, _, epg = w_expert.shape
    npairs = epg * (epg - 1) // 2
    ncls = n_groups * npairs
    tm = TM_MOE
    w_r = jnp.concatenate([w_group, w_expert.transpose(1, 0, 2).reshape(d, n_groups * epg)], axis=1)
    b_r = jnp.concatenate([b_group, b_expert.reshape(-1)])
    pad = LANES - w_r.shape[1]
    w_r = jnp.pad(w_r, ((0, 0), (0, pad)))
    b_r = jnp.pad(b_r, (0, pad)).reshape(1, LANES)
    route, cnt = _router(h, w_r, b_r, n_groups, epg)

    cls = route[:, 0].astype(jnp.int32)
    rank = route[:, 3].astype(jnp.int32)
    counts = cnt[0, :ncls].astype(jnp.int32)
    padded = ((counts + tm - 1) // tm) * tm
    pend = jnp.cumsum(padded)
    pstart = pend - padded
    pos = jnp.sum(jnp.where(cls[:, None] == jnp.arange(ncls)[None, :], pstart[None, :], 0), axis=1) + rank
    pos3 = pos.astype(jnp.int32).reshape(t // TM_PROJ, 1, TM_PROJ)
    p_rows = t + ncls * tm
    x_sorted = _dispatch(h, route, pos3, counts, pend.astype(jnp.int32), p_rows)

    tile_start = jnp.arange(p_rows // tm, dtype=jnp.int32) * tm
    tile_start = jnp.minimum(tile_start, pend[-1] - tm)
    tile_cls = jnp.sum(pend[None, :] <= tile_start[:, None], axis=1).astype(jnp.int32)
    n_live = (pend[-1:] // tm).astype(jnp.int32)
    pair = tile_cls % npairs
    pair_lo = sum(jnp.where(pair == k, v, 0) for k, v in enumerate(PAIRS_LO))
    pair_hi = sum(jnp.where(pair == k, v, 0) for k, v in enumerate(PAIRS_HI))
    e_lo = ((tile_cls // npairs) * epg + pair_lo).astype(jnp.int32)
    e_hi = ((tile_cls // npairs) * epg + pair_hi).astype(jnp.int32)

    y_sorted = _moe_experts(x_sorted, e_lo, e_hi, n_live, w_gate, w_up, w_down)
    return _combine(h, pos3, y_sorted, ln_g, ln_b, alpha)


def _even_mixer(h, batch, seq, alpha, ln_g, ln_b, w_in, f_bias, s5_params, w_glu, b_glu, w_out):
    t, d = h.shape
    heads = f_bias.shape[0]
    groups, p_state = s5_params[0].shape
    hc = s5_params[3].shape[-1]
    s5_width = groups * hc
    fox_width = d - s5_width
    dh = fox_width // heads
    q_scale = dh ** -0.5
    w_q, w_k, w_v, w_f, w_u = jnp.split(w_in, [fox_width, 2 * fox_width, 3 * fox_width, 3 * fox_width + heads], axis=1)
    hp = heads // 2
    w_f = jnp.pad(w_f.reshape(d, hp, 2), ((0, 0), (0, 0), (0, LANES - 2))).reshape(d, hp * LANES)
    fb = jnp.pad(f_bias.reshape(hp, 1, 2), ((0, 0), (0, 0), (0, LANES - 2)))
    w_cat = jnp.concatenate([w_q * q_scale, w_k, w_v, w_u, w_f], axis=1)
    qkv, u4, f_cols = _proj_even(h, w_cat.astype(BF16), 3 * fox_width, s5_width, hp * LANES)
    att = _fox_attention(qkv, f_cols, fb, batch, seq, heads, dh)
    ys4 = _s5(u4, _s5_tables(*s5_params), batch, seq)
    tm = TM_PROJ
    row = lambda i: (i, 0)
    fixed = lambda i: (0, 0)
    fixed_args = (w_glu.astype(BF16), b_glu.reshape(1, -1), w_out.astype(BF16), ln_g.reshape(1, -1), ln_b.reshape(1, -1))
    return pl.pallas_call(
        functools.partial(_out_even_kernel, alpha=alpha),
        grid=(t // tm,),
        in_specs=[pl.BlockSpec((tm, d), row), pl.BlockSpec((tm, fox_width), row),
                  pl.BlockSpec((ys4.shape[0], tm // S5_CHUNK, ys4.shape[2]), lambda i: (0, i, 0))]
        + [pl.BlockSpec(a.shape, fixed) for a in fixed_args],
        out_specs=pl.BlockSpec((tm, d), row),
        out_shape=jax.ShapeDtypeStruct((t, d), F32),
        scratch_shapes=[pltpu.VMEM((ys4.shape[0], tm, LANES), F32)],
        compiler_params=_cparams("parallel"),
        name="out_even",
    )(h, att, ys4, *fixed_args)


def _odd_mixer(h, batch, seq, alpha, ln_g, ln_b, w_in, conv_w, conv_b, i_bias, f_bias, w_out):
    t, d = h.shape
    heads = i_bias.shape[0]
    dmix = conv_w.shape[1] // 2
    dh = dmix // heads
    w_main, w_gates = w_in[:, :4 * dmix], w_in[:, 4 * dmix:]
    w_cat = jnp.concatenate([w_main, jnp.pad(w_gates, ((0, 0), (0, LANES - 2 * heads)))], axis=1)
    q, k, v, o, gates = _proj_odd(h, w_cat.astype(BF16), conv_w, conv_b, seq, dh)
    nh = MLSTM_HEADS_PER_STEP
    g = gates[:, :2 * heads].reshape(batch, seq, 2, heads).transpose(0, 3, 2, 1)
    gate_rows = g.reshape(batch, heads // nh, nh, 2, seq // MLSTM_CHUNK, MLSTM_CHUNK)
    gate_bias = jnp.stack([i_bias, f_bias], axis=1).reshape(heads // nh, nh, 2, 1, 1)
    hm = _mlstm(q, k, v, o, gate_rows, gate_bias, batch, seq, heads, dh)
    kern = functools.partial(_out_odd_kernel, alpha=alpha)
    return _row_tiled_call(kern, (h, hm), (w_out.astype(BF16), ln_g.reshape(1, -1), ln_b.reshape(1, -1)),
                           F32, "out_odd")


def kernel(x, ln_g, ln_b, even_w_in, fox_f_bias, s5_a_re, s5_a_im, s5_log_dt, s5_b_re, s5_b_im, s5_c_re, s5_c_im, s5_d, s5_w_glu, s5_b_glu, even_w_out, odd_w_in, mlstm_conv_w, mlstm_conv_b, mlstm_i_bias, mlstm_f_bias, odd_w_out, moe_w_group, moe_b_group, moe_w_expert, moe_b_expert, moe_w_gate, moe_w_up, moe_w_down):
    batch, seq, d = x.shape
    depth = ln_g.shape[0]
    alpha = (2 * depth) ** 0.25
    h = x.reshape(batch * seq, d)
    for layer in range(depth):
        j = layer // 2
        if layer % 2 == 0:
            s5_params = (s5_a_re[j], s5_a_im[j], s5_log_dt[j], s5_b_re[j], s5_b_im[j],
                         s5_c_re[j], s5_c_im[j], s5_d[j])
            h = _even_mixer(h, batch, seq, alpha, ln_g[layer, 0], ln_b[layer, 0], even_w_in[j], fox_f_bias[j],
                            s5_params, s5_w_glu[j], s5_b_glu[j], even_w_out[j])
        else:
            h = _odd_mixer(h, batch, seq, alpha, ln_g[layer, 0], ln_b[layer, 0], odd_w_in[j], mlstm_conv_w[j],
                           mlstm_conv_b[j], mlstm_i_bias[j], mlstm_f_bias[j], odd_w_out[j])
        h = _hier_moe_ln(h, ln_g[layer, 1], ln_b[layer, 1], alpha, moe_w_group[layer], moe_b_group[layer],
                         moe_w_expert[layer], moe_b_expert[layer], moe_w_gate[layer], moe_w_up[layer],
                         moe_w_down[layer])
    return h.reshape(batch, seq, d)
```

```python
import functools

import jax
import jax.numpy as jnp
from jax import lax
from jax.experimental import pallas as pl
from jax.experimental.pallas import tpu as pltpu

F32 = jnp.float32
BF16 = jnp.bfloat16
HIGHEST = lax.Precision.HIGHEST
LN_EPS = 1e-5
NEG_INF = float("-inf")

LANES = 128
VMEM_LIMIT = 56 * 1024 * 1024

TM_PROJ = 256
ATT_BLOCK = 256
ATT_QUERY_BLOCK = 512
S5_CHUNK = 16
MLSTM_CHUNK = 256
MLSTM_HEADS_PER_STEP = 4
TM_MOE = 256
PAIRS_LO = (0, 0, 0, 1, 1, 2)
PAIRS_HI = (1, 2, 3, 2, 3, 3)

NT_DIMS = (((1,), (1,)), ((), ()))
TN_DIMS = (((0,), (0,)), ((), ()))


def _cparams(*sem):
    return pltpu.CompilerParams(dimension_semantics=sem, vmem_limit_bytes=VMEM_LIMIT)


def _log_sigmoid(x):
    return jnp.minimum(x, 0.0) - jnp.log1p(jnp.exp(-jnp.abs(x)))


def _layer_norm(x, g, b):
    mu = jnp.mean(x, axis=-1, keepdims=True)
    xc = x - mu
    var = jnp.mean(xc * xc, axis=-1, keepdims=True)
    return xc * lax.rsqrt(var + LN_EPS) * g + b


def _iota(shape, dim):
    return lax.broadcasted_iota(jnp.int32, shape, dim)


def _proj_even_kernel(x_ref, w_ref, qkv_ref, u_ref, f_ref, zs_ref, *, n_qkv, n_u, n_f):
    xb = x_ref[...].astype(BF16)
    qkv_ref[...] = jnp.dot(xb, w_ref[:, :n_qkv], preferred_element_type=F32).astype(BF16)
    f_ref[...] = jnp.dot(xb, w_ref[:, n_qkv + n_u:n_qkv + n_u + n_f], preferred_element_type=F32)
    z = jnp.dot(xb, w_ref[:, n_qkv:n_qkv + n_u], preferred_element_type=F32)
    L = S5_CHUNK
    nchunk = x_ref.shape[0] // L
    for q in range(n_u // LANES):
        zs_ref[q] = z[:, q * LANES:(q + 1) * LANES]
        for s in range(L):
            u_ref[q, :, s * LANES:(s + 1) * LANES] = zs_ref[q, pl.ds(s, nchunk, stride=L), :].astype(BF16)


def _proj_even(x, w, n_qkv, n_u, n_f):
    t, d = x.shape
    tm = TM_PROJ
    L = S5_CHUNK
    nq = n_u // LANES
    return pl.pallas_call(
        functools.partial(_proj_even_kernel, n_qkv=n_qkv, n_u=n_u, n_f=n_f),
        grid=(t // tm,),
        in_specs=[pl.BlockSpec((tm, d), lambda i: (i, 0)),
                  pl.BlockSpec(w.shape, lambda i: (0, 0))],
        out_specs=[pl.BlockSpec((tm, n_qkv), lambda i: (i, 0)),
                   pl.BlockSpec((nq, tm // L, L * LANES), lambda i: (0, i, 0)),
                   pl.BlockSpec((tm, n_f), lambda i: (i, 0))],
        out_shape=[jax.ShapeDtypeStruct((t, n_qkv), BF16),
                   jax.ShapeDtypeStruct((nq, t // L, L * LANES), BF16),
                   jax.ShapeDtypeStruct((t, n_f), F32)],
        scratch_shapes=[pltpu.VMEM((nq, tm, LANES), F32)],
        compiler_params=_cparams("parallel"),
        name="proj_even",
    )(x, w)


def _proj_odd_kernel(x_ref, w_ref, cw_ref, cb_ref, q_ref, k_ref, v_ref, o_ref, g_ref, *zs_refs,
                     tm, dmix, k_scale, tiles_per_seq, conv_width):
    i = pl.program_id(0)
    xb = x_ref[...].astype(BF16)
    cw = zs_refs[0].shape[1]

    @pl.when(i % tiles_per_seq == 0)
    def _():
        for zs_ref in zs_refs:
            zs_ref[0:8, :] = jnp.zeros((8, cw), F32)

    g_ref[...] = jnp.dot(xb, w_ref[:, 4 * dmix:4 * dmix + LANES], preferred_element_type=F32)
    for zs_ref, c0 in zip(zs_refs, range(0, 2 * dmix, cw)):
        zs_ref[8:tm + 8, :] = jnp.dot(xb, w_ref[:, c0:c0 + cw], preferred_element_type=F32)
        vo = jnp.dot(xb, w_ref[:, 2 * dmix + c0:2 * dmix + c0 + cw], preferred_element_type=F32)
        if c0 < dmix:
            v_ref[:, c0:c0 + cw] = vo.astype(BF16)
        else:
            o_ref[:, c0 - dmix:c0 - dmix + cw] = vo
        cols = slice(c0, c0 + cw)
        acc = jnp.broadcast_to(cb_ref[:, cols], (tm, cw))
        for j in range(conv_width):
            acc = acc + cw_ref[j:j + 1, cols] * zs_ref[pl.ds(8 - (conv_width - 1) + j, tm), :]
        y = acc * jax.nn.sigmoid(acc)
        if c0 < dmix:
            q_ref[:, cols] = y.astype(BF16)
        else:
            k_ref[:, c0 - dmix:c0 - dmix + cw] = (y * k_scale).astype(BF16)
        zs_ref[0:8, :] = zs_ref[tm:tm + 8, :]


def _proj_odd(x, w, conv_w, conv_b, seq, head_dim):
    t, d = x.shape
    dmix = conv_w.shape[1] // 2
    tm = TM_PROJ
    kern = functools.partial(_proj_odd_kernel, tm=tm, dmix=dmix, k_scale=head_dim ** -0.5,
                             tiles_per_seq=seq // tm, conv_width=conv_w.shape[0])
    row = lambda i: (i, 0)
    fixed = lambda i: (0, 0)
    return pl.pallas_call(
        kern,
        grid=(t // tm,),
        in_specs=[pl.BlockSpec((tm, d), row), pl.BlockSpec(w.shape, fixed),
                  pl.BlockSpec(conv_w.shape, fixed), pl.BlockSpec((1, 2 * dmix), fixed)],
        out_specs=[pl.BlockSpec((tm, dmix), row)] * 4 + [pl.BlockSpec((tm, LANES), row)],
        out_shape=[jax.ShapeDtypeStruct((t, dmix), BF16)] * 3
        + [jax.ShapeDtypeStruct((t, dmix), F32), jax.ShapeDtypeStruct((t, LANES), F32)],
        scratch_shapes=[pltpu.VMEM((tm + 8, 512), F32)] * (2 * dmix // 512),
        compiler_params=_cparams("arbitrary"),
        name="proj_odd",
    )(x, w, conv_w, conv_b.reshape(1, -1))


def _split3(x):
    hi = x.astype(BF16).astype(F32)
    r = x - hi
    mid = r.astype(BF16).astype(F32)
    lo = (r - mid).astype(BF16).astype(F32)
    return hi, mid, lo


def _fox_kernel(fb_ref, f_ref, q_ref, k_ref, v_ref, o_ref, c_ref, kaug_ref, vt_ref, acc_ref, *, bq, blk, seq, dh):
    qi = pl.program_id(2)
    lane = _iota((blk, LANES), 1)
    head_lanes = (lane < dh, lane >= dh)

    @pl.when(qi == 0)
    def _():
        tril = (_iota((blk, blk), 1) <= _iota((blk, blk), 0)).astype(BF16)
        eye_b = (_iota((LANES, LANES), 0) == _iota((LANES, LANES), 1)).astype(BF16)

        def prep(jb, carry):
            rows = pl.ds(pl.multiple_of(jb * blk, blk), blk)
            ls = _log_sigmoid(f_ref[rows, :] + fb_ref[0])
            c3 = jnp.dot(tril, jnp.concatenate(_split3(ls), axis=1).astype(BF16), preferred_element_type=F32)
            c = c3[:, :LANES] + c3[:, LANES:2 * LANES] + c3[:, 2 * LANES:] + carry
            c_ref[rows, :] = c
            kblk = k_ref[rows, :]
            for j in range(2):
                hi, mid, lo = _split3(c[:, j:j + 1])
                aug = jnp.where(lane < 3, 1.0, jnp.where(lane == 3, -hi, jnp.where(
                    lane == 4, -mid, jnp.where(lane == 5, -lo, 0.0))))
                kaug_ref[j, rows, :] = jnp.concatenate(
                    [jnp.where(head_lanes[j], kblk, jnp.zeros_like(kblk)), aug.astype(BF16)], axis=1)
            vt_ref[jb] = lax.dot_general(eye_b, v_ref[rows, :], NT_DIMS,
                                         preferred_element_type=F32).astype(BF16)
            return c[blk - 1:blk, :]

        lax.fori_loop(0, seq // blk, prep, jnp.zeros((1, LANES), F32))

    q = q_ref[...]
    qlane = _iota((bq, LANES), 1)
    c_q = c_ref[pl.ds(pl.multiple_of(qi * bq, bq), bq), :]
    q_aug = []
    for j in range(2):
        hi, mid, lo = _split3(c_q[:, j:j + 1])
        aug = jnp.where(qlane == 0, hi, jnp.where(qlane == 1, mid, jnp.where(
            qlane == 2, lo, jnp.where(qlane < 6, 1.0, 0.0))))
        q_head = jnp.where((qlane < dh) if j == 0 else (qlane >= dh), q, jnp.zeros_like(q))
        q_aug.append(jnp.concatenate([q_head, aug.astype(BF16)], axis=1))
    acc_ref[...] = jnp.zeros_like(acc_ref)
    r = bq // blk

    def score(kb):
        krows = pl.ds(pl.multiple_of(kb * blk, blk), blk)
        return tuple(lax.dot_general(kaug_ref[j, krows, :], q_aug[j], NT_DIMS, preferred_element_type=F32)
                     for j in range(2))

    def accumulate(kb, probs, alphas):
        vt = vt_ref[kb]
        for j in range(2):
            acc_ref[j] = alphas[j] * acc_ref[j] + jnp.dot(vt[j * dh:(j + 1) * dh, :], probs[j],
                                                          preferred_element_type=F32)

    def softmax(scores, stats, diag):
        out, probs, alphas = [], [], []
        for j in range(2):
            m, l = stats[2 * j], stats[2 * j + 1]
            s = scores[j]
            if diag is not None:
                s = jnp.where(_iota((blk, bq), 0) + diag * blk <= _iota((blk, bq), 1), s, NEG_INF)
            m_new = jnp.maximum(m, jnp.max(s, axis=0, keepdims=True))
            alpha = jnp.exp(m - m_new)
            p = jnp.exp(s - m_new)
            out += [m_new, alpha * l + jnp.sum(p, axis=0, keepdims=True)]
            probs.append(p.astype(BF16))
            alphas.append(alpha)
        return tuple(out), tuple(probs), tuple(alphas)

    m0 = jnp.full((1, bq), NEG_INF, F32)
    l0 = jnp.zeros((1, bq), F32)

    def step(kb, state):
        scores, stats = state
        nxt = score(kb + 1)
        stats, probs, alphas = softmax(scores, stats, None)
        accumulate(kb, probs, alphas)
        return nxt, stats

    first = r * qi
    scores, stats = lax.fori_loop(0, first, step, (score(0), (m0, l0, m0, l0)))
    for diag in range(r):
        nxt = score(first + diag + 1) if diag + 1 < r else None
        stats, probs, alphas = softmax(scores, stats, diag)
        accumulate(first + diag, probs, alphas)
        scores = nxt
    out_t = jnp.concatenate([acc_ref[0] / stats[1], acc_ref[1] / stats[3]], axis=0).astype(BF16)
    eye_q = (_iota((blk, blk), 0) == _iota((blk, blk), 1)).astype(BF16)
    for c0 in range(0, bq, blk):
        o_ref[c0:c0 + blk, :] = lax.dot_general(eye_q, out_t[:, c0:c0 + blk], NT_DIMS,
                                                preferred_element_type=F32).astype(o_ref.dtype)


def _fox_attention(qkv, f_cols, f_bias, batch, seq, heads, dh):
    t = qkv.shape[0]
    blk = ATT_BLOCK
    bq = min(ATT_QUERY_BLOCK, seq)
    nq = seq // bq
    hp = heads * dh // LANES
    kern = functools.partial(_fox_kernel, bq=bq, blk=blk, seq=seq, dh=dh)
    return pl.pallas_call(
        kern,
        grid=(batch, hp, nq),
        in_specs=[pl.BlockSpec((1, 1, LANES), lambda b, p, i: (p, 0, 0)),
                  pl.BlockSpec((seq, LANES), lambda b, p, i: (b, p)),
                  pl.BlockSpec((bq, LANES), lambda b, p, i: (b * nq + i, p)),
                  pl.BlockSpec((seq, LANES), lambda b, p, i: (b, hp + p)),
                  pl.BlockSpec((seq, LANES), lambda b, p, i: (b, 2 * hp + p))],
        out_specs=pl.BlockSpec((bq, LANES), lambda b, p, i: (b * nq + i, p)),
        out_shape=jax.ShapeDtypeStruct((t, heads * dh), BF16),
        scratch_shapes=[pltpu.VMEM((seq, LANES), F32), pltpu.VMEM((2, seq, 2 * LANES), BF16),
                        pltpu.VMEM((seq // blk, LANES, blk), BF16), pltpu.VMEM((2, dh, bq), F32)],
        compiler_params=_cparams("parallel", "parallel", "arbitrary"),
        name="fox_attention",
    )(f_bias, f_cols, qkv, qkv, qkv)


def _s5_tables(a_re, a_im, log_dt, b_re, b_im, c_re, c_im, d_skip):
    L = S5_CHUNK
    g, p = a_re.shape
    hc = b_re.shape[-1]
    dt = jnp.exp(log_dt)[:, None]
    mag = jnp.exp(a_re * dt)
    lb_re = mag * jnp.cos(a_im * dt)
    lb_im = mag * jnp.sin(a_im * dt)
    num_re = lb_re - 1.0
    num_im = lb_im
    den = a_re * a_re + a_im * a_im
    z_re = (num_re * a_re + num_im * a_im) / den
    z_im = (num_im * a_re - num_re * a_im) / den
    bb_re = z_re[..., None] * b_re - z_im[..., None] * b_im
    bb_im = z_re[..., None] * b_im + z_im[..., None] * b_re
    tau = jnp.arange(L + 1, dtype=F32)
    pmag = jnp.exp((a_re * dt)[..., None] * tau)
    pw_re = pmag * jnp.cos((a_im * dt)[..., None] * tau)
    pw_im = pmag * jnp.sin((a_im * dt)[..., None] * tau)
    cp_re = c_re[..., None] * pw_re[:, None] - c_im[..., None] * pw_im[:, None]
    cp_im = c_re[..., None] * pw_im[:, None] + c_im[..., None] * pw_re[:, None]
    kern = (jnp.einsum("gopt,gpi->gtoi", cp_re[..., :L], bb_re, precision=HIGHEST)
            - jnp.einsum("gopt,gpi->gtoi", cp_im[..., :L], bb_im, precision=HIGHEST))
    kern = kern.at[:, 0].add(d_skip[:, :, None] * jnp.eye(hc, dtype=F32))
    rev = (L - 1) - jnp.arange(L)
    e_re = pw_re[:, :, rev][..., None] * bb_re[:, :, None] - pw_im[:, :, rev][..., None] * bb_im[:, :, None]
    e_im = pw_re[:, :, rev][..., None] * bb_im[:, :, None] + pw_im[:, :, rev][..., None] * bb_re[:, :, None]
    o_re = cp_re[..., 1:]
    o_im = -cp_im[..., 1:]

    gq = LANES // hc
    nq = g // gq
    eye = jnp.eye(gq, dtype=BF16)

    def tile(m, perm):
        m = m.astype(BF16).reshape(nq, gq, *m.shape[1:])
        return jnp.moveaxis(m, 1, perm)

    k_t = tile(kern.transpose(0, 1, 3, 2), 2)
    lag_blocks = (k_t[:, :, :, :, None, :] * eye[None, None, :, None, :, None]).reshape(nq, L, LANES, LANES)
    w_end = jnp.concatenate(
        [(tile(e.transpose(0, 2, 3, 1), 2)[:, :, :, :, None, :]
          * eye[None, None, :, None, :, None]).reshape(nq, L * LANES, gq * p) for e in (e_re, e_im)], axis=2)
    w_out = jnp.concatenate(
        [(tile(o.transpose(0, 2, 3, 1), 1)[:, :, :, :, None, :]
          * eye[None, :, None, None, :, None]).reshape(nq, gq * p, L * LANES) for o in (o_re, o_im)], axis=1)
    lam_re = pw_re[..., L].reshape(1, g * p)
    lam_im = pw_im[..., L].reshape(1, g * p)
    return lag_blocks, w_end, w_out, lam_re, lam_im


def _s5_end_kernel(u_ref, w_ref, ere_ref, eim_ref):
    e = jnp.dot(u_ref[0], w_ref[0], preferred_element_type=F32)
    ns = ere_ref.shape[0]
    for k in range(ns):
        ere_ref[k] = e[:, k * LANES:(k + 1) * LANES]
        eim_ref[k] = e[:, (ns + k) * LANES:(ns + k + 1) * LANES]


def _s5_scan_kernel(lre_ref, lim_ref, ere_ref, eim_ref, hre_ref, him_ref, *, batch, nchunk):
    ns = ere_ref.shape[0]
    lr = [lre_ref[k] for k in range(ns)]
    li = [lim_ref[k] for k in range(ns)]

    def step(j, carry):
        sl = pl.ds(j, batch, stride=nchunk)
        out = []
        for k in range(ns):
            hr, hi = carry[2 * k], carry[2 * k + 1]
            hre_ref[k, sl, :] = hr
            him_ref[k, sl, :] = hi
            out += [lr[k] * hr - li[k] * hi + ere_ref[k, sl, :], lr[k] * hi + li[k] * hr + eim_ref[k, sl, :]]
        return tuple(out)

    z = jnp.zeros((batch, LANES), F32)
    lax.fori_loop(0, nchunk, step, (z,) * (2 * ns))


def _s5_out_kernel(u_ref, lag_ref, hre_ref, him_ref, w_ref, y_ref, t_ref):
    L = lag_ref.shape[1]

    @pl.when(pl.program_id(1) == 0)
    def _():
        for s in range(L):
            for t in range(s, L):
                t_ref[s * LANES:(s + 1) * LANES, t * LANES:(t + 1) * LANES] = lag_ref[0, t - s]
            if s % 2 == 1:
                t_ref[s * LANES:(s + 1) * LANES, (s - 1) * LANES:s * LANES] = jnp.zeros((LANES, LANES), BF16)

    half = w_ref.shape[1] // 2
    ns = hre_ref.shape[0]
    h_re = jnp.concatenate([hre_ref[k] for k in range(ns)], axis=1).astype(BF16)
    h_im = jnp.concatenate([him_ref[k] for k in range(ns)], axis=1).astype(BF16)
    inter = jnp.dot(h_re, w_ref[0, :half, :], preferred_element_type=F32)
    inter = inter + jnp.dot(h_im, w_ref[0, half:, :], preferred_element_type=F32)
    ct = 2 * LANES
    for c0 in range(0, t_ref.shape[1], ct):
        k_hi = c0 + ct
        y = jnp.dot(u_ref[0, :, :k_hi], t_ref[:k_hi, c0:c0 + ct], preferred_element_type=F32)
        y_ref[0, :, c0:c0 + ct] = jax.nn.gelu(y + inter[:, c0:c0 + ct])


def _s5(u4, tables, batch, seq):
    lag_blocks, w_end, w_out, lam_re, lam_im = tables
    L = S5_CHUNK
    nq, rows, kc = u4.shape
    nchunk = seq // L
    sw = w_end.shape[2] // 2
    ns = sw // LANES
    rb = rows // 2
    e_re, e_im = pl.pallas_call(
        _s5_end_kernel,
        grid=(nq,),
        in_specs=[pl.BlockSpec((1, rows, kc), lambda q: (q, 0, 0)),
                  pl.BlockSpec((1, kc, 2 * sw), lambda q: (q, 0, 0))],
        out_specs=[pl.BlockSpec((ns, rows, LANES), lambda q: (q, 0, 0))] * 2,
        out_shape=[jax.ShapeDtypeStruct((nq * ns, rows, LANES), F32)] * 2,
        compiler_params=_cparams("parallel"),
        name="s5_chunk_end",
    )(u4, w_end)
    lam_spec = pl.BlockSpec((ns, 1, LANES), lambda q: (q, 0, 0))
    st_spec = pl.BlockSpec((ns, rows, LANES), lambda q: (q, 0, 0))
    h_re, h_im = pl.pallas_call(
        functools.partial(_s5_scan_kernel, batch=batch, nchunk=nchunk),
        grid=(nq,),
        in_specs=[lam_spec, lam_spec, st_spec, st_spec],
        out_specs=[st_spec] * 2,
        out_shape=[jax.ShapeDtypeStruct(e_re.shape, F32)] * 2,
        compiler_params=_cparams("parallel"),
        name="s5_chunk_scan",
    )(lam_re.reshape(nq * ns, 1, LANES), lam_im.reshape(nq * ns, 1, LANES), e_re, e_im)
    hs_spec = pl.BlockSpec((ns, rb, LANES), lambda q, r: (q, r, 0))
    return pl.pallas_call(
        _s5_out_kernel,
        grid=(nq, rows // rb),
        in_specs=[pl.BlockSpec((1, rb, kc), lambda q, r: (q, r, 0)),
                  pl.BlockSpec((1, L, LANES, LANES), lambda q, r: (q, 0, 0, 0)),
                  hs_spec, hs_spec,
                  pl.BlockSpec((1, 2 * sw, kc), lambda q, r: (q, 0, 0))],
        out_specs=pl.BlockSpec((1, rb, kc), lambda q, r: (q, r, 0)),
        out_shape=jax.ShapeDtypeStruct((nq, rows, kc), F32),
        scratch_shapes=[pltpu.VMEM((kc, kc), BF16)],
        compiler_params=_cparams("parallel", "arbitrary"),
        name="s5_out",
    )(u4, lag_blocks, h_re, h_im, w_out)


def _rows_to_cols(eye3, sub, a_row, b_row):
    a3 = jnp.concatenate(_split3(a_row), axis=1)
    b3 = jnp.concatenate(_split3(b_row), axis=1)
    rows = jnp.where(sub == 0, a3, jnp.where(sub == 1, b3, 0.0)).astype(BF16)
    cols = lax.dot_general(eye3, rows, NT_DIMS, preferred_element_type=F32)
    return cols[:, 0:1], cols[:, 1:2]


def _mlstm_kernel(gb_ref, g_ref, q_ref, k_ref, v_ref, o_ref, h_ref, c_ref, b_scr, i_scr, *, L, seq, dh, nh):
    tri = (_iota((L, L), 0) <= _iota((L, L), 1)).astype(F32)
    eye = jnp.concatenate([(_iota((L, L), 0) == _iota((L, L), 1)).astype(BF16)] * 3, axis=1)
    causal = _iota((L, L), 1) <= _iota((L, L), 0)
    lane = _iota((L, LANES), 1)
    one_col = (lane == 0).astype(BF16)
    sub = _iota((8, 3 * L), 0)
    c_ref[...] = jnp.zeros_like(c_ref)
    for hh in range(nh):
        log_f = _log_sigmoid(g_ref[0, 0, hh, 1] + gb_ref[0, hh, 1])
        b_scr[hh] = jnp.dot(log_f, tri, precision=HIGHEST, preferred_element_type=F32)
        i_scr[hh] = g_ref[0, 0, hh, 0] + gb_ref[0, hh, 0]

    def chunk(c, carry):
        st = pl.multiple_of(c * L, L)
        heads = range(nh)
        cols_h = [slice(hh * dh, (hh + 1) * dh) for hh in heads]
        b_row = [b_scr[hh, pl.ds(c, 1), :] for hh in heads]
        li_row = [i_scr[hh, pl.ds(c, 1), :] for hh in heads]
        cols = [_rows_to_cols(eye, sub, b_row[hh], li_row[hh]) for hh in heads]
        q = [q_ref[pl.ds(st, L), cols_h[hh]] for hh in heads]
        k = [k_ref[pl.ds(st, L), cols_h[hh]] for hh in heads]
        v = [v_ref[pl.ds(st, L), cols_h[hh]] for hh in heads]
        qk = [lax.dot_general(q[hh], k[hh], NT_DIMS, preferred_element_type=F32) for hh in heads]
        qc = [jnp.dot(q[hh], c_ref[hh].astype(BF16), preferred_element_type=F32) for hh in heads]

        s, m_t, m_inter, m_new, w_col, decay = [], [], [], [], [], []
        for hh in heads:
            m_prev = carry[hh]
            b_col, li_col = cols[hh]
            b_last = b_row[hh][:, L - 1:L]
            log_d = jnp.where(causal, b_col - b_row[hh] + li_row[hh], NEG_INF)
            m_inter.append(b_col + m_prev)
            m_t.append(jnp.maximum(m_inter[hh], jnp.max(log_d, axis=1, keepdims=True)))
            s.append((qk[hh] * jnp.exp(log_d - m_t[hh])).astype(BF16))
            g_row = b_last - b_row[hh] + li_row[hh]
            m_new.append(jnp.maximum(b_last + m_prev, jnp.max(g_row, axis=1, keepdims=True)))
            w_col.append(jnp.exp(b_last - b_col + li_col - m_new[hh]))
            decay.append(jnp.exp(b_last + m_prev - m_new[hh]))

        tot = [jnp.dot(s[hh], jnp.concatenate([v[hh], one_col], axis=1), preferred_element_type=F32)
               for hh in heads]
        upd = [lax.dot_general(k[hh], jnp.concatenate(
            [(v[hh].astype(F32) * w_col[hh]).astype(BF16), jnp.where(lane == 0, w_col[hh], 0.0).astype(BF16)],
            axis=1), TN_DIMS, preferred_element_type=F32) for hh in heads]
        for hh in heads:
            t_h = tot[hh] + jnp.exp(m_inter[hh] - m_t[hh]) * qc[hh]
            den = jnp.maximum(jnp.abs(t_h[:, dh:dh + 1]), jnp.exp(-m_t[hh]))
            h = t_h[:, :dh] / den
            h_ref[pl.ds(st, L), cols_h[hh]] = (h * jax.nn.sigmoid(o_ref[pl.ds(st, L), cols_h[hh]])).astype(h_ref.dtype)
            c_ref[hh] = decay[hh] * c_ref[hh] + upd[hh]
        return tuple(m_new)

    lax.fori_loop(0, seq // L, chunk, (jnp.zeros((1, 1), F32),) * nh)


def _mlstm(q, k, v, o, gate_rows, gate_bias, batch, seq, heads, dh):
    t = q.shape[0]
    L = MLSTM_CHUNK
    nh = MLSTM_HEADS_PER_STEP
    nc = seq // L
    col = pl.BlockSpec((seq, nh * dh), lambda b, h: (b, h))
    return pl.pallas_call(
        functools.partial(_mlstm_kernel, L=L, seq=seq, dh=dh, nh=nh),
        grid=(batch, heads // nh),
        in_specs=[pl.BlockSpec((1, nh, 2, 1, 1), lambda b, h: (h, 0, 0, 0, 0)),
                  pl.BlockSpec((1, 1, nh, 2, nc, L), lambda b, h: (b, h, 0, 0, 0, 0)),
                  col, col, col, col],
        out_specs=col,
        out_shape=jax.ShapeDtypeStruct((t, heads * dh), BF16),
        scratch_shapes=[pltpu.VMEM((nh, dh, 2 * dh), F32), pltpu.VMEM((nh, nc, L), F32),
                        pltpu.VMEM((nh, nc, L), F32)],
        compiler_params=_cparams("parallel", "parallel"),
        name="mlstm",
    )(gate_bias, gate_rows, q, k, v, o)


def _out_even_kernel(h_ref, att_ref, ys4_ref, wg_ref, bg_ref, wo_ref, g_ref, b_ref, o_ref, ys_ref, *, alpha):
    L = S5_CHUNK
    nq = ys4_ref.shape[0]
    nchunk = ys_ref.shape[1] // L
    for q in range(nq):
        for t in range(L):
            ys_ref[q, pl.ds(t, nchunk, stride=L), :] = ys4_ref[q, :, t * LANES:(t + 1) * LANES]
    ys = jnp.concatenate([ys_ref[q] for q in range(nq)], axis=1)
    half = att_ref.shape[1]
    gate = jax.nn.sigmoid(jnp.dot(ys.astype(BF16), wg_ref[...], preferred_element_type=F32) + bg_ref[...])
    mix = jnp.dot(att_ref[...], wo_ref[:half, :], preferred_element_type=F32)
    mix = mix + jnp.dot((ys * gate).astype(BF16), wo_ref[half:, :], preferred_element_type=F32)
    o_ref[...] = _layer_norm(alpha * h_ref[...] + mix, g_ref[...], b_ref[...])


def _out_odd_kernel(h_ref, hm_ref, wo_ref, g_ref, b_ref, o_ref, *, alpha):
    mix = jnp.dot(hm_ref[...], wo_ref[...], preferred_element_type=F32)
    o_ref[...] = _layer_norm(alpha * h_ref[...] + mix, g_ref[...], b_ref[...])


def _row_tiled_call(kern, row_args, fixed_args, out_dtype, name):
    t = row_args[0].shape[0]
    tm = TM_PROJ
    in_specs = [pl.BlockSpec((tm, a.shape[1]), lambda i: (i, 0)) for a in row_args]
    in_specs += [pl.BlockSpec(a.shape, lambda i: (0, 0)) for a in fixed_args]
    d = row_args[0].shape[1]
    return pl.pallas_call(
        kern,
        grid=(t // tm,),
        in_specs=in_specs,
        out_specs=pl.BlockSpec((tm, d), lambda i: (i, 0)),
        out_shape=jax.ShapeDtypeStruct((t, d), out_dtype),
        compiler_params=_cparams("parallel"),
        name=name,
    )(*row_args, *fixed_args)


def _router_kernel(h_ref, w_ref, b_ref, o_ref, cnt_ref, run_ref, w2_ref, *, n_groups, epg):
    @pl.when(pl.program_id(0) == 0)
    def _():
        run_ref[...] = jnp.zeros_like(run_ref)
        w = w_ref[...]
        w_hi = w.astype(BF16)
        w2_ref[:, :LANES] = w_hi
        w2_ref[:, LANES:] = (w - w_hi.astype(F32)).astype(BF16)

    h = h_ref[...]
    h_hi = h.astype(BF16)
    h_lo = (h - h_hi.astype(F32)).astype(BF16)
    part = jnp.dot(h_hi, w2_ref[...], preferred_element_type=F32)
    logits = (part[:, :LANES] + (part[:, LANES:] + jnp.dot(h_lo, w2_ref[:, :LANES], preferred_element_type=F32))
              + b_ref[...])
    lane = _iota(logits.shape, 1)
    big = jnp.int32(LANES)
    lg = jnp.where(lane < n_groups, logits, NEG_INF)
    mg = jnp.max(lg, axis=1, keepdims=True)
    g_val = 1.0 / jnp.sum(jnp.exp(lg - mg), axis=1, keepdims=True)
    g_idx = jnp.min(jnp.where(lg == mg, lane, big), axis=1, keepdims=True)
    lo_lane = n_groups + g_idx * epg
    le = jnp.where((lane >= lo_lane) & (lane < lo_lane + epg), logits, NEG_INF)
    m1 = jnp.max(le, axis=1, keepdims=True)
    i1 = jnp.min(jnp.where(le == m1, lane, big), axis=1, keepdims=True)
    le2 = jnp.where(lane == i1, NEG_INF, le)
    m2 = jnp.max(le2, axis=1, keepdims=True)
    i2 = jnp.min(jnp.where(le2 == m2, lane, big), axis=1, keepdims=True)
    r = jnp.exp(m2 - m1)
    w1 = g_val / (1.0 + r)
    w2 = g_val * r / (1.0 + r)
    e1 = i1 - lo_lane
    e2 = i2 - lo_lane
    first_lo = e1 < e2
    lo = jnp.where(first_lo, e1, e2)
    hi = jnp.where(first_lo, e2, e1)
    w_lo = jnp.where(first_lo, w1, w2)
    w_hi = jnp.where(first_lo, w2, w1)
    pair = (lo * (2 * epg - 1 - lo)) // 2 + (hi - lo - 1)
    cls = g_idx * (epg * (epg - 1) // 2) + pair
    tm = h.shape[0]
    onehot = lane == cls
    earlier = (_iota((tm, tm), 1) < _iota((tm, tm), 0)).astype(BF16)
    before = jnp.dot(earlier, onehot.astype(BF16), preferred_element_type=F32) + run_ref[...]
    rank = jnp.sum(jnp.where(onehot, before, 0.0), axis=1, keepdims=True)
    run = run_ref[...] + jnp.sum(onehot.astype(F32), axis=0, keepdims=True)
    run_ref[...] = run
    cnt_ref[...] = run
    out = jnp.where(lane == 0, cls.astype(F32),
                    jnp.where(lane == 1, w_lo, jnp.where(lane == 2, w_hi, jnp.where(lane == 3, rank, 0.0))))
    o_ref[...] = out


def _router(h, w_r, b_r, n_groups, epg):
    t, d = h.shape
    tm = TM_PROJ
    return pl.pallas_call(
        functools.partial(_router_kernel, n_groups=n_groups, epg=epg),
        grid=(t // tm,),
        in_specs=[pl.BlockSpec((tm, d), lambda i: (i, 0)),
                  pl.BlockSpec(w_r.shape, lambda i: (0, 0)),
                  pl.BlockSpec(b_r.shape, lambda i: (0, 0))],
        out_specs=[pl.BlockSpec((tm, LANES), lambda i: (i, 0)), pl.BlockSpec((1, LANES), lambda i: (0, 0))],
        out_shape=[jax.ShapeDtypeStruct((t, LANES), F32), jax.ShapeDtypeStruct((1, LANES), F32)],
        scratch_shapes=[pltpu.VMEM((1, LANES), F32), pltpu.VMEM((d, 2 * LANES), BF16)],
        compiler_params=_cparams("arbitrary"),
        name="router",
    )(h, w_r, b_r)


def _rows_wait(buf, sem):
    pltpu.make_async_copy(buf, buf, sem).wait()


def _dispatch_kernel(pos_ref, cnt_ref, pend_ref, x_ref, r_ref, o_hbm, xbuf, sem, zsem, *, tm, nsteps, ztile):
    i = pl.program_id(0)
    slot = i % 2
    d = x_ref.shape[1]

    @pl.when(i == 0)
    def _():
        zeros = xbuf.at[1, pl.ds(0, ztile)]
        xbuf[1] = jnp.zeros(xbuf.shape[1:], F32)
        ncls = cnt_ref.shape[0]
        used = pend_ref[ncls - 1]
        firsts = [(cnt_ref[c] > 0, pend_ref[c] - ztile) for c in range(ncls)]
        firsts += [(used + k * ztile < o_hbm.shape[0], used + k * ztile) for k in range(ncls)]
        for cond, first in firsts:
            @pl.when(cond)
            def _():
                pltpu.make_async_copy(zeros, o_hbm.at[pl.ds(pl.multiple_of(first, ztile), ztile)], zsem).start()
        for cond, _ in firsts:
            @pl.when(cond)
            def _():
                pltpu.make_async_copy(zeros, o_hbm.at[pl.ds(0, ztile)], zsem).wait()

    @pl.when(i >= 2)
    def _():
        _rows_wait(xbuf.at[slot], sem.at[slot])

    xbuf[slot, :, :d] = x_ref[...]
    xbuf[slot, :, d:] = r_ref[...]

    def body(r, c):
        pltpu.make_async_copy(xbuf.at[slot, pl.ds(r, 1)], o_hbm.at[pl.ds(pos_ref[0, 0, r], 1)], sem.at[slot]).start()
        return c
    lax.fori_loop(0, tm, body, 0, unroll=8)

    @pl.when(i == nsteps - 1)
    def _():
        _rows_wait(xbuf.at[slot], sem.at[slot])
        if nsteps > 1:
            _rows_wait(xbuf.at[1 - slot], sem.at[1 - slot])


def _dispatch(h, route, pos3, counts, pend, p_rows):
    t, d = h.shape
    w = d + route.shape[1]
    tm = pos3.shape[2]
    assert TM_MOE <= tm
    nsteps = t // tm
    smem = pl.BlockSpec(memory_space=pltpu.SMEM)
    return pl.pallas_call(
        functools.partial(_dispatch_kernel, tm=tm, nsteps=nsteps, ztile=TM_MOE),
        grid=(nsteps,),
        in_specs=[pl.BlockSpec((1, 1, tm), lambda i: (i, 0, 0), memory_space=pltpu.SMEM), smem, smem,
                  pl.BlockSpec((tm, d), lambda i: (i, 0)), pl.BlockSpec((tm, route.shape[1]), lambda i: (i, 0))],
        out_specs=pl.BlockSpec(memory_space=pl.ANY),
        out_shape=jax.ShapeDtypeStruct((p_rows, w), F32),
        scratch_shapes=[pltpu.VMEM((2, tm, w), F32), pltpu.SemaphoreType.DMA((2,)), pltpu.SemaphoreType.DMA(())],
        compiler_params=_cparams("arbitrary"),
        name="moe_dispatch",
    )(pos3, counts, pend, h, route)


def _combine_kernel(pos_ref, nxt_ref, h_ref, g_ref, b_ref, y_hbm, o_ref, ybuf, sem, *, alpha, tm, nsteps):
    i = pl.program_id(0)
    slot = i % 2

    def gather(p_ref, s):
        def body(r, c):
            pltpu.make_async_copy(y_hbm.at[pl.ds(p_ref[0, 0, r], 1)], ybuf.at[s, pl.ds(r, 1)], sem.at[s]).start()
            return c
        lax.fori_loop(0, tm, body, 0, unroll=8)

    @pl.when(i == 0)
    def _():
        gather(pos_ref, 0)

    @pl.when(i + 1 < nsteps)
    def _():
        gather(nxt_ref, 1 - slot)

    _rows_wait(ybuf.at[slot], sem.at[slot])
    o_ref[...] = _layer_norm(alpha * h_ref[...] + ybuf[slot], g_ref[...], b_ref[...])


def _combine(h, pos3, y_sorted, ln_g, ln_b, alpha):
    t, d = h.shape
    tm = pos3.shape[2]
    nsteps = t // tm
    row = lambda i: (i, 0)
    fixed = lambda i: (0, 0)
    return pl.pallas_call(
        functools.partial(_combine_kernel, alpha=alpha, tm=tm, nsteps=nsteps),
        grid=(nsteps,),
        in_specs=[pl.BlockSpec((1, 1, tm), lambda i: (i, 0, 0), memory_space=pltpu.SMEM),
                  pl.BlockSpec((1, 1, tm), lambda i: (jnp.minimum(i + 1, nsteps - 1), 0, 0), memory_space=pltpu.SMEM),
                  pl.BlockSpec((tm, d), row),
                  pl.BlockSpec((1, d), fixed), pl.BlockSpec((1, d), fixed),
                  pl.BlockSpec(memory_space=pl.ANY)],
        out_specs=pl.BlockSpec((tm, d), row),
        out_shape=jax.ShapeDtypeStruct((t, d), F32),
        scratch_shapes=[pltpu.VMEM((2, tm, d), F32), pltpu.SemaphoreType.DMA((2,))],
        compiler_params=_cparams("arbitrary"),
        name="moe_combine_ln",
    )(pos3, pos3, h, ln_g.reshape(1, -1), ln_b.reshape(1, -1), y_sorted)


def _moe_kernel(elo_ref, ehi_ref, nlive_ref, x_ref, g0_ref, u0_ref, d0_ref, g1_ref, u1_ref, d1_ref, y_ref,
                wg_ref, wu_ref, wd_ref):
    i = pl.program_id(0)
    d = y_ref.shape[1]
    live = i < nlive_ref[0]
    prev = jnp.maximum(i - 1, 0)
    experts = ((elo_ref, g0_ref, u0_ref, d0_ref), (ehi_ref, g1_ref, u1_ref, d1_ref))

    for j, (e_ref, g_ref, u_ref, d_ref) in enumerate(experts):
        @pl.when(live & ((i == 0) | (e_ref[i] != e_ref[prev])))
        def _():
            wg_ref[j] = g_ref[0].astype(BF16)
            wu_ref[j] = u_ref[0].astype(BF16)
            wd_ref[j] = d_ref[0].astype(BF16)

    @pl.when(jnp.logical_not(live))
    def _():
        y_ref[...] = jnp.zeros_like(y_ref)

    @pl.when(live)
    def _():
        x = x_ref[:, :d].astype(BF16)
        route = x_ref[:, d:]
        y = None
        for j in range(2):
            gate = jnp.dot(x, wg_ref[j], preferred_element_type=F32)
            up = jnp.dot(x, wu_ref[j], preferred_element_type=F32)
            hid = gate * jax.nn.sigmoid(gate) * up * route[:, j + 1:j + 2]
            part = jnp.dot(hid.astype(BF16), wd_ref[j], preferred_element_type=F32)
            y = part if y is None else y + part
        y_ref[...] = y


def _moe_experts(x_sorted, e_lo, e_hi, n_live, w_gate, w_up, w_down):
    p, xw = x_sorted.shape
    d = w_gate.shape[1]
    tm = TM_MOE
    f = w_gate.shape[2]
    lo_in = pl.BlockSpec((1, d, f), lambda i, lo, hi, nl: (lo[i], 0, 0))
    hi_in = pl.BlockSpec((1, d, f), lambda i, lo, hi, nl: (hi[i], 0, 0))
    lo_dn = pl.BlockSpec((1, f, d), lambda i, lo, hi, nl: (lo[i], 0, 0))
    hi_dn = pl.BlockSpec((1, f, d), lambda i, lo, hi, nl: (hi[i], 0, 0))
    grid_spec = pltpu.PrefetchScalarGridSpec(
        num_scalar_prefetch=3,
        grid=(p // tm,),
        in_specs=[pl.BlockSpec((tm, xw), lambda i, lo, hi, nl: (jnp.minimum(i, nl[0] - 1), 0)),
                  lo_in, lo_in, lo_dn, hi_in, hi_in, hi_dn],
        out_specs=pl.BlockSpec((tm, d), lambda i, lo, hi, nl: (i, 0)),
        scratch_shapes=[pltpu.VMEM((2, d, f), BF16), pltpu.VMEM((2, d, f), BF16), pltpu.VMEM((2, f, d), BF16)],
    )
    return pl.pallas_call(
        _moe_kernel,
        grid_spec=grid_spec,
        out_shape=jax.ShapeDtypeStruct((p, d), F32),
        compiler_params=_cparams("arbitrary"),
        name="moe_experts",
    )(e_lo, e_hi, n_live, x_sorted, w_gate, w_up, w_down, w_gate, w_up, w_down)


def _hier_moe_ln(h, ln_g, ln_b, alpha, w_group, b_group, w_expert, b_expert, w_gate, w_up, w_down):
    t, d = h.shape
    n_groups, _, epg = w_expert.shape
    npairs = epg * (epg - 1) // 2
    ncls = n_groups * npairs
    tm = TM_MOE
    w_r = jnp.concatenate([w_group, w_expert.transpose(1, 0, 2).reshape(d, n_groups * epg)], axis=1)
    b_r = jnp.concatenate([b_group, b_expert.reshape(-1)])
    pad = LANES - w_r.shape[1]
    w_r = jnp.pad(w_r, ((0, 0), (0, pad)))
    b_r = jnp.pad(b_r, (0, pad)).reshape(1, LANES)
    route, cnt = _router(h, w_r, b_r, n_groups, epg)

    cls = route[:, 0].astype(jnp.int32)
    rank = route[:, 3].astype(jnp.int32)
    counts = cnt[0, :ncls].astype(jnp.int32)
    padded = ((counts + tm - 1) // tm) * tm
    pend = jnp.cumsum(padded)
    pstart = pend - padded
    pos = jnp.sum(jnp.where(cls[:, None] == jnp.arange(ncls)[None, :], pstart[None, :], 0), axis=1) + rank
    pos3 = pos.astype(jnp.int32).reshape(t // TM_PROJ, 1, TM_PROJ)
    p_rows = t + ncls * tm
    x_sorted = _dispatch(h, route, pos3, counts, pend.astype(jnp.int32), p_rows)

    tile_start = jnp.arange(p_rows // tm, dtype=jnp.int32) * tm
    tile_start = jnp.minimum(tile_start, pend[-1] - tm)
    tile_cls = jnp.sum(pend[None, :] <= tile_start[:, None], axis=1).astype(jnp.int32)
    n_live = (pend[-1:] // tm).astype(jnp.int32)
    pair = tile_cls % npairs
    pair_lo = sum(jnp.where(pair == k, v, 0) for k, v in enumerate(PAIRS_LO))
    pair_hi = sum(jnp.where(pair == k, v, 0) for k, v in enumerate(PAIRS_HI))
    e_lo = ((tile_cls // npairs) * epg + pair_lo).astype(jnp.int32)
    e_hi = ((tile_cls // npairs) * epg + pair_hi).astype(jnp.int32)

    y_sorted = _moe_experts(x_sorted, e_lo, e_hi, n_live, w_gate, w_up, w_down)
    return _combine(h, pos3, y_sorted, ln_g, ln_b, alpha)


def _even_mixer(h, batch, seq, alpha, ln_g, ln_b, w_in, f_bias, s5_params, w_glu, b_glu, w_out):
    t, d = h.shape
    heads = f_bias.shape[0]
    groups, p_state = s5_params[0].shape
    hc = s5_params[3].shape[-1]
    s5_width = groups * hc
    fox_width = d - s5_width
    dh = fox_width // heads
    q_scale = dh ** -0.5
    w_q, w_k, w_v, w_f, w_u = jnp.split(w_in, [fox_width, 2 * fox_width, 3 * fox_width, 3 * fox_width + heads], axis=1)
    hp = heads // 2
    w_f = jnp.pad(w_f.reshape(d, hp, 2), ((0, 0), (0, 0), (0, LANES - 2))).reshape(d, hp * LANES)
    fb = jnp.pad(f_bias.reshape(hp, 1, 2), ((0, 0), (0, 0), (0, LANES - 2)))
    w_cat = jnp.concatenate([w_q * q_scale, w_k, w_v, w_u, w_f], axis=1)
    qkv, u4, f_cols = _proj_even(h, w_cat.astype(BF16), 3 * fox_width, s5_width, hp * LANES)
    att = _fox_attention(qkv, f_cols, fb, batch, seq, heads, dh)
    ys4 = _s5(u4, _s5_tables(*s5_params), batch, seq)
    tm = TM_PROJ
    row = lambda i: (i, 0)
    fixed = lambda i: (0, 0)
    fixed_args = (w_glu.astype(BF16), b_glu.reshape(1, -1), w_out.astype(BF16), ln_g.reshape(1, -1), ln_b.reshape(1, -1))
    return pl.pallas_call(
        functools.partial(_out_even_kernel, alpha=alpha),
        grid=(t // tm,),
        in_specs=[pl.BlockSpec((tm, d), row), pl.BlockSpec((tm, fox_width), row),
                  pl.BlockSpec((ys4.shape[0], tm // S5_CHUNK, ys4.shape[2]), lambda i: (0, i, 0))]
        + [pl.BlockSpec(a.shape, fixed) for a in fixed_args],
        out_specs=pl.BlockSpec((tm, d), row),
        out_shape=jax.ShapeDtypeStruct((t, d), F32),
        scratch_shapes=[pltpu.VMEM((ys4.shape[0], tm, LANES), F32)],
        compiler_params=_cparams("parallel"),
        name="out_even",
    )(h, att, ys4, *fixed_args)


def _odd_mixer(h, batch, seq, alpha, ln_g, ln_b, w_in, conv_w, conv_b, i_bias, f_bias, w_out):
    t, d = h.shape
    heads = i_bias.shape[0]
    dmix = conv_w.shape[1] // 2
    dh = dmix // heads
    w_main, w_gates = w_in[:, :4 * dmix], w_in[:, 4 * dmix:]
    w_cat = jnp.concatenate([w_main, jnp.pad(w_gates, ((0, 0), (0, LANES - 2 * heads)))], axis=1)
    q, k, v, o, gates = _proj_odd(h, w_cat.astype(BF16), conv_w, conv_b, seq, dh)
    nh = MLSTM_HEADS_PER_STEP
    g = gates[:, :2 * heads].reshape(batch, seq, 2, heads).transpose(0, 3, 2, 1)
    gate_rows = g.reshape(batch, heads // nh, nh, 2, seq // MLSTM_CHUNK, MLSTM_CHUNK)
    gate_bias = jnp.stack([i_bias, f_bias], axis=1).reshape(heads // nh, nh, 2, 1, 1)
    hm = _mlstm(q, k, v, o, gate_rows, gate_bias, batch, seq, heads, dh)
    kern = functools.partial(_out_odd_kernel, alpha=alpha)
    return _row_tiled_call(kern, (h, hm), (w_out.astype(BF16), ln_g.reshape(1, -1), ln_b.reshape(1, -1)),
                           F32, "out_odd")


def kernel(x, ln_g, ln_b, even_w_in, fox_f_bias, s5_a_re, s5_a_im, s5_log_dt, s5_b_re, s5_b_im, s5_c_re, s5_c_im, s5_d, s5_w_glu, s5_b_glu, even_w_out, odd_w_in, mlstm_conv_w, mlstm_conv_b, mlstm_i_bias, mlstm_f_bias, odd_w_out, moe_w_group, moe_b_group, moe_w_expert, moe_b_expert, moe_w_gate, moe_w_up, moe_w_down):
    batch, seq, d = x.shape
    depth = ln_g.shape[0]
    alpha = (2 * depth) ** 0.25
    h = x.reshape(batch * seq, d)
    for layer in range(depth):
        j = layer // 2
        if layer % 2 == 0:
            s5_params = (s5_a_re[j], s5_a_im[j], s5_log_dt[j], s5_b_re[j], s5_b_im[j],
                         s5_c_re[j], s5_c_im[j], s5_d[j])
            h = _even_mixer(h, batch, seq, alpha, ln_g[layer, 0], ln_b[layer, 0], even_w_in[j], fox_f_bias[j],
                            s5_params, s5_w_glu[j], s5_b_glu[j], even_w_out[j])
        else:
            h = _odd_mixer(h, batch, seq, alpha, ln_g[layer, 0], ln_b[layer, 0], odd_w_in[j], mlstm_conv_w[j],
                           mlstm_conv_b[j], mlstm_i_bias[j], mlstm_f_bias[j], odd_w_out[j])
        h = _hier_moe_ln(h, ln_g[layer, 1], ln_b[layer, 1], alpha, moe_w_group[layer], moe_b_group[layer],
                         moe_w_expert[layer], moe_b_expert[layer], moe_w_gate[layer], moe_w_up[layer],
                         moe_w_down[layer])
    return h.reshape(batch, seq, d)
```

```python
import functools

import jax
import jax.numpy as jnp
from jax import lax
from jax.experimental import pallas as pl
from jax.experimental.pallas import tpu as pltpu

F32 = jnp.float32
BF16 = jnp.bfloat16
HIGHEST = lax.Precision.HIGHEST
LN_EPS = 1e-5
NEG_INF = float("-inf")

LANES = 128
VMEM_LIMIT = 56 * 1024 * 1024

TM_PROJ = 256
ATT_BLOCK = 256
ATT_QUERY_BLOCK = 512
S5_CHUNK = 16
MLSTM_CHUNK = 256
MLSTM_HEADS_PER_STEP = 4
TM_MOE = 256
PAIRS_LO = (0, 0, 0, 1, 1, 2)
PAIRS_HI = (1, 2, 3, 2, 3, 3)

NT_DIMS = (((1,), (1,)), ((), ()))
TN_DIMS = (((0,), (0,)), ((), ()))


def _cparams(*sem):
    return pltpu.CompilerParams(dimension_semantics=sem, vmem_limit_bytes=VMEM_LIMIT)


def _log_sigmoid(x):
    return jnp.minimum(x, 0.0) - jnp.log1p(jnp.exp(-jnp.abs(x)))


def _layer_norm(x, g, b):
    mu = jnp.mean(x, axis=-1, keepdims=True)
    xc = x - mu
    var = jnp.mean(xc * xc, axis=-1, keepdims=True)
    return xc * lax.rsqrt(var + LN_EPS) * g + b


def _iota(shape, dim):
    return lax.broadcasted_iota(jnp.int32, shape, dim)


def _proj_even_kernel(x_ref, w_ref, qkv_ref, u_ref, f_ref, zs_ref, *, n_qkv, n_u, n_f):
    xb = x_ref[...].astype(BF16)
    qkv_ref[...] = jnp.dot(xb, w_ref[:, :n_qkv], preferred_element_type=F32).astype(BF16)
    f_ref[...] = jnp.dot(xb, w_ref[:, n_qkv + n_u:n_qkv + n_u + n_f], preferred_element_type=F32)
    z = jnp.dot(xb, w_ref[:, n_qkv:n_qkv + n_u], preferred_element_type=F32)
    L = S5_CHUNK
    nchunk = x_ref.shape[0] // L
    for q in range(n_u // LANES):
        zs_ref[q] = z[:, q * LANES:(q + 1) * LANES]
        for s in range(L):
            u_ref[q, :, s * LANES:(s + 1) * LANES] = zs_ref[q, pl.ds(s, nchunk, stride=L), :].astype(BF16)


def _proj_even(x, w, n_qkv, n_u, n_f):
    t, d = x.shape
    tm = TM_PROJ
    L = S5_CHUNK
    nq = n_u // LANES
    return pl.pallas_call(
        functools.partial(_proj_even_kernel, n_qkv=n_qkv, n_u=n_u, n_f=n_f),
        grid=(t // tm,),
        in_specs=[pl.BlockSpec((tm, d), lambda i: (i, 0)),
                  pl.BlockSpec(w.shape, lambda i: (0, 0))],
        out_specs=[pl.BlockSpec((tm, n_qkv), lambda i: (i, 0)),
                   pl.BlockSpec((nq, tm // L, L * LANES), lambda i: (0, i, 0)),
                   pl.BlockSpec((tm, n_f), lambda i: (i, 0))],
        out_shape=[jax.ShapeDtypeStruct((t, n_qkv), BF16),
                   jax.ShapeDtypeStruct((nq, t // L, L * LANES), BF16),
                   jax.ShapeDtypeStruct((t, n_f), F32)],
        scratch_shapes=[pltpu.VMEM((nq, tm, LANES), F32)],
        compiler_params=_cparams("parallel"),
        name="proj_even",
    )(x, w)


def _proj_odd_kernel(x_ref, w_ref, wg_ref, cw_ref, cb_ref, q_ref, k_ref, v_ref, o_ref, g_ref, *zs_refs,
                     tm, dmix, k_scale, tiles_per_seq, conv_width):
    i = pl.program_id(0)
    xb = x_ref[...].astype(BF16)
    cw = zs_refs[0].shape[1]

    @pl.when(i % tiles_per_seq == 0)
    def _():
        for zs_ref in zs_refs:
            zs_ref[0:8, :] = jnp.zeros((8, cw), F32)

    g_ref[...] = jnp.dot(xb, wg_ref[...], preferred_element_type=F32)
    for zs_ref, c0 in zip(zs_refs, range(0, 2 * dmix, cw)):
        zs_ref[8:tm + 8, :] = jnp.dot(xb, w_ref[:, c0:c0 + cw], preferred_element_type=F32)
        vo = jnp.dot(xb, w_ref[:, 2 * dmix + c0:2 * dmix + c0 + cw], preferred_element_type=F32)
        if c0 < dmix:
            v_ref[:, c0:c0 + cw] = vo.astype(BF16)
        else:
            o_ref[:, c0 - dmix:c0 - dmix + cw] = vo
        cols = slice(c0, c0 + cw)
        acc = jnp.broadcast_to(cb_ref[:, cols], (tm, cw))
        for j in range(conv_width):
            acc = acc + cw_ref[j:j + 1, cols] * zs_ref[pl.ds(8 - (conv_width - 1) + j, tm), :]
        y = acc * jax.nn.sigmoid(acc)
        if c0 < dmix:
            q_ref[:, cols] = y.astype(BF16)
        else:
            k_ref[:, c0 - dmix:c0 - dmix + cw] = (y * k_scale).astype(BF16)
        zs_ref[0:8, :] = zs_ref[tm:tm + 8, :]


def _proj_odd(x, w, w_gates, conv_w, conv_b, seq, head_dim):
    t, d = x.shape
    dmix = conv_w.shape[1] // 2
    tm = TM_PROJ
    kern = functools.partial(_proj_odd_kernel, tm=tm, dmix=dmix, k_scale=head_dim ** -0.5,
                             tiles_per_seq=seq // tm, conv_width=conv_w.shape[0])
    row = lambda i: (i, 0)
    fixed = lambda i: (0, 0)
    return pl.pallas_call(
        kern,
        grid=(t // tm,),
        in_specs=[pl.BlockSpec((tm, d), row), pl.BlockSpec(w.shape, fixed), pl.BlockSpec(w_gates.shape, fixed),
                  pl.BlockSpec(conv_w.shape, fixed), pl.BlockSpec((1, 2 * dmix), fixed)],
        out_specs=[pl.BlockSpec((tm, dmix), row)] * 4 + [pl.BlockSpec((tm, LANES), row)],
        out_shape=[jax.ShapeDtypeStruct((t, dmix), BF16)] * 3
        + [jax.ShapeDtypeStruct((t, dmix), F32), jax.ShapeDtypeStruct((t, LANES), F32)],
        scratch_shapes=[pltpu.VMEM((tm + 8, 512), F32)] * (2 * dmix // 512),
        compiler_params=_cparams("arbitrary"),
        name="proj_odd",
    )(x, w, w_gates, conv_w, conv_b.reshape(1, -1))


def _split3(x):
    hi = x.astype(BF16).astype(F32)
    r = x - hi
    mid = r.astype(BF16).astype(F32)
    lo = (r - mid).astype(BF16).astype(F32)
    return hi, mid, lo


def _fox_kernel(fb_ref, f_ref, q_ref, k_ref, v_ref, o_ref, c_ref, kaug_ref, vt_ref, acc_ref, *, bq, blk, seq, dh):
    qi = pl.program_id(2)
    lane = _iota((blk, LANES), 1)
    head_lanes = (lane < dh, lane >= dh)

    @pl.when(qi == 0)
    def _():
        tril = (_iota((blk, blk), 1) <= _iota((blk, blk), 0)).astype(BF16)
        eye_b = (_iota((LANES, LANES), 0) == _iota((LANES, LANES), 1)).astype(BF16)

        def prep(jb, carry):
            rows = pl.ds(pl.multiple_of(jb * blk, blk), blk)
            ls = _log_sigmoid(f_ref[rows, :] + fb_ref[0])
            c3 = jnp.dot(tril, jnp.concatenate(_split3(ls), axis=1).astype(BF16), preferred_element_type=F32)
            c = c3[:, :LANES] + c3[:, LANES:2 * LANES] + c3[:, 2 * LANES:] + carry
            c_ref[rows, :] = c
            kblk = k_ref[rows, :]
            for j in range(2):
                hi, mid, lo = _split3(c[:, j:j + 1])
                aug = jnp.where(lane < 3, 1.0, jnp.where(lane == 3, -hi, jnp.where(
                    lane == 4, -mid, jnp.where(lane == 5, -lo, 0.0))))
                kaug_ref[j, rows, :] = jnp.concatenate(
                    [jnp.where(head_lanes[j], kblk, jnp.zeros_like(kblk)), aug.astype(BF16)], axis=1)
            vt_ref[jb] = lax.dot_general(eye_b, v_ref[rows, :], NT_DIMS,
                                         preferred_element_type=F32).astype(BF16)
            return c[blk - 1:blk, :]

        lax.fori_loop(0, seq // blk, prep, jnp.zeros((1, LANES), F32))

    q = q_ref[...]
    qlane = _iota((bq, LANES), 1)
    c_q = c_ref[pl.ds(pl.multiple_of(qi * bq, bq), bq), :]
    q_aug = []
    for j in range(2):
        hi, mid, lo = _split3(c_q[:, j:j + 1])
        aug = jnp.where(qlane == 0, hi, jnp.where(qlane == 1, mid, jnp.where(
            qlane == 2, lo, jnp.where(qlane < 6, 1.0, 0.0))))
        q_head = jnp.where((qlane < dh) if j == 0 else (qlane >= dh), q, jnp.zeros_like(q))
        q_aug.append(jnp.concatenate([q_head, aug.astype(BF16)], axis=1))
    acc_ref[...] = jnp.zeros_like(acc_ref)
    r = bq // blk

    def score(kb):
        krows = pl.ds(pl.multiple_of(kb * blk, blk), blk)
        return tuple(lax.dot_general(kaug_ref[j, krows, :], q_aug[j], NT_DIMS, preferred_element_type=F32)
                     for j in range(2))

    def accumulate(kb, probs, alphas):
        vt = vt_ref[kb]
        for j in range(2):
            acc_ref[j] = alphas[j] * acc_ref[j] + jnp.dot(vt[j * dh:(j + 1) * dh, :], probs[j],
                                                          preferred_element_type=F32)

    def softmax(scores, stats, diag):
        out, probs, alphas = [], [], []
        for j in range(2):
            m, l = stats[2 * j], stats[2 * j + 1]
            s = scores[j]
            if diag is not None:
                s = jnp.where(_iota((blk, bq), 0) + diag * blk <= _iota((blk, bq), 1), s, NEG_INF)
            m_new = jnp.maximum(m, jnp.max(s, axis=0, keepdims=True))
            alpha = jnp.exp(m - m_new)
            p = jnp.exp(s - m_new)
            out += [m_new, alpha * l + jnp.sum(p, axis=0, keepdims=True)]
            probs.append(p.astype(BF16))
            alphas.append(alpha)
        return tuple(out), tuple(probs), tuple(alphas)

    m0 = jnp.full((1, bq), NEG_INF, F32)
    l0 = jnp.zeros((1, bq), F32)

    def step(kb, state):
        scores, stats = state
        nxt = score(kb + 1)
        stats, probs, alphas = softmax(scores, stats, None)
        accumulate(kb, probs, alphas)
        return nxt, stats

    first = r * qi
    scores, stats = lax.fori_loop(0, first, step, (score(0), (m0, l0, m0, l0)))
    for diag in range(r):
        nxt = score(first + diag + 1) if diag + 1 < r else None
        stats, probs, alphas = softmax(scores, stats, diag)
        accumulate(first + diag, probs, alphas)
        scores = nxt
    out_t = jnp.concatenate([acc_ref[0] / stats[1], acc_ref[1] / stats[3]], axis=0).astype(BF16)
    eye_q = (_iota((blk, blk), 0) == _iota((blk, blk), 1)).astype(BF16)
    for c0 in range(0, bq, blk):
        o_ref[c0:c0 + blk, :] = lax.dot_general(eye_q, out_t[:, c0:c0 + blk], NT_DIMS,
                                                preferred_element_type=F32).astype(o_ref.dtype)


def _fox_attention(qkv, f_cols, f_bias, batch, seq, heads, dh):
    t = qkv.shape[0]
    blk = ATT_BLOCK
    bq = min(ATT_QUERY_BLOCK, seq)
    nq = seq // bq
    hp = heads * dh // LANES
    kern = functools.partial(_fox_kernel, bq=bq, blk=blk, seq=seq, dh=dh)
    return pl.pallas_call(
        kern,
        grid=(batch, hp, nq),
        in_specs=[pl.BlockSpec((1, 1, LANES), lambda b, p, i: (p, 0, 0)),
                  pl.BlockSpec((seq, LANES), lambda b, p, i: (b, p)),
                  pl.BlockSpec((bq, LANES), lambda b, p, i: (b * nq + i, p)),
                  pl.BlockSpec((seq, LANES), lambda b, p, i: (b, hp + p)),
                  pl.BlockSpec((seq, LANES), lambda b, p, i: (b, 2 * hp + p))],
        out_specs=pl.BlockSpec((bq, LANES), lambda b, p, i: (b * nq + i, p)),
        out_shape=jax.ShapeDtypeStruct((t, heads * dh), BF16),
        scratch_shapes=[pltpu.VMEM((seq, LANES), F32), pltpu.VMEM((2, seq, 2 * LANES), BF16),
                        pltpu.VMEM((seq // blk, LANES, blk), BF16), pltpu.VMEM((2, dh, bq), F32)],
        compiler_params=_cparams("parallel", "parallel", "arbitrary"),
        name="fox_attention",
    )(f_bias, f_cols, qkv, qkv, qkv)


def _s5_tables(a_re, a_im, log_dt, b_re, b_im, c_re, c_im, d_skip):
    L = S5_CHUNK
    g, p = a_re.shape
    hc = b_re.shape[-1]
    dt = jnp.exp(log_dt)[:, None]
    mag = jnp.exp(a_re * dt)
    lb_re = mag * jnp.cos(a_im * dt)
    lb_im = mag * jnp.sin(a_im * dt)
    num_re = lb_re - 1.0
    num_im = lb_im
    den = a_re * a_re + a_im * a_im
    z_re = (num_re * a_re + num_im * a_im) / den
    z_im = (num_im * a_re - num_re * a_im) / den
    bb_re = z_re[..., None] * b_re - z_im[..., None] * b_im
    bb_im = z_re[..., None] * b_im + z_im[..., None] * b_re
    tau = jnp.arange(L + 1, dtype=F32)
    pmag = jnp.exp((a_re * dt)[..., None] * tau)
    pw_re = pmag * jnp.cos((a_im * dt)[..., None] * tau)
    pw_im = pmag * jnp.sin((a_im * dt)[..., None] * tau)
    cp_re = c_re[..., None] * pw_re[:, None] - c_im[..., None] * pw_im[:, None]
    cp_im = c_re[..., None] * pw_im[:, None] + c_im[..., None] * pw_re[:, None]
    kern = (jnp.einsum("gopt,gpi->gtoi", cp_re[..., :L], bb_re, precision=HIGHEST)
            - jnp.einsum("gopt,gpi->gtoi", cp_im[..., :L], bb_im, precision=HIGHEST))
    kern = kern.at[:, 0].add(d_skip[:, :, None] * jnp.eye(hc, dtype=F32))
    rev = (L - 1) - jnp.arange(L)
    e_re = pw_re[:, :, rev][..., None] * bb_re[:, :, None] - pw_im[:, :, rev][..., None] * bb_im[:, :, None]
    e_im = pw_re[:, :, rev][..., None] * bb_im[:, :, None] + pw_im[:, :, rev][..., None] * bb_re[:, :, None]
    o_re = cp_re[..., 1:]
    o_im = -cp_im[..., 1:]

    gq = LANES // hc
    nq = g // gq

    def tile(m, perm):
        m = m.astype(BF16).reshape(nq, gq, *m.shape[1:])
        return jnp.moveaxis(m, 1, perm)

    def embed(m2, spread, row_group, col_group):
        full = jnp.dot(m2, spread.astype(BF16), preferred_element_type=F32)
        keep = row_group(jnp.arange(full.shape[0]))[:, None] == col_group(jnp.arange(full.shape[1]))[None, :]
        return jnp.where(keep, full, 0.0).astype(BF16)

    eye_c = jnp.tile(jnp.eye(hc, dtype=F32), (1, gq))
    eye_p = jnp.tile(jnp.eye(p, dtype=F32), (1, gq))
    chan_group = lambda r: (r // hc) % gq
    k_t = tile(kern.transpose(0, 1, 3, 2), 2).reshape(nq * L * LANES, hc)
    lag_blocks = embed(k_t, eye_c, chan_group, chan_group).reshape(nq, L, LANES, LANES)
    w_end = jnp.concatenate(
        [embed(tile(e.transpose(0, 2, 3, 1), 2).reshape(nq * L * LANES, p), eye_p, chan_group,
               lambda c: c // p).reshape(nq, L * LANES, gq * p) for e in (e_re, e_im)], axis=2)
    w_out = jnp.concatenate(
        [embed(tile(o.transpose(0, 2, 3, 1), 1).reshape(nq * gq * p, L * hc), jnp.kron(jnp.eye(L, dtype=F32), eye_c),
               lambda r: (r // p) % gq, chan_group).reshape(nq, gq * p, L * LANES) for o in (o_re, o_im)], axis=1)
    lam_re = pw_re[..., L].reshape(1, g * p)
    lam_im = pw_im[..., L].reshape(1, g * p)
    return lag_blocks, w_end, w_out, lam_re, lam_im


def _s5_end_kernel(u_ref, w_ref, ere_ref, eim_ref):
    e = jnp.dot(u_ref[0], w_ref[0], preferred_element_type=F32)
    ns = ere_ref.shape[0]
    for k in range(ns):
        ere_ref[k] = e[:, k * LANES:(k + 1) * LANES]
        eim_ref[k] = e[:, (ns + k) * LANES:(ns + k + 1) * LANES]


def _s5_scan_kernel(lre_ref, lim_ref, ere_ref, eim_ref, hre_ref, him_ref, *, batch, nchunk):
    ns = ere_ref.shape[0]
    lr = [lre_ref[k] for k in range(ns)]
    li = [lim_ref[k] for k in range(ns)]

    def step(j, carry):
        sl = pl.ds(j, batch, stride=nchunk)
        out = []
        for k in range(ns):
            hr, hi = carry[2 * k], carry[2 * k + 1]
            hre_ref[k, sl, :] = hr
            him_ref[k, sl, :] = hi
            out += [lr[k] * hr - li[k] * hi + ere_ref[k, sl, :], lr[k] * hi + li[k] * hr + eim_ref[k, sl, :]]
        return tuple(out)

    z = jnp.zeros((batch, LANES), F32)
    lax.fori_loop(0, nchunk, step, (z,) * (2 * ns))


def _s5_out_kernel(u_ref, lag_ref, hre_ref, him_ref, w_ref, y_ref, t_ref):
    L = lag_ref.shape[1]

    @pl.when(pl.program_id(1) == 0)
    def _():
        for s in range(L):
            for t in range(s, L):
                t_ref[s * LANES:(s + 1) * LANES, t * LANES:(t + 1) * LANES] = lag_ref[0, t - s]
            if s % 2 == 1:
                t_ref[s * LANES:(s + 1) * LANES, (s - 1) * LANES:s * LANES] = jnp.zeros((LANES, LANES), BF16)

    half = w_ref.shape[1] // 2
    ns = hre_ref.shape[0]
    h_re = jnp.concatenate([hre_ref[k] for k in range(ns)], axis=1).astype(BF16)
    h_im = jnp.concatenate([him_ref[k] for k in range(ns)], axis=1).astype(BF16)
    inter = jnp.dot(h_re, w_ref[0, :half, :], preferred_element_type=F32)
    inter = inter + jnp.dot(h_im, w_ref[0, half:, :], preferred_element_type=F32)
    ct = 2 * LANES
    for c0 in range(0, t_ref.shape[1], ct):
        k_hi = c0 + ct
        y = jnp.dot(u_ref[0, :, :k_hi], t_ref[:k_hi, c0:c0 + ct], preferred_element_type=F32)
        y_ref[0, :, c0:c0 + ct] = jax.nn.gelu(y + inter[:, c0:c0 + ct])


def _s5(u4, tables, batch, seq):
    lag_blocks, w_end, w_out, lam_re, lam_im = tables
    L = S5_CHUNK
    nq, rows, kc = u4.shape
    nchunk = seq // L
    sw = w_end.shape[2] // 2
    ns = sw // LANES
    rb = rows // 2
    e_re, e_im = pl.pallas_call(
        _s5_end_kernel,
        grid=(nq,),
        in_specs=[pl.BlockSpec((1, rows, kc), lambda q: (q, 0, 0)),
                  pl.BlockSpec((1, kc, 2 * sw), lambda q: (q, 0, 0))],
        out_specs=[pl.BlockSpec((ns, rows, LANES), lambda q: (q, 0, 0))] * 2,
        out_shape=[jax.ShapeDtypeStruct((nq * ns, rows, LANES), F32)] * 2,
        compiler_params=_cparams("parallel"),
        name="s5_chunk_end",
    )(u4, w_end)
    lam_spec = pl.BlockSpec((ns, 1, LANES), lambda q: (q, 0, 0))
    st_spec = pl.BlockSpec((ns, rows, LANES), lambda q: (q, 0, 0))
    h_re, h_im = pl.pallas_call(
        functools.partial(_s5_scan_kernel, batch=batch, nchunk=nchunk),
        grid=(nq,),
        in_specs=[lam_spec, lam_spec, st_spec, st_spec],
        out_specs=[st_spec] * 2,
        out_shape=[jax.ShapeDtypeStruct(e_re.shape, F32)] * 2,
        compiler_params=_cparams("parallel"),
        name="s5_chunk_scan",
    )(lam_re.reshape(nq * ns, 1, LANES), lam_im.reshape(nq * ns, 1, LANES), e_re, e_im)
    hs_spec = pl.BlockSpec((ns, rb, LANES), lambda q, r: (q, r, 0))
    return pl.pallas_call(
        _s5_out_kernel,
        grid=(nq, rows // rb),
        in_specs=[pl.BlockSpec((1, rb, kc), lambda q, r: (q, r, 0)),
                  pl.BlockSpec((1, L, LANES, LANES), lambda q, r: (q, 0, 0, 0)),
                  hs_spec, hs_spec,
                  pl.BlockSpec((1, 2 * sw, kc), lambda q, r: (q, 0, 0))],
        out_specs=pl.BlockSpec((1, rb, kc), lambda q, r: (q, r, 0)),
        out_shape=jax.ShapeDtypeStruct((nq, rows, kc), F32),
        scratch_shapes=[pltpu.VMEM((kc, kc), BF16)],
        compiler_params=_cparams("parallel", "arbitrary"),
        name="s5_out",
    )(u4, lag_blocks, h_re, h_im, w_out)


def _rows_to_cols(eye3, sub, a_row, b_row):
    a3 = jnp.concatenate(_split3(a_row), axis=1)
    b3 = jnp.concatenate(_split3(b_row), axis=1)
    rows = jnp.where(sub < LANES, a3, b3).astype(BF16)
    cols = lax.dot_general(eye3, rows, NT_DIMS, preferred_element_type=F32)
    return cols[:, :LANES], cols[:, LANES:]


def _mlstm_kernel(gb_ref, g_ref, q_ref, k_ref, v_ref, o_ref, h_ref, c_ref, b_scr, i_scr, *, L, seq, dh, nh):
    tri = (_iota((L, L), 0) <= _iota((L, L), 1)).astype(F32)
    eye = jnp.concatenate([(_iota((L, L), 0) == _iota((L, L), 1)).astype(BF16)] * 3, axis=1)
    causal = _iota((L, L), 1) <= _iota((L, L), 0)
    lane = _iota((L, LANES), 1)
    one_col = (lane == 0).astype(BF16)
    sub = _iota((2 * LANES, 3 * L), 0)
    c_ref[...] = jnp.zeros_like(c_ref)
    for hh in range(nh):
        log_f = _log_sigmoid(g_ref[0, 0, hh, 1] + gb_ref[0, hh, 1])
        b_scr[hh] = jnp.dot(log_f, tri, precision=HIGHEST, preferred_element_type=F32)
        i_scr[hh] = g_ref[0, 0, hh, 0] + gb_ref[0, hh, 0]

    def chunk(c, carry):
        st = pl.multiple_of(c * L, L)
        heads = range(nh)
        cols_h = [slice(hh * dh, (hh + 1) * dh) for hh in heads]
        b_row = [b_scr[hh, pl.ds(c, 1), :] for hh in heads]
        li_row = [i_scr[hh, pl.ds(c, 1), :] for hh in heads]
        cols = [_rows_to_cols(eye, sub, b_row[hh], li_row[hh]) for hh in heads]
        q = [q_ref[pl.ds(st, L), cols_h[hh]] for hh in heads]
        k = [k_ref[pl.ds(st, L), cols_h[hh]] for hh in heads]
        v = [v_ref[pl.ds(st, L), cols_h[hh]] for hh in heads]
        qk = [lax.dot_general(q[hh], k[hh], NT_DIMS, preferred_element_type=F32) for hh in heads]
        qc = [jnp.dot(q[hh], c_ref[hh].astype(BF16), preferred_element_type=F32) for hh in heads]

        s, m_t, m_inter, m_new, w_col, decay = [], [], [], [], [], []
        for hh in heads:
            m_prev = carry[hh]
            b_full, li_full = cols[hh]
            b_col = b_full[:, 0:1]
            b_last = b_row[hh][:, L - 1:L]
            log_d = jnp.where(causal, jnp.tile(b_full, (1, L // LANES)) - b_row[hh] + li_row[hh], NEG_INF)
            m_inter.append(b_col + m_prev)
            m_t.append(jnp.maximum(m_inter[hh], jnp.max(log_d, axis=1, keepdims=True)))
            s.append((qk[hh] * jnp.exp(log_d - m_t[hh])).astype(BF16))
            g_row = b_last - b_row[hh] + li_row[hh]
            m_new.append(jnp.maximum(b_last + m_prev, jnp.max(g_row, axis=1, keepdims=True)))
            w_col.append(jnp.exp(b_last - b_full + li_full - m_new[hh]))
            decay.append(jnp.exp(b_last + m_prev - m_new[hh]))

        tot = [jnp.dot(s[hh], jnp.concatenate([v[hh], one_col], axis=1), preferred_element_type=F32)
               for hh in heads]
        upd = [lax.dot_general(k[hh], jnp.concatenate(
            [(v[hh].astype(F32) * w_col[hh]).astype(BF16), jnp.where(lane == 0, w_col[hh], 0.0).astype(BF16)],
            axis=1), TN_DIMS, preferred_element_type=F32) for hh in heads]
        for hh in heads:
            t_h = tot[hh] + jnp.exp(m_inter[hh] - m_t[hh]) * qc[hh]
            den = jnp.maximum(jnp.abs(t_h[:, dh:dh + 1]), jnp.exp(-m_t[hh]))
            h = t_h[:, :dh] / den
            h_ref[pl.ds(st, L), cols_h[hh]] = (h * jax.nn.sigmoid(o_ref[pl.ds(st, L), cols_h[hh]])).astype(h_ref.dtype)
            c_ref[hh] = decay[hh] * c_ref[hh] + upd[hh]
        return tuple(m_new)

    lax.fori_loop(0, seq // L, chunk, (jnp.zeros((1, 1), F32),) * nh)


def _mlstm(q, k, v, o, gate_rows, gate_bias, batch, seq, heads, dh):
    t = q.shape[0]
    L = MLSTM_CHUNK
    nh = MLSTM_HEADS_PER_STEP
    nc = seq // L
    col = pl.BlockSpec((seq, nh * dh), lambda b, h: (b, h))
    return pl.pallas_call(
        functools.partial(_mlstm_kernel, L=L, seq=seq, dh=dh, nh=nh),
        grid=(batch, heads // nh),
        in_specs=[pl.BlockSpec((1, nh, 2, 1, 1), lambda b, h: (h, 0, 0, 0, 0)),
                  pl.BlockSpec((1, 1, nh, 2, nc, L), lambda b, h: (b, h, 0, 0, 0, 0)),
                  col, col, col, col],
        out_specs=col,
        out_shape=jax.ShapeDtypeStruct((t, heads * dh), BF16),
        scratch_shapes=[pltpu.VMEM((nh, dh, 2 * dh), F32), pltpu.VMEM((nh, nc, L), F32),
                        pltpu.VMEM((nh, nc, L), F32)],
        compiler_params=_cparams("parallel", "parallel"),
        name="mlstm",
    )(gate_bias, gate_rows, q, k, v, o)


def _out_even_kernel(h_ref, att_ref, ys4_ref, wg_ref, bg_ref, wo_ref, g_ref, b_ref, o_ref, ys_ref, *, alpha):
    L = S5_CHUNK
    nq = ys4_ref.shape[0]
    nchunk = ys_ref.shape[1] // L
    for q in range(nq):
        for t in range(L):
            ys_ref[q, pl.ds(t, nchunk, stride=L), :] = ys4_ref[q, :, t * LANES:(t + 1) * LANES]
    ys = jnp.concatenate([ys_ref[q] for q in range(nq)], axis=1)
    half = att_ref.shape[1]
    gate = jax.nn.sigmoid(jnp.dot(ys.astype(BF16), wg_ref[...], preferred_element_type=F32) + bg_ref[...])
    mix = jnp.dot(att_ref[...], wo_ref[:half, :], preferred_element_type=F32)
    mix = mix + jnp.dot((ys * gate).astype(BF16), wo_ref[half:, :], preferred_element_type=F32)
    o_ref[...] = _layer_norm(alpha * h_ref[...] + mix, g_ref[...], b_ref[...])


def _out_odd_kernel(h_ref, hm_ref, wo_ref, g_ref, b_ref, o_ref, *, alpha):
    mix = jnp.dot(hm_ref[...], wo_ref[...], preferred_element_type=F32)
    o_ref[...] = _layer_norm(alpha * h_ref[...] + mix, g_ref[...], b_ref[...])


def _row_tiled_call(kern, row_args, fixed_args, out_dtype, name):
    t = row_args[0].shape[0]
    tm = TM_PROJ
    in_specs = [pl.BlockSpec((tm, a.shape[1]), lambda i: (i, 0)) for a in row_args]
    in_specs += [pl.BlockSpec(a.shape, lambda i: (0, 0)) for a in fixed_args]
    d = row_args[0].shape[1]
    return pl.pallas_call(
        kern,
        grid=(t // tm,),
        in_specs=in_specs,
        out_specs=pl.BlockSpec((tm, d), lambda i: (i, 0)),
        out_shape=jax.ShapeDtypeStruct((t, d), out_dtype),
        compiler_params=_cparams("parallel"),
        name=name,
    )(*row_args, *fixed_args)


def _router_kernel(h_ref, w_ref, b_ref, o_ref, cnt_ref, run_ref, w2_ref, *, n_groups, epg):
    @pl.when(pl.program_id(0) == 0)
    def _():
        run_ref[...] = jnp.zeros_like(run_ref)
        w = w_ref[...]
        w_hi = w.astype(BF16)
        w2_ref[:, :LANES] = w_hi
        w2_ref[:, LANES:] = (w - w_hi.astype(F32)).astype(BF16)

    h = h_ref[...]
    h_hi = h.astype(BF16)
    h_lo = (h - h_hi.astype(F32)).astype(BF16)
    part = jnp.dot(h_hi, w2_ref[...], preferred_element_type=F32)
    logits = (part[:, :LANES] + (part[:, LANES:] + jnp.dot(h_lo, w2_ref[:, :LANES], preferred_element_type=F32))
              + b_ref[...])
    lane = _iota(logits.shape, 1)
    big = jnp.int32(LANES)
    lg = jnp.where(lane < n_groups, logits, NEG_INF)
    mg = jnp.max(lg, axis=1, keepdims=True)
    g_val = 1.0 / jnp.sum(jnp.exp(lg - mg), axis=1, keepdims=True)
    g_idx = jnp.min(jnp.where(lg == mg, lane, big), axis=1, keepdims=True)
    lo_lane = n_groups + g_idx * epg
    le = jnp.where((lane >= lo_lane) & (lane < lo_lane + epg), logits, NEG_INF)
    m1 = jnp.max(le, axis=1, keepdims=True)
    i1 = jnp.min(jnp.where(le == m1, lane, big), axis=1, keepdims=True)
    le2 = jnp.where(lane == i1, NEG_INF, le)
    m2 = jnp.max(le2, axis=1, keepdims=True)
    i2 = jnp.min(jnp.where(le2 == m2, lane, big), axis=1, keepdims=True)
    r = jnp.exp(m2 - m1)
    w1 = g_val / (1.0 + r)
    w2 = g_val * r / (1.0 + r)
    e1 = i1 - lo_lane
    e2 = i2 - lo_lane
    first_lo = e1 < e2
    lo = jnp.where(first_lo, e1, e2)
    hi = jnp.where(first_lo, e2, e1)
    w_lo = jnp.where(first_lo, w1, w2)
    w_hi = jnp.where(first_lo, w2, w1)
    pair = (lo * (2 * epg - 1 - lo)) // 2 + (hi - lo - 1)
    cls = g_idx * (epg * (epg - 1) // 2) + pair
    tm = h.shape[0]
    onehot = lane == cls
    earlier = (_iota((tm, tm), 1) < _iota((tm, tm), 0)).astype(BF16)
    before = jnp.dot(earlier, onehot.astype(BF16), preferred_element_type=F32) + run_ref[...]
    rank = jnp.sum(jnp.where(onehot, before, 0.0), axis=1, keepdims=True)
    run = run_ref[...] + jnp.sum(onehot.astype(F32), axis=0, keepdims=True)
    run_ref[...] = run
    cnt_ref[...] = run
    out = jnp.where(lane == 0, cls.astype(F32),
                    jnp.where(lane == 1, w_lo, jnp.where(lane == 2, w_hi, jnp.where(lane == 3, rank, 0.0))))
    o_ref[...] = out


def _router(h, w_r, b_r, n_groups, epg):
    t, d = h.shape
    tm = TM_PROJ
    return pl.pallas_call(
        functools.partial(_router_kernel, n_groups=n_groups, epg=epg),
        grid=(t // tm,),
        in_specs=[pl.BlockSpec((tm, d), lambda i: (i, 0)),
                  pl.BlockSpec(w_r.shape, lambda i: (0, 0)),
                  pl.BlockSpec(b_r.shape, lambda i: (0, 0))],
        out_specs=[pl.BlockSpec((tm, LANES), lambda i: (i, 0)), pl.BlockSpec((1, LANES), lambda i: (0, 0))],
        out_shape=[jax.ShapeDtypeStruct((t, LANES), F32), jax.ShapeDtypeStruct((1, LANES), F32)],
        scratch_shapes=[pltpu.VMEM((1, LANES), F32), pltpu.VMEM((d, 2 * LANES), BF16)],
        compiler_params=_cparams("arbitrary"),
        name="router",
    )(h, w_r, b_r)


def _rows_wait(buf, sem):
    pltpu.make_async_copy(buf, buf, sem).wait()


def _dispatch_kernel(pos_ref, cnt_ref, pend_ref, x_ref, r_ref, o_hbm, xbuf, sem, zsem, *, tm, nsteps, ztile):
    i = pl.program_id(0)
    slot = i % 2
    d = x_ref.shape[1]

    @pl.when(i == 0)
    def _():
        zeros = xbuf.at[1, pl.ds(0, ztile)]
        xbuf[1] = jnp.zeros(xbuf.shape[1:], F32)
        ncls = cnt_ref.shape[0]
        used = pend_ref[ncls - 1]
        firsts = [(cnt_ref[c] > 0, pend_ref[c] - ztile) for c in range(ncls)]
        firsts += [(used + k * ztile < o_hbm.shape[0], used + k * ztile) for k in range(ncls)]
        for cond, first in firsts:
            @pl.when(cond)
            def _():
                pltpu.make_async_copy(zeros, o_hbm.at[pl.ds(pl.multiple_of(first, ztile), ztile)], zsem).start()
        for cond, _ in firsts:
            @pl.when(cond)
            def _():
                pltpu.make_async_copy(zeros, o_hbm.at[pl.ds(0, ztile)], zsem).wait()

    @pl.when(i >= 2)
    def _():
        _rows_wait(xbuf.at[slot], sem.at[slot])

    xbuf[slot, :, :d] = x_ref[...]
    xbuf[slot, :, d:] = r_ref[...]

    def body(r, c):
        pltpu.make_async_copy(xbuf.at[slot, pl.ds(r, 1)], o_hbm.at[pl.ds(pos_ref[0, 0, r], 1)], sem.at[slot]).start()
        return c
    lax.fori_loop(0, tm, body, 0, unroll=8)

    @pl.when(i == nsteps - 1)
    def _():
        _rows_wait(xbuf.at[slot], sem.at[slot])
        if nsteps > 1:
            _rows_wait(xbuf.at[1 - slot], sem.at[1 - slot])


def _dispatch(h, route, pos3, counts, pend, p_rows):
    t, d = h.shape
    w = d + route.shape[1]
    tm = pos3.shape[2]
    assert TM_MOE <= tm
    nsteps = t // tm
    smem = pl.BlockSpec(memory_space=pltpu.SMEM)
    return pl.pallas_call(
        functools.partial(_dispatch_kernel, tm=tm, nsteps=nsteps, ztile=TM_MOE),
        grid=(nsteps,),
        in_specs=[pl.BlockSpec((1, 1, tm), lambda i: (i, 0, 0), memory_space=pltpu.SMEM), smem, smem,
                  pl.BlockSpec((tm, d), lambda i: (i, 0)), pl.BlockSpec((tm, route.shape[1]), lambda i: (i, 0))],
        out_specs=pl.BlockSpec(memory_space=pl.ANY),
        out_shape=jax.ShapeDtypeStruct((p_rows, w), F32),
        scratch_shapes=[pltpu.VMEM((2, tm, w), F32), pltpu.SemaphoreType.DMA((2,)), pltpu.SemaphoreType.DMA(())],
        compiler_params=_cparams("arbitrary"),
        name="moe_dispatch",
    )(pos3, counts, pend, h, route)


def _combine_kernel(pos_ref, nxt_ref, h_ref, g_ref, b_ref, y_hbm, o_ref, ybuf, sem, *, alpha, tm, nsteps):
    i = pl.program_id(0)
    slot = i % 2

    def gather(p_ref, s):
        def body(r, c):
            pltpu.make_async_copy(y_hbm.at[pl.ds(p_ref[0, 0, r], 1)], ybuf.at[s, pl.ds(r, 1)], sem.at[s]).start()
            return c
        lax.fori_loop(0, tm, body, 0, unroll=8)

    @pl.when(i == 0)
    def _():
        gather(pos_ref, 0)

    @pl.when(i + 1 < nsteps)
    def _():
        gather(nxt_ref, 1 - slot)

    _rows_wait(ybuf.at[slot], sem.at[slot])
    o_ref[...] = _layer_norm(alpha * h_ref[...] + ybuf[slot], g_ref[...], b_ref[...])


def _combine(h, pos3, y_sorted, ln_g, ln_b, alpha):
    t, d = h.shape
    tm = pos3.shape[2]
    nsteps = t // tm
    row = lambda i: (i, 0)
    fixed = lambda i: (0, 0)
    return pl.pallas_call(
        functools.partial(_combine_kernel, alpha=alpha, tm=tm, nsteps=nsteps),
        grid=(nsteps,),
        in_specs=[pl.BlockSpec((1, 1, tm), lambda i: (i, 0, 0), memory_space=pltpu.SMEM),
                  pl.BlockSpec((1, 1, tm), lambda i: (jnp.minimum(i + 1, nsteps - 1), 0, 0), memory_space=pltpu.SMEM),
                  pl.BlockSpec((tm, d), row),
                  pl.BlockSpec((1, d), fixed), pl.BlockSpec((1, d), fixed),
                  pl.BlockSpec(memory_space=pl.ANY)],
        out_specs=pl.BlockSpec((tm, d), row),
        out_shape=jax.ShapeDtypeStruct((t, d), F32),
        scratch_shapes=[pltpu.VMEM((2, tm, d), F32), pltpu.SemaphoreType.DMA((2,))],
        compiler_params=_cparams("arbitrary"),
        name="moe_combine_ln",
    )(pos3, pos3, h, ln_g.reshape(1, -1), ln_b.reshape(1, -1), y_sorted)


def _moe_kernel(elo_ref, ehi_ref, nlive_ref, x_ref, g0_ref, u0_ref, d0_ref, g1_ref, u1_ref, d1_ref, y_ref,
                wg_ref, wu_ref, wd_ref):
    i = pl.program_id(0)
    d = y_ref.shape[1]
    live = i < nlive_ref[0]
    prev = jnp.maximum(i - 1, 0)
    experts = ((elo_ref, g0_ref, u0_ref, d0_ref), (ehi_ref, g1_ref, u1_ref, d1_ref))

    for j, (e_ref, g_ref, u_ref, d_ref) in enumerate(experts):
        @pl.when(live & ((i == 0) | (e_ref[i] != e_ref[prev])))
        def _():
            wg_ref[j] = g_ref[0].astype(BF16)
            wu_ref[j] = u_ref[0].astype(BF16)
            wd_ref[j] = d_ref[0].astype(BF16)

    @pl.when(jnp.logical_not(live))
    def _():
        y_ref[...] = jnp.zeros_like(y_ref)

    @pl.when(live)
    def _():
        x = x_ref[:, :d].astype(BF16)
        route = x_ref[:, d:]
        y = None
        for j in range(2):
            gate = jnp.dot(x, wg_ref[j], preferred_element_type=F32)
            up = jnp.dot(x, wu_ref[j], preferred_element_type=F32)
            hid = gate * jax.nn.sigmoid(gate) * up * route[:, j + 1:j + 2]
            part = jnp.dot(hid.astype(BF16), wd_ref[j], preferred_element_type=F32)
            y = part if y is None else y + part
        y_ref[...] = y


def _moe_experts(x_sorted, e_lo, e_hi, n_live, w_gate, w_up, w_down):
    p, xw = x_sorted.shape
    d = w_gate.shape[1]
    tm = TM_MOE
    f = w_gate.shape[2]
    lo_in = pl.BlockSpec((1, d, f), lambda i, lo, hi, nl: (lo[i], 0, 0))
    hi_in = pl.BlockSpec((1, d, f), lambda i, lo, hi, nl: (hi[i], 0, 0))
    lo_dn = pl.BlockSpec((1, f, d), lambda i, lo, hi, nl: (lo[i], 0, 0))
    hi_dn = pl.BlockSpec((1, f, d), lambda i, lo, hi, nl: (hi[i], 0, 0))
    grid_spec = pltpu.PrefetchScalarGridSpec(
        num_scalar_prefetch=3,
        grid=(p // tm,),
        in_specs=[pl.BlockSpec((tm, xw), lambda i, lo, hi, nl: (jnp.minimum(i, nl[0] - 1), 0)),
                  lo_in, lo_in, lo_dn, hi_in, hi_in, hi_dn],
        out_specs=pl.BlockSpec((tm, d), lambda i, lo, hi, nl: (i, 0)),
        scratch_shapes=[pltpu.VMEM((2, d, f), BF16), pltpu.VMEM((2, d, f), BF16), pltpu.VMEM((2, f, d), BF16)],
    )
    return pl.pallas_call(
        _moe_kernel,
        grid_spec=grid_spec,
        out_shape=jax.ShapeDtypeStruct((p, d), F32),
        compiler_params=_cparams("arbitrary"),
        name="moe_experts",
    )(e_lo, e_hi, n_live, x_sorted, w_gate, w_up, w_down, w_gate, w_up, w_down)


def _hier_moe_ln(h, ln_g, ln_b, alpha, w_group, b_group, w_expert, b_expert, w_gate, w_up, w_down):
    t, d = h.shape
    n_groups, _, epg = w_expert.shape
    npairs = epg * (epg - 1) // 2
    ncls = n_groups * npairs
    tm = TM_MOE
    w_r = jnp.concatenate([w_group, w_expert.transpose(1, 0, 2).reshape(d, n_groups * epg)], axis=1)
    b_r = jnp.concatenate([b_group, b_expert.reshape(-1)])
    pad = LANES - w_r.shape[1]
    w_r = jnp.pad(w_r, ((0, 0), (0, pad)))
    b_r = jnp.pad(b_r, (0, pad)).reshape(1, LANES)
    route, cnt = _router(h, w_r, b_r, n_groups, epg)

    cls = route[:, 0].astype(jnp.int32)
    rank = route[:, 3].astype(jnp.int32)
    counts = cnt[0, :ncls].astype(jnp.int32)
    padded = ((counts + tm - 1) // tm) * tm
    pend = jnp.cumsum(padded)
    pstart = pend - padded
    pos = jnp.sum(jnp.where(cls[:, None] == jnp.arange(ncls)[None, :], pstart[None, :], 0), axis=1) + rank
    pos3 = pos.astype(jnp.int32).reshape(t // TM_PROJ, 1, TM_PROJ)
    p_rows = t + ncls * tm
    x_sorted = _dispatch(h, route, pos3, counts, pend.astype(jnp.int32), p_rows)

    tile_start = jnp.arange(p_rows // tm, dtype=jnp.int32) * tm
    tile_start = jnp.minimum(tile_start, pend[-1] - tm)
    tile_cls = jnp.sum(pend[None, :] <= tile_start[:, None], axis=1).astype(jnp.int32)
    n_live = (pend[-1:] // tm).astype(jnp.int32)
    pair = tile_cls % npairs
    pair_lo = sum(jnp.where(pair == k, v, 0) for k, v in enumerate(PAIRS_LO))
    pair_hi = sum(jnp.where(pair == k, v, 0) for k, v in enumerate(PAIRS_HI))
    e_lo = ((tile_cls // npairs) * epg + pair_lo).astype(jnp.int32)
    e_hi = ((tile_cls // npairs) * epg + pair_hi).astype(jnp.int32)

    y_sorted = _moe_experts(x_sorted, e_lo, e_hi, n_live, w_gate, w_up, w_down)
    return _combine(h, pos3, y_sorted, ln_g, ln_b, alpha)


def _even_mixer(h, batch, seq, alpha, ln_g, ln_b, w_in, f_bias, s5_params, w_glu, b_glu, w_out):
    t, d = h.shape
    heads = f_bias.shape[0]
    groups, p_state = s5_params[0].shape
    hc = s5_params[3].shape[-1]
    s5_width = groups * hc
    fox_width = d - s5_width
    dh = fox_width // heads
    q_scale = dh ** -0.5
    w_q, w_k, w_v, w_f, w_u = jnp.split(w_in, [fox_width, 2 * fox_width, 3 * fox_width, 3 * fox_width + heads], axis=1)
    hp = heads // 2
    w_f = jnp.pad(w_f.reshape(d, hp, 2), ((0, 0), (0, 0), (0, LANES - 2))).reshape(d, hp * LANES)
    fb = jnp.pad(f_bias.reshape(hp, 1, 2), ((0, 0), (0, 0), (0, LANES - 2)))
    w_cat = jnp.concatenate([w_q * q_scale, w_k, w_v, w_u, w_f], axis=1)
    qkv, u4, f_cols = _proj_even(h, w_cat.astype(BF16), 3 * fox_width, s5_width, hp * LANES)
    att = _fox_attention(qkv, f_cols, fb, batch, seq, heads, dh)
    ys4 = _s5(u4, _s5_tables(*s5_params), batch, seq)
    tm = TM_PROJ
    row = lambda i: (i, 0)
    fixed = lambda i: (0, 0)
    fixed_args = (w_glu.astype(BF16), b_glu.reshape(1, -1), w_out.astype(BF16), ln_g.reshape(1, -1), ln_b.reshape(1, -1))
    return pl.pallas_call(
        functools.partial(_out_even_kernel, alpha=alpha),
        grid=(t // tm,),
        in_specs=[pl.BlockSpec((tm, d), row), pl.BlockSpec((tm, fox_width), row),
                  pl.BlockSpec((ys4.shape[0], tm // S5_CHUNK, ys4.shape[2]), lambda i: (0, i, 0))]
        + [pl.BlockSpec(a.shape, fixed) for a in fixed_args],
        out_specs=pl.BlockSpec((tm, d), row),
        out_shape=jax.ShapeDtypeStruct((t, d), F32),
        scratch_shapes=[pltpu.VMEM((ys4.shape[0], tm, LANES), F32)],
        compiler_params=_cparams("parallel"),
        name="out_even",
    )(h, att, ys4, *fixed_args)


def _odd_mixer(h, batch, seq, alpha, ln_g, ln_b, w_in, conv_w, conv_b, i_bias, f_bias, w_out):
    t, d = h.shape
    heads = i_bias.shape[0]
    dmix = conv_w.shape[1] // 2
    dh = dmix // heads
    w_main = w_in[:, :4 * dmix].astype(BF16)
    w_gates = jnp.pad(w_in[:, 4 * dmix:], ((0, 0), (0, LANES - 2 * heads))).astype(BF16)
    q, k, v, o, gates = _proj_odd(h, w_main, w_gates, conv_w, conv_b, seq, dh)
    nh = MLSTM_HEADS_PER_STEP
    g = gates[:, :2 * heads].reshape(batch, seq, 2, heads).transpose(0, 3, 2, 1)
    gate_rows = g.reshape(batch, heads // nh, nh, 2, seq // MLSTM_CHUNK, MLSTM_CHUNK)
    gate_bias = jnp.stack([i_bias, f_bias], axis=1).reshape(heads // nh, nh, 2, 1, 1)
    hm = _mlstm(q, k, v, o, gate_rows, gate_bias, batch, seq, heads, dh)
    kern = functools.partial(_out_odd_kernel, alpha=alpha)
    return _row_tiled_call(kern, (h, hm), (w_out.astype(BF16), ln_g.reshape(1, -1), ln_b.reshape(1, -1)),
                           F32, "out_odd")


def kernel(x, ln_g, ln_b, even_w_in, fox_f_bias, s5_a_re, s5_a_im, s5_log_dt, s5_b_re, s5_b_im, s5_c_re, s5_c_im, s5_d, s5_w_glu, s5_b_glu, even_w_out, odd_w_in, mlstm_conv_w, mlstm_conv_b, mlstm_i_bias, mlstm_f_bias, odd_w_out, moe_w_group, moe_b_group, moe_w_expert, moe_b_expert, moe_w_gate, moe_w_up, moe_w_down):
    batch, seq, d = x.shape
    depth = ln_g.shape[0]
    alpha = (2 * depth) ** 0.25
    h = x.reshape(batch * seq, d)
    for layer in range(depth):
        j = layer // 2
        if layer % 2 == 0:
            s5_params = (s5_a_re[j], s5_a_im[j], s5_log_dt[j], s5_b_re[j], s5_b_im[j],
                         s5_c_re[j], s5_c_im[j], s5_d[j])
            h = _even_mixer(h, batch, seq, alpha, ln_g[layer, 0], ln_b[layer, 0], even_w_in[j], fox_f_bias[j],
                            s5_params, s5_w_glu[j], s5_b_glu[j], even_w_out[j])
        else:
            h = _odd_mixer(h, batch, seq, alpha, ln_g[layer, 0], ln_b[layer, 0], odd_w_in[j], mlstm_conv_w[j],
                           mlstm_conv_b[j], mlstm_i_bias[j], mlstm_f_bias[j], odd_w_out[j])
        h = _hier_moe_ln(h, ln_g[layer, 1], ln_b[layer, 1], alpha, moe_w_group[layer], moe_b_group[layer],
                         moe_w_expert[layer], moe_b_expert[layer], moe_w_gate[layer], moe_w_up[layer],
                         moe_w_down[layer])
    return h.reshape(batch, seq, d)
```

```python
import functools

import jax
import jax.numpy as jnp
from jax import lax
from jax.experimental import pallas as pl
from jax.experimental.pallas import tpu as pltpu

F32 = jnp.float32
BF16 = jnp.bfloat16
HIGHEST = lax.Precision.HIGHEST
LN_EPS = 1e-5
NEG_INF = float("-inf")

LANES = 128
VMEM_LIMIT = 56 * 1024 * 1024

TM_PROJ = 256
ATT_BLOCK = 256
ATT_QUERY_BLOCK = 512
S5_CHUNK = 16
MLSTM_CHUNK = 256
MLSTM_HEADS_PER_STEP = 4
TM_MOE = 256
PAIRS_LO = (0, 0, 0, 1, 1, 2)
PAIRS_HI = (1, 2, 3, 2, 3, 3)

NT_DIMS = (((1,), (1,)), ((), ()))
TN_DIMS = (((0,), (0,)), ((), ()))


def _cparams(*sem):
    return pltpu.CompilerParams(dimension_semantics=sem, vmem_limit_bytes=VMEM_LIMIT)


def _log_sigmoid(x):
    return jnp.minimum(x, 0.0) - jnp.log1p(jnp.exp(-jnp.abs(x)))


def _layer_norm(x, g, b):
    mu = jnp.mean(x, axis=-1, keepdims=True)
    xc = x - mu
    var = jnp.mean(xc * xc, axis=-1, keepdims=True)
    return xc * lax.rsqrt(var + LN_EPS) * g + b


def _iota(shape, dim):
    return lax.broadcasted_iota(jnp.int32, shape, dim)


def _proj_even_kernel(x_ref, w_ref, qkv_ref, u_ref, f_ref, zs_ref, *, n_qkv, n_u, n_f):
    xb = x_ref[...].astype(BF16)
    qkv_ref[...] = jnp.dot(xb, w_ref[:, :n_qkv], preferred_element_type=F32).astype(BF16)
    f_ref[...] = jnp.dot(xb, w_ref[:, n_qkv + n_u:n_qkv + n_u + n_f], preferred_element_type=F32)
    z = jnp.dot(xb, w_ref[:, n_qkv:n_qkv + n_u], preferred_element_type=F32)
    L = S5_CHUNK
    nchunk = x_ref.shape[0] // L
    for q in range(n_u // LANES):
        zs_ref[q] = z[:, q * LANES:(q + 1) * LANES]
        for s in range(L):
            u_ref[q, :, s * LANES:(s + 1) * LANES] = zs_ref[q, pl.ds(s, nchunk, stride=L), :].astype(BF16)


def _proj_even(x, w, n_qkv, n_u, n_f):
    t, d = x.shape
    tm = TM_PROJ
    L = S5_CHUNK
    nq = n_u // LANES
    return pl.pallas_call(
        functools.partial(_proj_even_kernel, n_qkv=n_qkv, n_u=n_u, n_f=n_f),
        grid=(t // tm,),
        in_specs=[pl.BlockSpec((tm, d), lambda i: (i, 0)),
                  pl.BlockSpec(w.shape, lambda i: (0, 0))],
        out_specs=[pl.BlockSpec((tm, n_qkv), lambda i: (i, 0)),
                   pl.BlockSpec((nq, tm // L, L * LANES), lambda i: (0, i, 0)),
                   pl.BlockSpec((tm, n_f), lambda i: (i, 0))],
        out_shape=[jax.ShapeDtypeStruct((t, n_qkv), BF16),
                   jax.ShapeDtypeStruct((nq, t // L, L * LANES), BF16),
                   jax.ShapeDtypeStruct((t, n_f), F32)],
        scratch_shapes=[pltpu.VMEM((nq, tm, LANES), F32)],
        compiler_params=_cparams("parallel"),
        name="proj_even",
    )(x, w)


def _proj_odd_kernel(x_ref, w_ref, cw_ref, cb_ref, q_ref, k_ref, v_ref, o_ref, g_ref, *zs_refs,
                     tm, dmix, k_scale, tiles_per_seq, conv_width):
    i = pl.program_id(0)
    xb = x_ref[...].astype(BF16)
    cw = zs_refs[0].shape[1]

    @pl.when(i % tiles_per_seq == 0)
    def _():
        for zs_ref in zs_refs:
            zs_ref[0:8, :] = jnp.zeros((8, cw), F32)

    gates = jnp.dot(xb, w_ref[:, 4 * dmix:], preferred_element_type=F32)
    g_ref[...] = jnp.concatenate([gates, jnp.zeros((tm, LANES - gates.shape[1]), F32)], axis=1)
    for zs_ref, c0 in zip(zs_refs, range(0, 2 * dmix, cw)):
        zs_ref[8:tm + 8, :] = jnp.dot(xb, w_ref[:, c0:c0 + cw], preferred_element_type=F32)
        vo = jnp.dot(xb, w_ref[:, 2 * dmix + c0:2 * dmix + c0 + cw], preferred_element_type=F32)
        if c0 < dmix:
            v_ref[:, c0:c0 + cw] = vo.astype(BF16)
        else:
            o_ref[:, c0 - dmix:c0 - dmix + cw] = vo
        cols = slice(c0, c0 + cw)
        acc = jnp.broadcast_to(cb_ref[:, cols], (tm, cw))
        for j in range(conv_width):
            acc = acc + cw_ref[j:j + 1, cols] * zs_ref[pl.ds(8 - (conv_width - 1) + j, tm), :]
        y = acc * jax.nn.sigmoid(acc)
        if c0 < dmix:
            q_ref[:, cols] = y.astype(BF16)
        else:
            k_ref[:, c0 - dmix:c0 - dmix + cw] = (y * k_scale).astype(BF16)
        zs_ref[0:8, :] = zs_ref[tm:tm + 8, :]


def _proj_odd(x, w, conv_w, conv_b, seq, head_dim):
    t, d = x.shape
    dmix = conv_w.shape[1] // 2
    tm = TM_PROJ
    kern = functools.partial(_proj_odd_kernel, tm=tm, dmix=dmix, k_scale=head_dim ** -0.5,
                             tiles_per_seq=seq // tm, conv_width=conv_w.shape[0])
    row = lambda i: (i, 0)
    fixed = lambda i: (0, 0)
    return pl.pallas_call(
        kern,
        grid=(t // tm,),
        in_specs=[pl.BlockSpec((tm, d), row), pl.BlockSpec(w.shape, fixed),
                  pl.BlockSpec(conv_w.shape, fixed), pl.BlockSpec((1, 2 * dmix), fixed)],
        out_specs=[pl.BlockSpec((tm, dmix), row)] * 4 + [pl.BlockSpec((tm, LANES), row)],
        out_shape=[jax.ShapeDtypeStruct((t, dmix), BF16)] * 3
        + [jax.ShapeDtypeStruct((t, dmix), F32), jax.ShapeDtypeStruct((t, LANES), F32)],
        scratch_shapes=[pltpu.VMEM((tm + 8, 512), F32)] * (2 * dmix // 512),
        compiler_params=_cparams("arbitrary"),
        name="proj_odd",
    )(x, w, conv_w, conv_b.reshape(1, -1))


def _split3(x):
    hi = x.astype(BF16).astype(F32)
    r = x - hi
    mid = r.astype(BF16).astype(F32)
    lo = (r - mid).astype(BF16).astype(F32)
    return hi, mid, lo


def _fox_kernel(fb_ref, f_ref, q_ref, k_ref, v_ref, o_ref, c_ref, kaug_ref, vt_ref, acc_ref, *, bq, blk, seq, dh):
    qi = pl.program_id(2)
    lane = _iota((blk, LANES), 1)
    head_lanes = (lane < dh, lane >= dh)

    @pl.when(qi == 0)
    def _():
        tril = (_iota((blk, blk), 1) <= _iota((blk, blk), 0)).astype(BF16)
        eye_b = (_iota((LANES, LANES), 0) == _iota((LANES, LANES), 1)).astype(BF16)

        def prep(jb, carry):
            rows = pl.ds(pl.multiple_of(jb * blk, blk), blk)
            ls = _log_sigmoid(f_ref[rows, :] + fb_ref[0])
            c3 = jnp.dot(tril, jnp.concatenate(_split3(ls), axis=1).astype(BF16), preferred_element_type=F32)
            c = c3[:, :LANES] + c3[:, LANES:2 * LANES] + c3[:, 2 * LANES:] + carry
            c_ref[rows, :] = c
            kblk = k_ref[rows, :]
            for j in range(2):
                hi, mid, lo = _split3(c[:, j:j + 1])
                aug = jnp.where(lane < 3, 1.0, jnp.where(lane == 3, -hi, jnp.where(
                    lane == 4, -mid, jnp.where(lane == 5, -lo, 0.0))))
                kaug_ref[j, rows, :] = jnp.concatenate(
                    [jnp.where(head_lanes[j], kblk, jnp.zeros_like(kblk)), aug.astype(BF16)], axis=1)
            vt_ref[jb] = lax.dot_general(eye_b, v_ref[rows, :], NT_DIMS,
                                         preferred_element_type=F32).astype(BF16)
            return c[blk - 1:blk, :]

        lax.fori_loop(0, seq // blk, prep, jnp.zeros((1, LANES), F32))

    q = q_ref[...]
    qlane = _iota((bq, LANES), 1)
    c_q = c_ref[pl.ds(pl.multiple_of(qi * bq, bq), bq), :]
    q_aug = []
    for j in range(2):
        hi, mid, lo = _split3(c_q[:, j:j + 1])
        aug = jnp.where(qlane == 0, hi, jnp.where(qlane == 1, mid, jnp.where(
            qlane == 2, lo, jnp.where(qlane < 6, 1.0, 0.0))))
        q_head = jnp.where((qlane < dh) if j == 0 else (qlane >= dh), q, jnp.zeros_like(q))
        q_aug.append(jnp.concatenate([q_head, aug.astype(BF16)], axis=1))
    acc_ref[...] = jnp.zeros_like(acc_ref)
    r = bq // blk

    def score(kb):
        krows = pl.ds(pl.multiple_of(kb * blk, blk), blk)
        return tuple(lax.dot_general(kaug_ref[j, krows, :], q_aug[j], NT_DIMS, preferred_element_type=F32)
                     for j in range(2))

    def accumulate(kb, probs, alphas):
        vt = vt_ref[kb]
        for j in range(2):
            acc_ref[j] = alphas[j] * acc_ref[j] + jnp.dot(vt[j * dh:(j + 1) * dh, :], probs[j],
                                                          preferred_element_type=F32)

    def softmax(scores, stats, diag):
        out, probs, alphas = [], [], []
        for j in range(2):
            m, l = stats[2 * j], stats[2 * j + 1]
            s = scores[j]
            if diag is not None:
                s = jnp.where(_iota((blk, bq), 0) + diag * blk <= _iota((blk, bq), 1), s, NEG_INF)
            m_new = jnp.maximum(m, jnp.max(s, axis=0, keepdims=True))
            alpha = jnp.exp(m - m_new)
            p = jnp.exp(s - m_new)
            out += [m_new, alpha * l + jnp.sum(p, axis=0, keepdims=True)]
            probs.append(p.astype(BF16))
            alphas.append(alpha)
        return tuple(out), tuple(probs), tuple(alphas)

    m0 = jnp.full((1, bq), NEG_INF, F32)
    l0 = jnp.zeros((1, bq), F32)

    def step(kb, state):
        scores, stats = state
        nxt = score(kb + 1)
        stats, probs, alphas = softmax(scores, stats, None)
        accumulate(kb, probs, alphas)
        return nxt, stats

    first = r * qi
    scores, stats = lax.fori_loop(0, first, step, (score(0), (m0, l0, m0, l0)))
    for diag in range(r):
        nxt = score(first + diag + 1) if diag + 1 < r else None
        stats, probs, alphas = softmax(scores, stats, diag)
        accumulate(first + diag, probs, alphas)
        scores = nxt
    out_t = jnp.concatenate([acc_ref[0] / stats[1], acc_ref[1] / stats[3]], axis=0).astype(BF16)
    eye_q = (_iota((blk, blk), 0) == _iota((blk, blk), 1)).astype(BF16)
    for c0 in range(0, bq, blk):
        o_ref[c0:c0 + blk, :] = lax.dot_general(eye_q, out_t[:, c0:c0 + blk], NT_DIMS,
                                                preferred_element_type=F32).astype(o_ref.dtype)


def _fox_attention(qkv, f_cols, f_bias, batch, seq, heads, dh):
    t = qkv.shape[0]
    blk = ATT_BLOCK
    bq = min(ATT_QUERY_BLOCK, seq)
    nq = seq // bq
    hp = heads * dh // LANES
    kern = functools.partial(_fox_kernel, bq=bq, blk=blk, seq=seq, dh=dh)
    return pl.pallas_call(
        kern,
        grid=(batch, hp, nq),
        in_specs=[pl.BlockSpec((1, 1, LANES), lambda b, p, i: (p, 0, 0)),
                  pl.BlockSpec((seq, LANES), lambda b, p, i: (b, p)),
                  pl.BlockSpec((bq, LANES), lambda b, p, i: (b * nq + i, p)),
                  pl.BlockSpec((seq, LANES), lambda b, p, i: (b, hp + p)),
                  pl.BlockSpec((seq, LANES), lambda b, p, i: (b, 2 * hp + p))],
        out_specs=pl.BlockSpec((bq, LANES), lambda b, p, i: (b * nq + i, p)),
        out_shape=jax.ShapeDtypeStruct((t, heads * dh), BF16),
        scratch_shapes=[pltpu.VMEM((seq, LANES), F32), pltpu.VMEM((2, seq, 2 * LANES), BF16),
                        pltpu.VMEM((seq // blk, LANES, blk), BF16), pltpu.VMEM((2, dh, bq), F32)],
        compiler_params=_cparams("parallel", "parallel", "arbitrary"),
        name="fox_attention",
    )(f_bias, f_cols, qkv, qkv, qkv)


def _s5_tables(a_re, a_im, log_dt, b_re, b_im, c_re, c_im, d_skip):
    L = S5_CHUNK
    g, p = a_re.shape
    hc = b_re.shape[-1]
    dt = jnp.exp(log_dt)[:, None]
    mag = jnp.exp(a_re * dt)
    lb_re = mag * jnp.cos(a_im * dt)
    lb_im = mag * jnp.sin(a_im * dt)
    num_re = lb_re - 1.0
    num_im = lb_im
    den = a_re * a_re + a_im * a_im
    z_re = (num_re * a_re + num_im * a_im) / den
    z_im = (num_im * a_re - num_re * a_im) / den
    bb_re = z_re[..., None] * b_re - z_im[..., None] * b_im
    bb_im = z_re[..., None] * b_im + z_im[..., None] * b_re
    tau = jnp.arange(L + 1, dtype=F32)
    pmag = jnp.exp((a_re * dt)[..., None] * tau)
    pw_re = pmag * jnp.cos((a_im * dt)[..., None] * tau)
    pw_im = pmag * jnp.sin((a_im * dt)[..., None] * tau)
    cp_re = c_re[..., None] * pw_re[:, None] - c_im[..., None] * pw_im[:, None]
    cp_im = c_re[..., None] * pw_im[:, None] + c_im[..., None] * pw_re[:, None]
    kern = (jnp.einsum("gopt,gpi->gtoi", cp_re[..., :L], bb_re, precision=HIGHEST)
            - jnp.einsum("gopt,gpi->gtoi", cp_im[..., :L], bb_im, precision=HIGHEST))
    kern = kern.at[:, 0].add(d_skip[:, :, None] * jnp.eye(hc, dtype=F32))
    rev = (L - 1) - jnp.arange(L)
    e_re = pw_re[:, :, rev][..., None] * bb_re[:, :, None] - pw_im[:, :, rev][..., None] * bb_im[:, :, None]
    e_im = pw_re[:, :, rev][..., None] * bb_im[:, :, None] + pw_im[:, :, rev][..., None] * bb_re[:, :, None]
    o_re = cp_re[..., 1:]
    o_im = -cp_im[..., 1:]

    gq = LANES // hc
    nq = g // gq

    def tile(m, perm):
        m = m.astype(BF16).reshape(nq, gq, *m.shape[1:])
        return jnp.moveaxis(m, 1, perm)

    def embed(m2, spread, row_group, col_group):
        full = jnp.dot(m2, spread.astype(BF16), preferred_element_type=F32)
        keep = row_group(jnp.arange(full.shape[0]))[:, None] == col_group(jnp.arange(full.shape[1]))[None, :]
        return jnp.where(keep, full, 0.0).astype(BF16)

    eye_c = jnp.tile(jnp.eye(hc, dtype=F32), (1, gq))
    eye_p = jnp.tile(jnp.eye(p, dtype=F32), (1, gq))
    chan_group = lambda r: (r // hc) % gq
    k_t = tile(kern.transpose(0, 1, 3, 2), 2).reshape(nq * L * LANES, hc)
    lag_blocks = embed(k_t, eye_c, chan_group, chan_group).reshape(nq, L, LANES, LANES)
    w_end = jnp.concatenate(
        [embed(tile(e.transpose(0, 2, 3, 1), 2).reshape(nq * L * LANES, p), eye_p, chan_group,
               lambda c: c // p).reshape(nq, L * LANES, gq * p) for e in (e_re, e_im)], axis=2)
    w_out = jnp.concatenate(
        [embed(tile(o.transpose(0, 2, 3, 1), 1).reshape(nq * gq * p, L * hc), jnp.kron(jnp.eye(L, dtype=F32), eye_c),
               lambda r: (r // p) % gq, chan_group).reshape(nq, gq * p, L * LANES) for o in (o_re, o_im)], axis=1)
    lam_re = pw_re[..., L].reshape(1, g * p)
    lam_im = pw_im[..., L].reshape(1, g * p)
    return lag_blocks, w_end, w_out, lam_re, lam_im


def _s5_end_kernel(u_ref, w_ref, ere_ref, eim_ref):
    e = jnp.dot(u_ref[0], w_ref[0], preferred_element_type=F32)
    ns = ere_ref.shape[0]
    for k in range(ns):
        ere_ref[k] = e[:, k * LANES:(k + 1) * LANES]
        eim_ref[k] = e[:, (ns + k) * LANES:(ns + k + 1) * LANES]


def _s5_scan_kernel(lre_ref, lim_ref, ere_ref, eim_ref, hre_ref, him_ref, *, batch, nchunk):
    ns = ere_ref.shape[0]
    lr = [lre_ref[k] for k in range(ns)]
    li = [lim_ref[k] for k in range(ns)]

    def step(j, carry):
        sl = pl.ds(j, batch, stride=nchunk)
        out = []
        for k in range(ns):
            hr, hi = carry[2 * k], carry[2 * k + 1]
            hre_ref[k, sl, :] = hr
            him_ref[k, sl, :] = hi
            out += [lr[k] * hr - li[k] * hi + ere_ref[k, sl, :], lr[k] * hi + li[k] * hr + eim_ref[k, sl, :]]
        return tuple(out)

    z = jnp.zeros((batch, LANES), F32)
    lax.fori_loop(0, nchunk, step, (z,) * (2 * ns))


def _s5_out_kernel(u_ref, lag_ref, hre_ref, him_ref, w_ref, y_ref, t_ref):
    L = lag_ref.shape[1]

    @pl.when(pl.program_id(1) == 0)
    def _():
        for s in range(L):
            for t in range(s, L):
                t_ref[s * LANES:(s + 1) * LANES, t * LANES:(t + 1) * LANES] = lag_ref[0, t - s]
            if s % 2 == 1:
                t_ref[s * LANES:(s + 1) * LANES, (s - 1) * LANES:s * LANES] = jnp.zeros((LANES, LANES), BF16)

    half = w_ref.shape[1] // 2
    ns = hre_ref.shape[0]
    h_re = jnp.concatenate([hre_ref[k] for k in range(ns)], axis=1).astype(BF16)
    h_im = jnp.concatenate([him_ref[k] for k in range(ns)], axis=1).astype(BF16)
    inter = jnp.dot(h_re, w_ref[0, :half, :], preferred_element_type=F32)
    inter = inter + jnp.dot(h_im, w_ref[0, half:, :], preferred_element_type=F32)
    ct = 2 * LANES
    for c0 in range(0, t_ref.shape[1], ct):
        k_hi = c0 + ct
        y = jnp.dot(u_ref[0, :, :k_hi], t_ref[:k_hi, c0:c0 + ct], preferred_element_type=F32)
        y_ref[0, :, c0:c0 + ct] = jax.nn.gelu(y + inter[:, c0:c0 + ct])


def _s5(u4, tables, batch, seq):
    lag_blocks, w_end, w_out, lam_re, lam_im = tables
    L = S5_CHUNK
    nq, rows, kc = u4.shape
    nchunk = seq // L
    sw = w_end.shape[2] // 2
    ns = sw // LANES
    rb = rows // 2
    e_re, e_im = pl.pallas_call(
        _s5_end_kernel,
        grid=(nq,),
        in_specs=[pl.BlockSpec((1, rows, kc), lambda q: (q, 0, 0)),
                  pl.BlockSpec((1, kc, 2 * sw), lambda q: (q, 0, 0))],
        out_specs=[pl.BlockSpec((ns, rows, LANES), lambda q: (q, 0, 0))] * 2,
        out_shape=[jax.ShapeDtypeStruct((nq * ns, rows, LANES), F32)] * 2,
        compiler_params=_cparams("parallel"),
        name="s5_chunk_end",
    )(u4, w_end)
    lam_spec = pl.BlockSpec((ns, 1, LANES), lambda q: (q, 0, 0))
    st_spec = pl.BlockSpec((ns, rows, LANES), lambda q: (q, 0, 0))
    h_re, h_im = pl.pallas_call(
        functools.partial(_s5_scan_kernel, batch=batch, nchunk=nchunk),
        grid=(nq,),
        in_specs=[lam_spec, lam_spec, st_spec, st_spec],
        out_specs=[st_spec] * 2,
        out_shape=[jax.ShapeDtypeStruct(e_re.shape, F32)] * 2,
        compiler_params=_cparams("parallel"),
        name="s5_chunk_scan",
    )(lam_re.reshape(nq * ns, 1, LANES), lam_im.reshape(nq * ns, 1, LANES), e_re, e_im)
    hs_spec = pl.BlockSpec((ns, rb, LANES), lambda q, r: (q, r, 0))
    return pl.pallas_call(
        _s5_out_kernel,
        grid=(nq, rows // rb),
        in_specs=[pl.BlockSpec((1, rb, kc), lambda q, r: (q, r, 0)),
                  pl.BlockSpec((1, L, LANES, LANES), lambda q, r: (q, 0, 0, 0)),
                  hs_spec, hs_spec,
                  pl.BlockSpec((1, 2 * sw, kc), lambda q, r: (q, 0, 0))],
        out_specs=pl.BlockSpec((1, rb, kc), lambda q, r: (q, r, 0)),
        out_shape=jax.ShapeDtypeStruct((nq, rows, kc), F32),
        scratch_shapes=[pltpu.VMEM((kc, kc), BF16)],
        compiler_params=_cparams("parallel", "arbitrary"),
        name="s5_out",
    )(u4, lag_blocks, h_re, h_im, w_out)


def _rows_to_cols(eye3, sub, a_row, b_row):
    a3 = jnp.concatenate(_split3(a_row), axis=1)
    b3 = jnp.concatenate(_split3(b_row), axis=1)
    rows = jnp.where(sub < LANES, a3, b3).astype(BF16)
    cols = lax.dot_general(eye3, rows, NT_DIMS, preferred_element_type=F32)
    return cols[:, :LANES], cols[:, LANES:]


def _mlstm_kernel(gb_ref, g_ref, q_ref, k_ref, v_ref, o_ref, h_ref, c_ref, b_scr, i_scr, *, L, seq, dh, nh):
    tri = (_iota((L, L), 0) <= _iota((L, L), 1)).astype(F32)
    eye = jnp.concatenate([(_iota((L, L), 0) == _iota((L, L), 1)).astype(BF16)] * 3, axis=1)
    causal = _iota((L, L), 1) <= _iota((L, L), 0)
    lane = _iota((L, LANES), 1)
    one_col = (lane == 0).astype(BF16)
    sub = _iota((2 * LANES, 3 * L), 0)
    c_ref[...] = jnp.zeros_like(c_ref)
    for hh in range(nh):
        log_f = _log_sigmoid(g_ref[0, 0, hh, 1] + gb_ref[0, hh, 1])
        b_scr[hh] = jnp.dot(log_f, tri, precision=HIGHEST, preferred_element_type=F32)
        i_scr[hh] = g_ref[0, 0, hh, 0] + gb_ref[0, hh, 0]

    def chunk(c, carry):
        st = pl.multiple_of(c * L, L)
        heads = range(nh)
        cols_h = [slice(hh * dh, (hh + 1) * dh) for hh in heads]
        b_row = [b_scr[hh, pl.ds(c, 1), :] for hh in heads]
        li_row = [i_scr[hh, pl.ds(c, 1), :] for hh in heads]
        cols = [_rows_to_cols(eye, sub, b_row[hh], li_row[hh]) for hh in heads]
        q = [q_ref[pl.ds(st, L), cols_h[hh]] for hh in heads]
        k = [k_ref[pl.ds(st, L), cols_h[hh]] for hh in heads]
        v = [v_ref[pl.ds(st, L), cols_h[hh]] for hh in heads]
        qk = [lax.dot_general(q[hh], k[hh], NT_DIMS, preferred_element_type=F32) for hh in heads]
        qc = [jnp.dot(q[hh], c_ref[hh].astype(BF16), preferred_element_type=F32) for hh in heads]

        s, m_t, m_inter, m_new, w_col, decay = [], [], [], [], [], []
        for hh in heads:
            m_prev = carry[hh]
            b_full, li_full = cols[hh]
            b_col = b_full[:, 0:1]
            b_last = b_row[hh][:, L - 1:L]
            log_d = jnp.where(causal, jnp.tile(b_full, (1, L // LANES)) - b_row[hh] + li_row[hh], NEG_INF)
            m_inter.append(b_col + m_prev)
            m_t.append(jnp.maximum(m_inter[hh], jnp.max(log_d, axis=1, keepdims=True)))
            s.append((qk[hh] * jnp.exp(log_d - m_t[hh])).astype(BF16))
            g_row = b_last - b_row[hh] + li_row[hh]
            m_new.append(jnp.maximum(b_last + m_prev, jnp.max(g_row, axis=1, keepdims=True)))
            w_col.append(jnp.exp(b_last - b_full + li_full - m_new[hh]))
            decay.append(jnp.exp(b_last + m_prev - m_new[hh]))

        tot = [jnp.dot(s[hh], jnp.concatenate([v[hh], one_col], axis=1), preferred_element_type=F32)
               for hh in heads]
        upd = [lax.dot_general(k[hh], jnp.concatenate(
            [(v[hh].astype(F32) * w_col[hh]).astype(BF16), jnp.where(lane == 0, w_col[hh], 0.0).astype(BF16)],
            axis=1), TN_DIMS, preferred_element_type=F32) for hh in heads]
        for hh in heads:
            t_h = tot[hh] + jnp.exp(m_inter[hh] - m_t[hh]) * qc[hh]
            den = jnp.maximum(jnp.abs(t_h[:, dh:dh + 1]), jnp.exp(-m_t[hh]))
            h = t_h[:, :dh] / den
            h_ref[pl.ds(st, L), cols_h[hh]] = (h * jax.nn.sigmoid(o_ref[pl.ds(st, L), cols_h[hh]])).astype(h_ref.dtype)
            c_ref[hh] = decay[hh] * c_ref[hh] + upd[hh]
        return tuple(m_new)

    lax.fori_loop(0, seq // L, chunk, (jnp.zeros((1, 1), F32),) * nh)


def _mlstm(q, k, v, o, gate_rows, gate_bias, batch, seq, heads, dh):
    t = q.shape[0]
    L = MLSTM_CHUNK
    nh = MLSTM_HEADS_PER_STEP
    nc = seq // L
    col = pl.BlockSpec((seq, nh * dh), lambda b, h: (b, h))
    return pl.pallas_call(
        functools.partial(_mlstm_kernel, L=L, seq=seq, dh=dh, nh=nh),
        grid=(batch, heads // nh),
        in_specs=[pl.BlockSpec((1, nh, 2, 1, 1), lambda b, h: (h, 0, 0, 0, 0)),
                  pl.BlockSpec((1, 1, nh, 2, nc, L), lambda b, h: (b, h, 0, 0, 0, 0)),
                  col, col, col, col],
        out_specs=col,
        out_shape=jax.ShapeDtypeStruct((t, heads * dh), BF16),
        scratch_shapes=[pltpu.VMEM((nh, dh, 2 * dh), F32), pltpu.VMEM((nh, nc, L), F32),
                        pltpu.VMEM((nh, nc, L), F32)],
        compiler_params=_cparams("parallel", "parallel"),
        name="mlstm",
    )(gate_bias, gate_rows, q, k, v, o)


def _out_even_kernel(h_ref, att_ref, ys4_ref, wg_ref, bg_ref, wo_ref, g_ref, b_ref, o_ref, ys_ref, *, alpha):
    L = S5_CHUNK
    nq = ys4_ref.shape[0]
    nchunk = ys_ref.shape[1] // L
    for q in range(nq):
        for t in range(L):
            ys_ref[q, pl.ds(t, nchunk, stride=L), :] = ys4_ref[q, :, t * LANES:(t + 1) * LANES]
    ys = jnp.concatenate([ys_ref[q] for q in range(nq)], axis=1)
    half = att_ref.shape[1]
    gate = jax.nn.sigmoid(jnp.dot(ys.astype(BF16), wg_ref[...], preferred_element_type=F32) + bg_ref[...])
    mix = jnp.dot(att_ref[...], wo_ref[:half, :], preferred_element_type=F32)
    mix = mix + jnp.dot((ys * gate).astype(BF16), wo_ref[half:, :], preferred_element_type=F32)
    o_ref[...] = _layer_norm(alpha * h_ref[...] + mix, g_ref[...], b_ref[...])


def _out_odd_kernel(h_ref, hm_ref, wo_ref, g_ref, b_ref, o_ref, *, alpha):
    mix = jnp.dot(hm_ref[...], wo_ref[...], preferred_element_type=F32)
    o_ref[...] = _layer_norm(alpha * h_ref[...] + mix, g_ref[...], b_ref[...])


def _row_tiled_call(kern, row_args, fixed_args, out_dtype, name):
    t = row_args[0].shape[0]
    tm = TM_PROJ
    in_specs = [pl.BlockSpec((tm, a.shape[1]), lambda i: (i, 0)) for a in row_args]
    in_specs += [pl.BlockSpec(a.shape, lambda i: (0, 0)) for a in fixed_args]
    d = row_args[0].shape[1]
    return pl.pallas_call(
        kern,
        grid=(t // tm,),
        in_specs=in_specs,
        out_specs=pl.BlockSpec((tm, d), lambda i: (i, 0)),
        out_shape=jax.ShapeDtypeStruct((t, d), out_dtype),
        compiler_params=_cparams("parallel"),
        name=name,
    )(*row_args, *fixed_args)


def _router_kernel(h_ref, w_ref, b_ref, o_ref, cnt_ref, run_ref, w2_ref, *, n_groups, epg):
    @pl.when(pl.program_id(0) == 0)
    def _():
        run_ref[...] = jnp.zeros_like(run_ref)
        w = w_ref[...]
        w_hi = w.astype(BF16)
        w2_ref[:, :LANES] = w_hi
        w2_ref[:, LANES:] = (w - w_hi.astype(F32)).astype(BF16)

    h = h_ref[...]
    h_hi = h.astype(BF16)
    h_lo = (h - h_hi.astype(F32)).astype(BF16)
    part = jnp.dot(h_hi, w2_ref[...], preferred_element_type=F32)
    logits = (part[:, :LANES] + (part[:, LANES:] + jnp.dot(h_lo, w2_ref[:, :LANES], preferred_element_type=F32))
              + b_ref[...])
    lane = _iota(logits.shape, 1)
    big = jnp.int32(LANES)
    lg = jnp.where(lane < n_groups, logits, NEG_INF)
    mg = jnp.max(lg, axis=1, keepdims=True)
    g_val = 1.0 / jnp.sum(jnp.exp(lg - mg), axis=1, keepdims=True)
    g_idx = jnp.min(jnp.where(lg == mg, lane, big), axis=1, keepdims=True)
    lo_lane = n_groups + g_idx * epg
    le = jnp.where((lane >= lo_lane) & (lane < lo_lane + epg), logits, NEG_INF)
    m1 = jnp.max(le, axis=1, keepdims=True)
    i1 = jnp.min(jnp.where(le == m1, lane, big), axis=1, keepdims=True)
    le2 = jnp.where(lane == i1, NEG_INF, le)
    m2 = jnp.max(le2, axis=1, keepdims=True)
    i2 = jnp.min(jnp.where(le2 == m2, lane, big), axis=1, keepdims=True)
    r = jnp.exp(m2 - m1)
    w1 = g_val / (1.0 + r)
    w2 = g_val * r / (1.0 + r)
    e1 = i1 - lo_lane
    e2 = i2 - lo_lane
    first_lo = e1 < e2
    lo = jnp.where(first_lo, e1, e2)
    hi = jnp.where(first_lo, e2, e1)
    w_lo = jnp.where(first_lo, w1, w2)
    w_hi = jnp.where(first_lo, w2, w1)
    pair = (lo * (2 * epg - 1 - lo)) // 2 + (hi - lo - 1)
    cls = g_idx * (epg * (epg - 1) // 2) + pair
    tm = h.shape[0]
    onehot = lane == cls
    earlier = (_iota((tm, tm), 1) < _iota((tm, tm), 0)).astype(BF16)
    before = jnp.dot(earlier, onehot.astype(BF16), preferred_element_type=F32) + run_ref[...]
    rank = jnp.sum(jnp.where(onehot, before, 0.0), axis=1, keepdims=True)
    run = run_ref[...] + jnp.sum(onehot.astype(F32), axis=0, keepdims=True)
    run_ref[...] = run
    cnt_ref[...] = run
    out = jnp.where(lane == 0, cls.astype(F32),
                    jnp.where(lane == 1, w_lo, jnp.where(lane == 2, w_hi, jnp.where(lane == 3, rank, 0.0))))
    o_ref[...] = out


def _router(h, w_r, b_r, n_groups, epg):
    t, d = h.shape
    tm = TM_PROJ
    return pl.pallas_call(
        functools.partial(_router_kernel, n_groups=n_groups, epg=epg),
        grid=(t // tm,),
        in_specs=[pl.BlockSpec((tm, d), lambda i: (i, 0)),
                  pl.BlockSpec(w_r.shape, lambda i: (0, 0)),
                  pl.BlockSpec(b_r.shape, lambda i: (0, 0))],
        out_specs=[pl.BlockSpec((tm, LANES), lambda i: (i, 0)), pl.BlockSpec((1, LANES), lambda i: (0, 0))],
        out_shape=[jax.ShapeDtypeStruct((t, LANES), F32), jax.ShapeDtypeStruct((1, LANES), F32)],
        scratch_shapes=[pltpu.VMEM((1, LANES), F32), pltpu.VMEM((d, 2 * LANES), BF16)],
        compiler_params=_cparams("arbitrary"),
        name="router",
    )(h, w_r, b_r)


def _rows_wait(buf, sem):
    pltpu.make_async_copy(buf, buf, sem).wait()


def _dispatch_kernel(pos_ref, cnt_ref, pend_ref, x_ref, r_ref, o_hbm, xbuf, sem, zsem, *, tm, nsteps, ztile):
    i = pl.program_id(0)
    slot = i % 2
    d = x_ref.shape[1]

    @pl.when(i == 0)
    def _():
        zeros = xbuf.at[1, pl.ds(0, ztile)]
        xbuf[1] = jnp.zeros(xbuf.shape[1:], F32)
        ncls = cnt_ref.shape[0]
        used = pend_ref[ncls - 1]
        firsts = [(cnt_ref[c] > 0, pend_ref[c] - ztile) for c in range(ncls)]
        firsts += [(used + k * ztile < o_hbm.shape[0], used + k * ztile) for k in range(ncls)]
        for cond, first in firsts:
            @pl.when(cond)
            def _():
                pltpu.make_async_copy(zeros, o_hbm.at[pl.ds(pl.multiple_of(first, ztile), ztile)], zsem).start()
        for cond, _ in firsts:
            @pl.when(cond)
            def _():
                pltpu.make_async_copy(zeros, o_hbm.at[pl.ds(0, ztile)], zsem).wait()

    @pl.when(i >= 2)
    def _():
        _rows_wait(xbuf.at[slot], sem.at[slot])

    xbuf[slot, :, :d] = x_ref[...]
    xbuf[slot, :, d:] = r_ref[...]

    def body(r, c):
        pltpu.make_async_copy(xbuf.at[slot, pl.ds(r, 1)], o_hbm.at[pl.ds(pos_ref[0, 0, r], 1)], sem.at[slot]).start()
        return c
    lax.fori_loop(0, tm, body, 0, unroll=8)

    @pl.when(i == nsteps - 1)
    def _():
        _rows_wait(xbuf.at[slot], sem.at[slot])
        if nsteps > 1:
            _rows_wait(xbuf.at[1 - slot], sem.at[1 - slot])


def _dispatch(h, route, pos3, counts, pend, p_rows):
    t, d = h.shape
    w = d + route.shape[1]
    tm = pos3.shape[2]
    assert TM_MOE <= tm
    nsteps = t // tm
    smem = pl.BlockSpec(memory_space=pltpu.SMEM)
    return pl.pallas_call(
        functools.partial(_dispatch_kernel, tm=tm, nsteps=nsteps, ztile=TM_MOE),
        grid=(nsteps,),
        in_specs=[pl.BlockSpec((1, 1, tm), lambda i: (i, 0, 0), memory_space=pltpu.SMEM), smem, smem,
                  pl.BlockSpec((tm, d), lambda i: (i, 0)), pl.BlockSpec((tm, route.shape[1]), lambda i: (i, 0))],
        out_specs=pl.BlockSpec(memory_space=pl.ANY),
        out_shape=jax.ShapeDtypeStruct((p_rows, w), F32),
        scratch_shapes=[pltpu.VMEM((2, tm, w), F32), pltpu.SemaphoreType.DMA((2,)), pltpu.SemaphoreType.DMA(())],
        compiler_params=_cparams("arbitrary"),
        name="moe_dispatch",
    )(pos3, counts, pend, h, route)


def _combine_kernel(pos_ref, nxt_ref, h_ref, g_ref, b_ref, y_hbm, o_ref, ybuf, sem, *, alpha, tm, nsteps):
    i = pl.program_id(0)
    slot = i % 2

    def gather(p_ref, s):
        def body(r, c):
            pltpu.make_async_copy(y_hbm.at[pl.ds(p_ref[0, 0, r], 1)], ybuf.at[s, pl.ds(r, 1)], sem.at[s]).start()
            return c
        lax.fori_loop(0, tm, body, 0, unroll=8)

    @pl.when(i == 0)
    def _():
        gather(pos_ref, 0)

    @pl.when(i + 1 < nsteps)
    def _():
        gather(nxt_ref, 1 - slot)

    _rows_wait(ybuf.at[slot], sem.at[slot])
    o_ref[...] = _layer_norm(alpha * h_ref[...] + ybuf[slot], g_ref[...], b_ref[...])


def _combine(h, pos3, y_sorted, ln_g, ln_b, alpha):
    t, d = h.shape
    tm = pos3.shape[2]
    nsteps = t // tm
    row = lambda i: (i, 0)
    fixed = lambda i: (0, 0)
    return pl.pallas_call(
        functools.partial(_combine_kernel, alpha=alpha, tm=tm, nsteps=nsteps),
        grid=(nsteps,),
        in_specs=[pl.BlockSpec((1, 1, tm), lambda i: (i, 0, 0), memory_space=pltpu.SMEM),
                  pl.BlockSpec((1, 1, tm), lambda i: (jnp.minimum(i + 1, nsteps - 1), 0, 0), memory_space=pltpu.SMEM),
                  pl.BlockSpec((tm, d), row),
                  pl.BlockSpec((1, d), fixed), pl.BlockSpec((1, d), fixed),
                  pl.BlockSpec(memory_space=pl.ANY)],
        out_specs=pl.BlockSpec((tm, d), row),
        out_shape=jax.ShapeDtypeStruct((t, d), F32),
        scratch_shapes=[pltpu.VMEM((2, tm, d), F32), pltpu.SemaphoreType.DMA((2,))],
        compiler_params=_cparams("arbitrary"),
        name="moe_combine_ln",
    )(pos3, pos3, h, ln_g.reshape(1, -1), ln_b.reshape(1, -1), y_sorted)


def _moe_kernel(elo_ref, ehi_ref, nlive_ref, x_ref, g0_ref, u0_ref, d0_ref, g1_ref, u1_ref, d1_ref, y_ref,
                wg_ref, wu_ref, wd_ref):
    i = pl.program_id(0)
    d = y_ref.shape[1]
    live = i < nlive_ref[0]
    prev = jnp.maximum(i - 1, 0)
    experts = ((elo_ref, g0_ref, u0_ref, d0_ref), (ehi_ref, g1_ref, u1_ref, d1_ref))

    for j, (e_ref, g_ref, u_ref, d_ref) in enumerate(experts):
        @pl.when(live & ((i == 0) | (e_ref[i] != e_ref[prev])))
        def _():
            wg_ref[j] = g_ref[0].astype(BF16)
            wu_ref[j] = u_ref[0].astype(BF16)
            wd_ref[j] = d_ref[0].astype(BF16)

    @pl.when(jnp.logical_not(live))
    def _():
        y_ref[...] = jnp.zeros_like(y_ref)

    @pl.when(live)
    def _():
        x = x_ref[:, :d].astype(BF16)
        route = x_ref[:, d:]
        y = None
        for j in range(2):
            gate = jnp.dot(x, wg_ref[j], preferred_element_type=F32)
            up = jnp.dot(x, wu_ref[j], preferred_element_type=F32)
            hid = gate * jax.nn.sigmoid(gate) * up * route[:, j + 1:j + 2]
            part = jnp.dot(hid.astype(BF16), wd_ref[j], preferred_element_type=F32)
            y = part if y is None else y + part
        y_ref[...] = y


def _moe_experts(x_sorted, e_lo, e_hi, n_live, w_gate, w_up, w_down):
    p, xw = x_sorted.shape
    d = w_gate.shape[1]
    tm = TM_MOE
    f = w_gate.shape[2]
    lo_in = pl.BlockSpec((1, d, f), lambda i, lo, hi, nl: (lo[i], 0, 0))
    hi_in = pl.BlockSpec((1, d, f), lambda i, lo, hi, nl: (hi[i], 0, 0))
    lo_dn = pl.BlockSpec((1, f, d), lambda i, lo, hi, nl: (lo[i], 0, 0))
    hi_dn = pl.BlockSpec((1, f, d), lambda i, lo, hi, nl: (hi[i], 0, 0))
    grid_spec = pltpu.PrefetchScalarGridSpec(
        num_scalar_prefetch=3,
        grid=(p // tm,),
        in_specs=[pl.BlockSpec((tm, xw), lambda i, lo, hi, nl: (jnp.minimum(i, nl[0] - 1), 0)),
                  lo_in, lo_in, lo_dn, hi_in, hi_in, hi_dn],
        out_specs=pl.BlockSpec((tm, d), lambda i, lo, hi, nl: (i, 0)),
        scratch_shapes=[pltpu.VMEM((2, d, f), BF16), pltpu.VMEM((2, d, f), BF16), pltpu.VMEM((2, f, d), BF16)],
    )
    return pl.pallas_call(
        _moe_kernel,
        grid_spec=grid_spec,
        out_shape=jax.ShapeDtypeStruct((p, d), F32),
        compiler_params=_cparams("arbitrary"),
        name="moe_experts",
    )(e_lo, e_hi, n_live, x_sorted, w_gate, w_up, w_down, w_gate, w_up, w_down)


def _hier_moe_ln(h, ln_g, ln_b, alpha, w_group, b_group, w_expert, b_expert, w_gate, w_up, w_down):
    t, d = h.shape
    n_groups, _, epg = w_expert.shape
    npairs = epg * (epg - 1) // 2
    ncls = n_groups * npairs
    tm = TM_MOE
    w_r = jnp.concatenate([w_group, w_expert.transpose(1, 0, 2).reshape(d, n_groups * epg)], axis=1)
    b_r = jnp.concatenate([b_group, b_expert.reshape(-1)])
    pad = LANES - w_r.shape[1]
    w_r = jnp.pad(w_r, ((0, 0), (0, pad)))
    b_r = jnp.pad(b_r, (0, pad)).reshape(1, LANES)
    route, cnt = _router(h, w_r, b_r, n_groups, epg)

    cls = route[:, 0].astype(jnp.int32)
    rank = route[:, 3].astype(jnp.int32)
    counts = cnt[0, :ncls].astype(jnp.int32)
    padded = ((counts + tm - 1) // tm) * tm
    pend = jnp.cumsum(padded)
    pstart = pend - padded
    pos = jnp.sum(jnp.where(cls[:, None] == jnp.arange(ncls)[None, :], pstart[None, :], 0), axis=1) + rank
    pos3 = pos.astype(jnp.int32).reshape(t // TM_PROJ, 1, TM_PROJ)
    p_rows = t + ncls * tm
    x_sorted = _dispatch(h, route, pos3, counts, pend.astype(jnp.int32), p_rows)

    tile_start = jnp.arange(p_rows // tm, dtype=jnp.int32) * tm
    tile_start = jnp.minimum(tile_start, pend[-1] - tm)
    tile_cls = jnp.sum(pend[None, :] <= tile_start[:, None], axis=1).astype(jnp.int32)
    n_live = (pend[-1:] // tm).astype(jnp.int32)
    pair = tile_cls % npairs
    pair_lo = sum(jnp.where(pair == k, v, 0) for k, v in enumerate(PAIRS_LO))
    pair_hi = sum(jnp.where(pair == k, v, 0) for k, v in enumerate(PAIRS_HI))
    e_lo = ((tile_cls // npairs) * epg + pair_lo).astype(jnp.int32)
    e_hi = ((tile_cls // npairs) * epg + pair_hi).astype(jnp.int32)

    y_sorted = _moe_experts(x_sorted, e_lo, e_hi, n_live, w_gate, w_up, w_down)
    return _combine(h, pos3, y_sorted, ln_g, ln_b, alpha)


def _even_mixer(h, batch, seq, alpha, ln_g, ln_b, w_in, f_bias, s5_params, w_glu, b_glu, w_out):
    t, d = h.shape
    heads = f_bias.shape[0]
    groups, p_state = s5_params[0].shape
    hc = s5_params[3].shape[-1]
    s5_width = groups * hc
    fox_width = d - s5_width
    dh = fox_width // heads
    q_scale = dh ** -0.5
    w_q, w_k, w_v, w_f, w_u = jnp.split(w_in, [fox_width, 2 * fox_width, 3 * fox_width, 3 * fox_width + heads], axis=1)
    hp = heads // 2
    w_f = jnp.pad(w_f.reshape(d, hp, 2), ((0, 0), (0, 0), (0, LANES - 2))).reshape(d, hp * LANES)
    fb = jnp.pad(f_bias.reshape(hp, 1, 2), ((0, 0), (0, 0), (0, LANES - 2)))
    w_cat = jnp.concatenate([w_q * q_scale, w_k, w_v, w_u, w_f], axis=1)
    qkv, u4, f_cols = _proj_even(h, w_cat.astype(BF16), 3 * fox_width, s5_width, hp * LANES)
    att = _fox_attention(qkv, f_cols, fb, batch, seq, heads, dh)
    ys4 = _s5(u4, _s5_tables(*s5_params), batch, seq)
    tm = TM_PROJ
    row = lambda i: (i, 0)
    fixed = lambda i: (0, 0)
    fixed_args = (w_glu.astype(BF16), b_glu.reshape(1, -1), w_out.astype(BF16), ln_g.reshape(1, -1), ln_b.reshape(1, -1))
    return pl.pallas_call(
        functools.partial(_out_even_kernel, alpha=alpha),
        grid=(t // tm,),
        in_specs=[pl.BlockSpec((tm, d), row), pl.BlockSpec((tm, fox_width), row),
                  pl.BlockSpec((ys4.shape[0], tm // S5_CHUNK, ys4.shape[2]), lambda i: (0, i, 0))]
        + [pl.BlockSpec(a.shape, fixed) for a in fixed_args],
        out_specs=pl.BlockSpec((tm, d), row),
        out_shape=jax.ShapeDtypeStruct((t, d), F32),
        scratch_shapes=[pltpu.VMEM((ys4.shape[0], tm, LANES), F32)],
        compiler_params=_cparams("parallel"),
        name="out_even",
    )(h, att, ys4, *fixed_args)


def _odd_mixer(h, batch, seq, alpha, ln_g, ln_b, w_in, conv_w, conv_b, i_bias, f_bias, w_out):
    t, d = h.shape
    heads = i_bias.shape[0]
    dmix = conv_w.shape[1] // 2
    dh = dmix // heads
    q, k, v, o, gates = _proj_odd(h, w_in.astype(BF16), conv_w, conv_b, seq, dh)
    nh = MLSTM_HEADS_PER_STEP
    g = gates[:, :2 * heads].reshape(batch, seq, 2, heads).transpose(0, 3, 2, 1)
    gate_rows = g.reshape(batch, heads // nh, nh, 2, seq // MLSTM_CHUNK, MLSTM_CHUNK)
    gate_bias = jnp.stack([i_bias, f_bias], axis=1).reshape(heads // nh, nh, 2, 1, 1)
    hm = _mlstm(q, k, v, o, gate_rows, gate_bias, batch, seq, heads, dh)
    kern = functools.partial(_out_odd_kernel, alpha=alpha)
    return _row_tiled_call(kern, (h, hm), (w_out.astype(BF16), ln_g.reshape(1, -1), ln_b.reshape(1, -1)),
                           F32, "out_odd")


def kernel(x, ln_g, ln_b, even_w_in, fox_f_bias, s5_a_re, s5_a_im, s5_log_dt, s5_b_re, s5_b_im, s5_c_re, s5_c_im, s5_d, s5_w_glu, s5_b_glu, even_w_out, odd_w_in, mlstm_conv_w, mlstm_conv_b, mlstm_i_bias, mlstm_f_bias, odd_w_out, moe_w_group, moe_b_group, moe_w_expert, moe_b_expert, moe_w_gate, moe_w_up, moe_w_down):
    batch, seq, d = x.shape
    depth = ln_g.shape[0]
    alpha = (2 * depth) ** 0.25
    h = x.reshape(batch * seq, d)
    for layer in range(depth):
        j = layer // 2
        if layer % 2 == 0:
            s5_params = (s5_a_re[j], s5_a_im[j], s5_log_dt[j], s5_b_re[j], s5_b_im[j],
                         s5_c_re[j], s5_c_im[j], s5_d[j])
            h = _even_mixer(h, batch, seq, alpha, ln_g[layer, 0], ln_b[layer, 0], even_w_in[j], fox_f_bias[j],
                            s5_params, s5_w_glu[j], s5_b_glu[j], even_w_out[j])
        else:
            h = _odd_mixer(h, batch, seq, alpha, ln_g[layer, 0], ln_b[layer, 0], odd_w_in[j], mlstm_conv_w[j],
                           mlstm_conv_b[j], mlstm_i_bias[j], mlstm_f_bias[j], odd_w_out[j])
        h = _hier_moe_ln(h, ln_g[layer, 1], ln_b[layer, 1], alpha, moe_w_group[layer], moe_b_group[layer],
                         moe_w_expert[layer], moe_b_expert[layer], moe_w_gate[layer], moe_w_up[layer],
                         moe_w_down[layer])
    return h.reshape(batch, seq, d)
```

```python
import functools

import jax
import jax.numpy as jnp
from jax import lax
from jax.experimental import pallas as pl
from jax.experimental.pallas import tpu as pltpu

F32 = jnp.float32
BF16 = jnp.bfloat16
HIGHEST = lax.Precision.HIGHEST
LN_EPS = 1e-5
NEG_INF = float("-inf")

LANES = 128
VMEM_LIMIT = 56 * 1024 * 1024

TM_PROJ = 256
ATT_BLOCK = 256
ATT_QUERY_BLOCK = 512
S5_CHUNK = 16
MLSTM_CHUNK = 256
MLSTM_HEADS_PER_STEP = 4
TM_MOE = 256
PAIRS_LO = (0, 0, 0, 1, 1, 2)
PAIRS_HI = (1, 2, 3, 2, 3, 3)

NT_DIMS = (((1,), (1,)), ((), ()))
TN_DIMS = (((0,), (0,)), ((), ()))


def _cparams(*sem):
    return pltpu.CompilerParams(dimension_semantics=sem, vmem_limit_bytes=VMEM_LIMIT)


def _log_sigmoid(x):
    return jnp.minimum(x, 0.0) - jnp.log1p(jnp.exp(-jnp.abs(x)))


def _layer_norm(x, g, b):
    mu = jnp.mean(x, axis=-1, keepdims=True)
    xc = x - mu
    var = jnp.mean(xc * xc, axis=-1, keepdims=True)
    return xc * lax.rsqrt(var + LN_EPS) * g + b


def _iota(shape, dim):
    return lax.broadcasted_iota(jnp.int32, shape, dim)


def _proj_even_kernel(x_ref, wqkv_ref, wu_ref, wf_ref, qkv_ref, u_ref, f_ref, zs_ref):
    xb = x_ref[...].astype(BF16)
    qkv_ref[...] = jnp.dot(xb, wqkv_ref[...], preferred_element_type=F32).astype(BF16)
    f_ref[...] = jnp.dot(xb, wf_ref[...], preferred_element_type=F32)
    z = jnp.dot(xb, wu_ref[...], preferred_element_type=F32)
    L = S5_CHUNK
    nchunk = x_ref.shape[0] // L
    for q in range(zs_ref.shape[0]):
        zs_ref[q] = z[:, q * LANES:(q + 1) * LANES]
        for s in range(L):
            u_ref[q, :, s * LANES:(s + 1) * LANES] = zs_ref[q, pl.ds(s, nchunk, stride=L), :].astype(BF16)


def _proj_even(x, w_qkv, w_u, w_f):
    t, d = x.shape
    tm = TM_PROJ
    L = S5_CHUNK
    n_qkv, n_u, n_f = w_qkv.shape[1], w_u.shape[1], w_f.shape[1]
    nq = n_u // LANES
    fixed = lambda i: (0, 0)
    return pl.pallas_call(
        _proj_even_kernel,
        grid=(t // tm,),
        in_specs=[pl.BlockSpec((tm, d), lambda i: (i, 0)), pl.BlockSpec(w_qkv.shape, fixed),
                  pl.BlockSpec(w_u.shape, fixed), pl.BlockSpec(w_f.shape, fixed)],
        out_specs=[pl.BlockSpec((tm, n_qkv), lambda i: (i, 0)),
                   pl.BlockSpec((nq, tm // L, L * LANES), lambda i: (0, i, 0)),
                   pl.BlockSpec((tm, n_f), lambda i: (i, 0))],
        out_shape=[jax.ShapeDtypeStruct((t, n_qkv), BF16),
                   jax.ShapeDtypeStruct((nq, t // L, L * LANES), BF16),
                   jax.ShapeDtypeStruct((t, n_f), F32)],
        scratch_shapes=[pltpu.VMEM((nq, tm, LANES), F32)],
        compiler_params=_cparams("parallel"),
        name="proj_even",
    )(x, w_qkv, w_u, w_f)


def _proj_odd_kernel(x_ref, w_ref, cw_ref, cb_ref, q_ref, k_ref, v_ref, o_ref, g_ref, *zs_refs,
                     tm, dmix, k_scale, tiles_per_seq, conv_width):
    i = pl.program_id(0)
    xb = x_ref[...].astype(BF16)
    cw = zs_refs[0].shape[1]

    @pl.when(i % tiles_per_seq == 0)
    def _():
        for zs_ref in zs_refs:
            zs_ref[0:8, :] = jnp.zeros((8, cw), F32)

    gates = jnp.dot(xb, w_ref[:, 4 * dmix:], preferred_element_type=F32)
    g_ref[...] = jnp.concatenate([gates, jnp.zeros((tm, LANES - gates.shape[1]), F32)], axis=1)
    for zs_ref, c0 in zip(zs_refs, range(0, 2 * dmix, cw)):
        zs_ref[8:tm + 8, :] = jnp.dot(xb, w_ref[:, c0:c0 + cw], preferred_element_type=F32)
        vo = jnp.dot(xb, w_ref[:, 2 * dmix + c0:2 * dmix + c0 + cw], preferred_element_type=F32)
        if c0 < dmix:
            v_ref[:, c0:c0 + cw] = vo.astype(BF16)
        else:
            o_ref[:, c0 - dmix:c0 - dmix + cw] = vo
        cols = slice(c0, c0 + cw)
        acc = jnp.broadcast_to(cb_ref[:, cols], (tm, cw))
        for j in range(conv_width):
            acc = acc + cw_ref[j:j + 1, cols] * zs_ref[pl.ds(8 - (conv_width - 1) + j, tm), :]
        y = acc * jax.nn.sigmoid(acc)
        if c0 < dmix:
            q_ref[:, cols] = y.astype(BF16)
        else:
            k_ref[:, c0 - dmix:c0 - dmix + cw] = (y * k_scale).astype(BF16)
        zs_ref[0:8, :] = zs_ref[tm:tm + 8, :]


def _proj_odd(x, w, conv_w, conv_b, seq, head_dim):
    t, d = x.shape
    dmix = conv_w.shape[1] // 2
    tm = TM_PROJ
    kern = functools.partial(_proj_odd_kernel, tm=tm, dmix=dmix, k_scale=head_dim ** -0.5,
                             tiles_per_seq=seq // tm, conv_width=conv_w.shape[0])
    row = lambda i: (i, 0)
    fixed = lambda i: (0, 0)
    return pl.pallas_call(
        kern,
        grid=(t // tm,),
        in_specs=[pl.BlockSpec((tm, d), row), pl.BlockSpec(w.shape, fixed),
                  pl.BlockSpec(conv_w.shape, fixed), pl.BlockSpec((1, 2 * dmix), fixed)],
        out_specs=[pl.BlockSpec((tm, dmix), row)] * 4 + [pl.BlockSpec((tm, LANES), row)],
        out_shape=[jax.ShapeDtypeStruct((t, dmix), BF16)] * 3
        + [jax.ShapeDtypeStruct((t, dmix), F32), jax.ShapeDtypeStruct((t, LANES), F32)],
        scratch_shapes=[pltpu.VMEM((tm + 8, 512), F32)] * (2 * dmix // 512),
        compiler_params=_cparams("arbitrary"),
        name="proj_odd",
    )(x, w, conv_w, conv_b.reshape(1, -1))


def _split3(x):
    hi = x.astype(BF16).astype(F32)
    r = x - hi
    mid = r.astype(BF16).astype(F32)
    lo = (r - mid).astype(BF16).astype(F32)
    return hi, mid, lo


def _fox_kernel(fb_ref, f_ref, q_ref, k_ref, v_ref, o_ref, c_ref, kaug_ref, vt_ref, acc_ref, *, bq, blk, seq, dh):
    qi = pl.program_id(2)
    lane = _iota((blk, LANES), 1)
    head_lanes = (lane < dh, lane >= dh)

    @pl.when(qi == 0)
    def _():
        tril = (_iota((blk, blk), 1) <= _iota((blk, blk), 0)).astype(BF16)
        eye_b = (_iota((LANES, LANES), 0) == _iota((LANES, LANES), 1)).astype(BF16)

        def prep(jb, carry):
            rows = pl.ds(pl.multiple_of(jb * blk, blk), blk)
            ls = _log_sigmoid(f_ref[rows, :] + fb_ref[0])
            c3 = jnp.dot(tril, jnp.concatenate(_split3(ls), axis=1).astype(BF16), preferred_element_type=F32)
            c = c3[:, :LANES] + c3[:, LANES:2 * LANES] + c3[:, 2 * LANES:] + carry
            c_ref[rows, :] = c
            kblk = k_ref[rows, :]
            for j in range(2):
                hi, mid, lo = _split3(c[:, j:j + 1])
                aug = jnp.where(lane < 3, 1.0, jnp.where(lane == 3, -hi, jnp.where(
                    lane == 4, -mid, jnp.where(lane == 5, -lo, 0.0))))
                kaug_ref[j, rows, :] = jnp.concatenate(
                    [jnp.where(head_lanes[j], kblk, jnp.zeros_like(kblk)), aug.astype(BF16)], axis=1)
            vt_ref[jb] = lax.dot_general(eye_b, v_ref[rows, :], NT_DIMS,
                                         preferred_element_type=F32).astype(BF16)
            return c[blk - 1:blk, :]

        lax.fori_loop(0, seq // blk, prep, jnp.zeros((1, LANES), F32))

    q = q_ref[...]
    qlane = _iota((bq, LANES), 1)
    c_q = c_ref[pl.ds(pl.multiple_of(qi * bq, bq), bq), :]
    q_aug = []
    for j in range(2):
        hi, mid, lo = _split3(c_q[:, j:j + 1])
        aug = jnp.where(qlane == 0, hi, jnp.where(qlane == 1, mid, jnp.where(
            qlane == 2, lo, jnp.where(qlane < 6, 1.0, 0.0))))
        q_head = jnp.where((qlane < dh) if j == 0 else (qlane >= dh), q, jnp.zeros_like(q))
        q_aug.append(jnp.concatenate([q_head, aug.astype(BF16)], axis=1))
    acc_ref[...] = jnp.zeros_like(acc_ref)
    r = bq // blk

    def score(kb):
        krows = pl.ds(pl.multiple_of(kb * blk, blk), blk)
        return tuple(lax.dot_general(kaug_ref[j, krows, :], q_aug[j], NT_DIMS, preferred_element_type=F32)
                     for j in range(2))

    def accumulate(kb, probs, alphas):
        vt = vt_ref[kb]
        for j in range(2):
            acc_ref[j] = alphas[j] * acc_ref[j] + jnp.dot(vt[j * dh:(j + 1) * dh, :], probs[j],
                                                          preferred_element_type=F32)

    def softmax(scores, stats, diag):
        out, probs, alphas = [], [], []
        for j in range(2):
            m, l = stats[2 * j], stats[2 * j + 1]
            s = scores[j]
            if diag is not None:
                s = jnp.where(_iota((blk, bq), 0) + diag * blk <= _iota((blk, bq), 1), s, NEG_INF)
            m_new = jnp.maximum(m, jnp.max(s, axis=0, keepdims=True))
            alpha = jnp.exp(m - m_new)
            p = jnp.exp(s - m_new)
            out += [m_new, alpha * l + jnp.sum(p, axis=0, keepdims=True)]
            probs.append(p.astype(BF16))
            alphas.append(alpha)
        return tuple(out), tuple(probs), tuple(alphas)

    m0 = jnp.full((1, bq), NEG_INF, F32)
    l0 = jnp.zeros((1, bq), F32)

    def step(kb, state):
        scores, stats = state
        nxt = score(kb + 1)
        stats, probs, alphas = softmax(scores, stats, None)
        accumulate(kb, probs, alphas)
        return nxt, stats

    first = r * qi
    scores, stats = lax.fori_loop(0, first, step, (score(0), (m0, l0, m0, l0)))
    for diag in range(r):
        nxt = score(first + diag + 1) if diag + 1 < r else None
        stats, probs, alphas = softmax(scores, stats, diag)
        accumulate(first + diag, probs, alphas)
        scores = nxt
    out_t = jnp.concatenate([acc_ref[0] / stats[1], acc_ref[1] / stats[3]], axis=0).astype(BF16)
    eye_q = (_iota((blk, blk), 0) == _iota((blk, blk), 1)).astype(BF16)
    for c0 in range(0, bq, blk):
        o_ref[c0:c0 + blk, :] = lax.dot_general(eye_q, out_t[:, c0:c0 + blk], NT_DIMS,
                                                preferred_element_type=F32).astype(o_ref.dtype)


def _fox_attention(qkv, f_cols, f_bias, batch, seq, heads, dh):
    t = qkv.shape[0]
    blk = ATT_BLOCK
    bq = min(ATT_QUERY_BLOCK, seq)
    nq = seq // bq
    hp = heads * dh // LANES
    kern = functools.partial(_fox_kernel, bq=bq, blk=blk, seq=seq, dh=dh)
    return pl.pallas_call(
        kern,
        grid=(batch, hp, nq),
        in_specs=[pl.BlockSpec((1, 1, LANES), lambda b, p, i: (p, 0, 0)),
                  pl.BlockSpec((seq, LANES), lambda b, p, i: (b, p)),
                  pl.BlockSpec((bq, LANES), lambda b, p, i: (b * nq + i, p)),
                  pl.BlockSpec((seq, LANES), lambda b, p, i: (b, hp + p)),
                  pl.BlockSpec((seq, LANES), lambda b, p, i: (b, 2 * hp + p))],
        out_specs=pl.BlockSpec((bq, LANES), lambda b, p, i: (b * nq + i, p)),
        out_shape=jax.ShapeDtypeStruct((t, heads * dh), BF16),
        scratch_shapes=[pltpu.VMEM((seq, LANES), F32), pltpu.VMEM((2, seq, 2 * LANES), BF16),
                        pltpu.VMEM((seq // blk, LANES, blk), BF16), pltpu.VMEM((2, dh, bq), F32)],
        compiler_params=_cparams("parallel", "parallel", "arbitrary"),
        name="fox_attention",
    )(f_bias, f_cols, qkv, qkv, qkv)


def _s5_tables(a_re, a_im, log_dt, b_re, b_im, c_re, c_im, d_skip):
    L = S5_CHUNK
    g, p = a_re.shape
    hc = b_re.shape[-1]
    dt = jnp.exp(log_dt)[:, None]
    mag = jnp.exp(a_re * dt)
    lb_re = mag * jnp.cos(a_im * dt)
    lb_im = mag * jnp.sin(a_im * dt)
    num_re = lb_re - 1.0
    num_im = lb_im
    den = a_re * a_re + a_im * a_im
    z_re = (num_re * a_re + num_im * a_im) / den
    z_im = (num_im * a_re - num_re * a_im) / den
    bb_re = z_re[..., None] * b_re - z_im[..., None] * b_im
    bb_im = z_re[..., None] * b_im + z_im[..., None] * b_re
    tau = jnp.arange(L + 1, dtype=F32)
    pmag = jnp.exp((a_re * dt)[..., None] * tau)
    pw_re = pmag * jnp.cos((a_im * dt)[..., None] * tau)
    pw_im = pmag * jnp.sin((a_im * dt)[..., None] * tau)
    cp_re = c_re[..., None] * pw_re[:, None] - c_im[..., None] * pw_im[:, None]
    cp_im = c_re[..., None] * pw_im[:, None] + c_im[..., None] * pw_re[:, None]
    kern = (jnp.einsum("gopt,gpi->gtoi", cp_re[..., :L], bb_re, precision=HIGHEST)
            - jnp.einsum("gopt,gpi->gtoi", cp_im[..., :L], bb_im, precision=HIGHEST))
    kern = kern.at[:, 0].add(d_skip[:, :, None] * jnp.eye(hc, dtype=F32))
    rev = (L - 1) - jnp.arange(L)
    e_re = pw_re[:, :, rev][..., None] * bb_re[:, :, None] - pw_im[:, :, rev][..., None] * bb_im[:, :, None]
    e_im = pw_re[:, :, rev][..., None] * bb_im[:, :, None] + pw_im[:, :, rev][..., None] * bb_re[:, :, None]
    o_re = cp_re[..., 1:]
    o_im = -cp_im[..., 1:]

    gq = LANES // hc
    nq = g // gq

    def tile(m, perm):
        m = m.astype(BF16).reshape(nq, gq, *m.shape[1:])
        return jnp.moveaxis(m, 1, perm)

    def embed(m2, spread, row_group, col_group):
        full = jnp.dot(m2, spread.astype(BF16), preferred_element_type=F32)
        keep = row_group(jnp.arange(full.shape[0]))[:, None] == col_group(jnp.arange(full.shape[1]))[None, :]
        return jnp.where(keep, full, 0.0).astype(BF16)

    eye_c = jnp.tile(jnp.eye(hc, dtype=F32), (1, gq))
    eye_p = jnp.tile(jnp.eye(p, dtype=F32), (1, gq))
    chan_group = lambda r: (r // hc) % gq
    k_t = tile(kern.transpose(0, 1, 3, 2), 2).reshape(nq * L * LANES, hc)
    lag_blocks = embed(k_t, eye_c, chan_group, chan_group).reshape(nq, L, LANES, LANES)
    w_end = jnp.concatenate(
        [embed(tile(e.transpose(0, 2, 3, 1), 2).reshape(nq * L * LANES, p), eye_p, chan_group,
               lambda c: c // p).reshape(nq, L * LANES, gq * p) for e in (e_re, e_im)], axis=2)
    w_out = jnp.concatenate(
        [embed(tile(o.transpose(0, 2, 3, 1), 1).reshape(nq * gq * p, L * hc), jnp.kron(jnp.eye(L, dtype=F32), eye_c),
               lambda r: (r // p) % gq, chan_group).reshape(nq, gq * p, L * LANES) for o in (o_re, o_im)], axis=1)
    lam_re = pw_re[..., L].reshape(1, g * p)
    lam_im = pw_im[..., L].reshape(1, g * p)
    return lag_blocks, w_end, w_out, lam_re, lam_im


def _s5_end_kernel(u_ref, w_ref, ere_ref, eim_ref):
    e = jnp.dot(u_ref[0], w_ref[0], preferred_element_type=F32)
    ns = ere_ref.shape[0]
    for k in range(ns):
        ere_ref[k] = e[:, k * LANES:(k + 1) * LANES]
        eim_ref[k] = e[:, (ns + k) * LANES:(ns + k + 1) * LANES]


def _s5_scan_kernel(lre_ref, lim_ref, ere_ref, eim_ref, hre_ref, him_ref, *, batch, nchunk):
    ns = ere_ref.shape[0]
    lr = [lre_ref[k] for k in range(ns)]
    li = [lim_ref[k] for k in range(ns)]

    def step(j, carry):
        sl = pl.ds(j, batch, stride=nchunk)
        out = []
        for k in range(ns):
            hr, hi = carry[2 * k], carry[2 * k + 1]
            hre_ref[k, sl, :] = hr
            him_ref[k, sl, :] = hi
            out += [lr[k] * hr - li[k] * hi + ere_ref[k, sl, :], lr[k] * hi + li[k] * hr + eim_ref[k, sl, :]]
        return tuple(out)

    z = jnp.zeros((batch, LANES), F32)
    lax.fori_loop(0, nchunk, step, (z,) * (2 * ns))


def _s5_out_kernel(u_ref, lag_ref, hre_ref, him_ref, w_ref, y_ref, t_ref):
    L = lag_ref.shape[1]

    @pl.when(pl.program_id(1) == 0)
    def _():
        for s in range(L):
            for t in range(s, L):
                t_ref[s * LANES:(s + 1) * LANES, t * LANES:(t + 1) * LANES] = lag_ref[0, t - s]
            if s % 2 == 1:
                t_ref[s * LANES:(s + 1) * LANES, (s - 1) * LANES:s * LANES] = jnp.zeros((LANES, LANES), BF16)

    half = w_ref.shape[1] // 2
    ns = hre_ref.shape[0]
    h_re = jnp.concatenate([hre_ref[k] for k in range(ns)], axis=1).astype(BF16)
    h_im = jnp.concatenate([him_ref[k] for k in range(ns)], axis=1).astype(BF16)
    inter = jnp.dot(h_re, w_ref[0, :half, :], preferred_element_type=F32)
    inter = inter + jnp.dot(h_im, w_ref[0, half:, :], preferred_element_type=F32)
    ct = 2 * LANES
    for c0 in range(0, t_ref.shape[1], ct):
        k_hi = c0 + ct
        y = jnp.dot(u_ref[0, :, :k_hi], t_ref[:k_hi, c0:c0 + ct], preferred_element_type=F32)
        y_ref[0, :, c0:c0 + ct] = jax.nn.gelu(y + inter[:, c0:c0 + ct])


def _s5(u4, tables, batch, seq):
    lag_blocks, w_end, w_out, lam_re, lam_im = tables
    L = S5_CHUNK
    nq, rows, kc = u4.shape
    nchunk = seq // L
    sw = w_end.shape[2] // 2
    ns = sw // LANES
    rb = rows // 2
    e_re, e_im = pl.pallas_call(
        _s5_end_kernel,
        grid=(nq,),
        in_specs=[pl.BlockSpec((1, rows, kc), lambda q: (q, 0, 0)),
                  pl.BlockSpec((1, kc, 2 * sw), lambda q: (q, 0, 0))],
        out_specs=[pl.BlockSpec((ns, rows, LANES), lambda q: (q, 0, 0))] * 2,
        out_shape=[jax.ShapeDtypeStruct((nq * ns, rows, LANES), F32)] * 2,
        compiler_params=_cparams("parallel"),
        name="s5_chunk_end",
    )(u4, w_end)
    lam_spec = pl.BlockSpec((ns, 1, LANES), lambda q: (q, 0, 0))
    st_spec = pl.BlockSpec((ns, rows, LANES), lambda q: (q, 0, 0))
    h_re, h_im = pl.pallas_call(
        functools.partial(_s5_scan_kernel, batch=batch, nchunk=nchunk),
        grid=(nq,),
        in_specs=[lam_spec, lam_spec, st_spec, st_spec],
        out_specs=[st_spec] * 2,
        out_shape=[jax.ShapeDtypeStruct(e_re.shape, F32)] * 2,
        compiler_params=_cparams("parallel"),
        name="s5_chunk_scan",
    )(lam_re.reshape(nq * ns, 1, LANES), lam_im.reshape(nq * ns, 1, LANES), e_re, e_im)
    hs_spec = pl.BlockSpec((ns, rb, LANES), lambda q, r: (q, r, 0))
    return pl.pallas_call(
        _s5_out_kernel,
        grid=(nq, rows // rb),
        in_specs=[pl.BlockSpec((1, rb, kc), lambda q, r: (q, r, 0)),
                  pl.BlockSpec((1, L, LANES, LANES), lambda q, r: (q, 0, 0, 0)),
                  hs_spec, hs_spec,
                  pl.BlockSpec((1, 2 * sw, kc), lambda q, r: (q, 0, 0))],
        out_specs=pl.BlockSpec((1, rb, kc), lambda q, r: (q, r, 0)),
        out_shape=jax.ShapeDtypeStruct((nq, rows, kc), F32),
        scratch_shapes=[pltpu.VMEM((kc, kc), BF16)],
        compiler_params=_cparams("parallel", "arbitrary"),
        name="s5_out",
    )(u4, lag_blocks, h_re, h_im, w_out)


def _rows_to_cols(eye3, sub, a_row, b_row):
    a3 = jnp.concatenate(_split3(a_row), axis=1)
    b3 = jnp.concatenate(_split3(b_row), axis=1)
    rows = jnp.where(sub < LANES, a3, b3).astype(BF16)
    cols = lax.dot_general(eye3, rows, NT_DIMS, preferred_element_type=F32)
    return cols[:, :LANES], cols[:, LANES:]


def _mlstm_kernel(gb_ref, g_ref, q_ref, k_ref, v_ref, o_ref, h_ref, c_ref, b_scr, i_scr, *, L, seq, dh, nh):
    tri = (_iota((L, L), 0) <= _iota((L, L), 1)).astype(F32)
    eye = jnp.concatenate([(_iota((L, L), 0) == _iota((L, L), 1)).astype(BF16)] * 3, axis=1)
    causal = _iota((L, L), 1) <= _iota((L, L), 0)
    lane = _iota((L, LANES), 1)
    one_col = (lane == 0).astype(BF16)
    sub = _iota((2 * LANES, 3 * L), 0)
    c_ref[...] = jnp.zeros_like(c_ref)
    for hh in range(nh):
        log_f = _log_sigmoid(g_ref[0, 0, hh, 1] + gb_ref[0, hh, 1])
        b_scr[hh] = jnp.dot(log_f, tri, precision=HIGHEST, preferred_element_type=F32)
        i_scr[hh] = g_ref[0, 0, hh, 0] + gb_ref[0, hh, 0]

    def chunk(c, carry):
        st = pl.multiple_of(c * L, L)
        heads = range(nh)
        cols_h = [slice(hh * dh, (hh + 1) * dh) for hh in heads]
        b_row = [b_scr[hh, pl.ds(c, 1), :] for hh in heads]
        li_row = [i_scr[hh, pl.ds(c, 1), :] for hh in heads]
        cols = [_rows_to_cols(eye, sub, b_row[hh], li_row[hh]) for hh in heads]
        q = [q_ref[pl.ds(st, L), cols_h[hh]] for hh in heads]
        k = [k_ref[pl.ds(st, L), cols_h[hh]] for hh in heads]
        v = [v_ref[pl.ds(st, L), cols_h[hh]] for hh in heads]
        qk = [lax.dot_general(q[hh], k[hh], NT_DIMS, preferred_element_type=F32) for hh in heads]
        qc = [jnp.dot(q[hh], c_ref[hh].astype(BF16), preferred_element_type=F32) for hh in heads]

        s, m_t, m_inter, m_new, w_col, decay = [], [], [], [], [], []
        for hh in heads:
            m_prev = carry[hh]
            b_full, li_full = cols[hh]
            b_col = b_full[:, 0:1]
            b_last = b_row[hh][:, L - 1:L]
            log_d = jnp.where(causal, jnp.tile(b_full, (1, L // LANES)) - b_row[hh] + li_row[hh], NEG_INF)
            m_inter.append(b_col + m_prev)
            m_t.append(jnp.maximum(m_inter[hh], jnp.max(log_d, axis=1, keepdims=True)))
            s.append((qk[hh] * jnp.exp(log_d - m_t[hh])).astype(BF16))
            g_row = b_last - b_row[hh] + li_row[hh]
            m_new.append(jnp.maximum(b_last + m_prev, jnp.max(g_row, axis=1, keepdims=True)))
            w_col.append(jnp.exp(b_last - b_full + li_full - m_new[hh]))
            decay.append(jnp.exp(b_last + m_prev - m_new[hh]))

        tot = [jnp.dot(s[hh], jnp.concatenate([v[hh], one_col], axis=1), preferred_element_type=F32)
               for hh in heads]
        upd = [lax.dot_general(k[hh], jnp.concatenate(
            [(v[hh].astype(F32) * w_col[hh]).astype(BF16), jnp.where(lane == 0, w_col[hh], 0.0).astype(BF16)],
            axis=1), TN_DIMS, preferred_element_type=F32) for hh in heads]
        for hh in heads:
            t_h = tot[hh] + jnp.exp(m_inter[hh] - m_t[hh]) * qc[hh]
            den = jnp.maximum(jnp.abs(t_h[:, dh:dh + 1]), jnp.exp(-m_t[hh]))
            h = t_h[:, :dh] / den
            h_ref[pl.ds(st, L), cols_h[hh]] = (h * jax.nn.sigmoid(o_ref[pl.ds(st, L), cols_h[hh]])).astype(h_ref.dtype)
            c_ref[hh] = decay[hh] * c_ref[hh] + upd[hh]
        return tuple(m_new)

    lax.fori_loop(0, seq // L, chunk, (jnp.zeros((1, 1), F32),) * nh)


def _mlstm(q, k, v, o, gate_rows, gate_bias, batch, seq, heads, dh):
    t = q.shape[0]
    L = MLSTM_CHUNK
    nh = MLSTM_HEADS_PER_STEP
    nc = seq // L
    col = pl.BlockSpec((seq, nh * dh), lambda b, h: (b, h))
    return pl.pallas_call(
        functools.partial(_mlstm_kernel, L=L, seq=seq, dh=dh, nh=nh),
        grid=(batch, heads // nh),
        in_specs=[pl.BlockSpec((1, nh, 2, 1, 1), lambda b, h: (h, 0, 0, 0, 0)),
                  pl.BlockSpec((1, 1, nh, 2, nc, L), lambda b, h: (b, h, 0, 0, 0, 0)),
                  col, col, col, col],
        out_specs=col,
        out_shape=jax.ShapeDtypeStruct((t, heads * dh), BF16),
        scratch_shapes=[pltpu.VMEM((nh, dh, 2 * dh), F32), pltpu.VMEM((nh, nc, L), F32),
                        pltpu.VMEM((nh, nc, L), F32)],
        compiler_params=_cparams("parallel", "parallel"),
        name="mlstm",
    )(gate_bias, gate_rows, q, k, v, o)


def _out_even_kernel(h_ref, att_ref, ys4_ref, wg_ref, bg_ref, wo_ref, g_ref, b_ref, o_ref, ys_ref, *, alpha):
    L = S5_CHUNK
    nq = ys4_ref.shape[0]
    nchunk = ys_ref.shape[1] // L
    for q in range(nq):
        for t in range(L):
            ys_ref[q, pl.ds(t, nchunk, stride=L), :] = ys4_ref[q, :, t * LANES:(t + 1) * LANES]
    ys = jnp.concatenate([ys_ref[q] for q in range(nq)], axis=1)
    half = att_ref.shape[1]
    gate = jax.nn.sigmoid(jnp.dot(ys.astype(BF16), wg_ref[...], preferred_element_type=F32) + bg_ref[...])
    mix = jnp.dot(att_ref[...], wo_ref[:half, :], preferred_element_type=F32)
    mix = mix + jnp.dot((ys * gate).astype(BF16), wo_ref[half:, :], preferred_element_type=F32)
    o_ref[...] = _layer_norm(alpha * h_ref[...] + mix, g_ref[...], b_ref[...])


def _out_odd_kernel(h_ref, hm_ref, wo_ref, g_ref, b_ref, o_ref, *, alpha):
    mix = jnp.dot(hm_ref[...], wo_ref[...], preferred_element_type=F32)
    o_ref[...] = _layer_norm(alpha * h_ref[...] + mix, g_ref[...], b_ref[...])


def _row_tiled_call(kern, row_args, fixed_args, out_dtype, name):
    t = row_args[0].shape[0]
    tm = TM_PROJ
    in_specs = [pl.BlockSpec((tm, a.shape[1]), lambda i: (i, 0)) for a in row_args]
    in_specs += [pl.BlockSpec(a.shape, lambda i: (0, 0)) for a in fixed_args]
    d = row_args[0].shape[1]
    return pl.pallas_call(
        kern,
        grid=(t // tm,),
        in_specs=in_specs,
        out_specs=pl.BlockSpec((tm, d), lambda i: (i, 0)),
        out_shape=jax.ShapeDtypeStruct((t, d), out_dtype),
        compiler_params=_cparams("parallel"),
        name=name,
    )(*row_args, *fixed_args)


def _router_kernel(h_ref, w_ref, b_ref, o_ref, cnt_ref, run_ref, w2_ref, *, n_groups, epg):
    @pl.when(pl.program_id(0) == 0)
    def _():
        run_ref[...] = jnp.zeros_like(run_ref)
        w = w_ref[...]
        w_hi = w.astype(BF16)
        w2_ref[:, :LANES] = w_hi
        w2_ref[:, LANES:] = (w - w_hi.astype(F32)).astype(BF16)

    h = h_ref[...]
    h_hi = h.astype(BF16)
    h_lo = (h - h_hi.astype(F32)).astype(BF16)
    part = jnp.dot(h_hi, w2_ref[...], preferred_element_type=F32)
    logits = (part[:, :LANES] + (part[:, LANES:] + jnp.dot(h_lo, w2_ref[:, :LANES], preferred_element_type=F32))
              + b_ref[...])
    lane = _iota(logits.shape, 1)
    big = jnp.int32(LANES)
    lg = jnp.where(lane < n_groups, logits, NEG_INF)
    mg = jnp.max(lg, axis=1, keepdims=True)
    g_val = 1.0 / jnp.sum(jnp.exp(lg - mg), axis=1, keepdims=True)
    g_idx = jnp.min(jnp.where(lg == mg, lane, big), axis=1, keepdims=True)
    lo_lane = n_groups + g_idx * epg
    le = jnp.where((lane >= lo_lane) & (lane < lo_lane + epg), logits, NEG_INF)
    m1 = jnp.max(le, axis=1, keepdims=True)
    i1 = jnp.min(jnp.where(le == m1, lane, big), axis=1, keepdims=True)
    le2 = jnp.where(lane == i1, NEG_INF, le)
    m2 = jnp.max(le2, axis=1, keepdims=True)
    i2 = jnp.min(jnp.where(le2 == m2, lane, big), axis=1, keepdims=True)
    r = jnp.exp(m2 - m1)
    w1 = g_val / (1.0 + r)
    w2 = g_val * r / (1.0 + r)
    e1 = i1 - lo_lane
    e2 = i2 - lo_lane
    first_lo = e1 < e2
    lo = jnp.where(first_lo, e1, e2)
    hi = jnp.where(first_lo, e2, e1)
    w_lo = jnp.where(first_lo, w1, w2)
    w_hi = jnp.where(first_lo, w2, w1)
    pair = (lo * (2 * epg - 1 - lo)) // 2 + (hi - lo - 1)
    cls = g_idx * (epg * (epg - 1) // 2) + pair
    tm = h.shape[0]
    onehot = lane == cls
    earlier = (_iota((tm, tm), 1) < _iota((tm, tm), 0)).astype(BF16)
    before = jnp.dot(earlier, onehot.astype(BF16), preferred_element_type=F32) + run_ref[...]
    rank = jnp.sum(jnp.where(onehot, before, 0.0), axis=1, keepdims=True)
    run = run_ref[...] + jnp.sum(onehot.astype(F32), axis=0, keepdims=True)
    run_ref[...] = run
    cnt_ref[...] = run
    out = jnp.where(lane == 0, cls.astype(F32),
                    jnp.where(lane == 1, w_lo, jnp.where(lane == 2, w_hi, jnp.where(lane == 3, rank, 0.0))))
    o_ref[...] = out


def _router(h, w_r, b_r, n_groups, epg):
    t, d = h.shape
    tm = TM_PROJ
    return pl.pallas_call(
        functools.partial(_router_kernel, n_groups=n_groups, epg=epg),
        grid=(t // tm,),
        in_specs=[pl.BlockSpec((tm, d), lambda i: (i, 0)),
                  pl.BlockSpec(w_r.shape, lambda i: (0, 0)),
                  pl.BlockSpec(b_r.shape, lambda i: (0, 0))],
        out_specs=[pl.BlockSpec((tm, LANES), lambda i: (i, 0)), pl.BlockSpec((1, LANES), lambda i: (0, 0))],
        out_shape=[jax.ShapeDtypeStruct((t, LANES), F32), jax.ShapeDtypeStruct((1, LANES), F32)],
        scratch_shapes=[pltpu.VMEM((1, LANES), F32), pltpu.VMEM((d, 2 * LANES), BF16)],
        compiler_params=_cparams("arbitrary"),
        name="router",
    )(h, w_r, b_r)


def _rows_wait(buf, sem):
    pltpu.make_async_copy(buf, buf, sem).wait()


def _dispatch_kernel(pos_ref, cnt_ref, pend_ref, x_ref, r_ref, o_hbm, xbuf, sem, zsem, *, tm, nsteps, ztile):
    i = pl.program_id(0)
    slot = i % 2
    d = x_ref.shape[1]

    @pl.when(i == 0)
    def _():
        zeros = xbuf.at[1, pl.ds(0, ztile)]
        xbuf[1] = jnp.zeros(xbuf.shape[1:], F32)
        ncls = cnt_ref.shape[0]
        used = pend_ref[ncls - 1]
        firsts = [(cnt_ref[c] > 0, pend_ref[c] - ztile) for c in range(ncls)]
        firsts += [(used + k * ztile < o_hbm.shape[0], used + k * ztile) for k in range(ncls)]
        for cond, first in firsts:
            @pl.when(cond)
            def _():
                pltpu.make_async_copy(zeros, o_hbm.at[pl.ds(pl.multiple_of(first, ztile), ztile)], zsem).start()
        for cond, _ in firsts:
            @pl.when(cond)
            def _():
                pltpu.make_async_copy(zeros, o_hbm.at[pl.ds(0, ztile)], zsem).wait()

    @pl.when(i >= 2)
    def _():
        _rows_wait(xbuf.at[slot], sem.at[slot])

    xbuf[slot, :, :d] = x_ref[...]
    xbuf[slot, :, d:] = r_ref[...]

    def body(r, c):
        pltpu.make_async_copy(xbuf.at[slot, pl.ds(r, 1)], o_hbm.at[pl.ds(pos_ref[0, 0, r], 1)], sem.at[slot]).start()
        return c
    lax.fori_loop(0, tm, body, 0, unroll=8)

    @pl.when(i == nsteps - 1)
    def _():
        _rows_wait(xbuf.at[slot], sem.at[slot])
        if nsteps > 1:
            _rows_wait(xbuf.at[1 - slot], sem.at[1 - slot])


def _dispatch(h, route, pos3, counts, pend, p_rows):
    t, d = h.shape
    w = d + route.shape[1]
    tm = pos3.shape[2]
    assert TM_MOE <= tm
    nsteps = t // tm
    smem = pl.BlockSpec(memory_space=pltpu.SMEM)
    return pl.pallas_call(
        functools.partial(_dispatch_kernel, tm=tm, nsteps=nsteps, ztile=TM_MOE),
        grid=(nsteps,),
        in_specs=[pl.BlockSpec((1, 1, tm), lambda i: (i, 0, 0), memory_space=pltpu.SMEM), smem, smem,
                  pl.BlockSpec((tm, d), lambda i: (i, 0)), pl.BlockSpec((tm, route.shape[1]), lambda i: (i, 0))],
        out_specs=pl.BlockSpec(memory_space=pl.ANY),
        out_shape=jax.ShapeDtypeStruct((p_rows, w), F32),
        scratch_shapes=[pltpu.VMEM((2, tm, w), F32), pltpu.SemaphoreType.DMA((2,)), pltpu.SemaphoreType.DMA(())],
        compiler_params=_cparams("arbitrary"),
        name="moe_dispatch",
    )(pos3, counts, pend, h, route)


def _combine_kernel(pos_ref, nxt_ref, h_ref, g_ref, b_ref, y_hbm, o_ref, ybuf, sem, *, alpha, tm, nsteps):
    i = pl.program_id(0)
    slot = i % 2

    def gather(p_ref, s):
        def body(r, c):
            pltpu.make_async_copy(y_hbm.at[pl.ds(p_ref[0, 0, r], 1)], ybuf.at[s, pl.ds(r, 1)], sem.at[s]).start()
            return c
        lax.fori_loop(0, tm, body, 0, unroll=8)

    @pl.when(i == 0)
    def _():
        gather(pos_ref, 0)

    @pl.when(i + 1 < nsteps)
    def _():
        gather(nxt_ref, 1 - slot)

    _rows_wait(ybuf.at[slot], sem.at[slot])
    o_ref[...] = _layer_norm(alpha * h_ref[...] + ybuf[slot], g_ref[...], b_ref[...])


def _combine(h, pos3, y_sorted, ln_g, ln_b, alpha):
    t, d = h.shape
    tm = pos3.shape[2]
    nsteps = t // tm
    row = lambda i: (i, 0)
    fixed = lambda i: (0, 0)
    return pl.pallas_call(
        functools.partial(_combine_kernel, alpha=alpha, tm=tm, nsteps=nsteps),
        grid=(nsteps,),
        in_specs=[pl.BlockSpec((1, 1, tm), lambda i: (i, 0, 0), memory_space=pltpu.SMEM),
                  pl.BlockSpec((1, 1, tm), lambda i: (jnp.minimum(i + 1, nsteps - 1), 0, 0), memory_space=pltpu.SMEM),
                  pl.BlockSpec((tm, d), row),
                  pl.BlockSpec((1, d), fixed), pl.BlockSpec((1, d), fixed),
                  pl.BlockSpec(memory_space=pl.ANY)],
        out_specs=pl.BlockSpec((tm, d), row),
        out_shape=jax.ShapeDtypeStruct((t, d), F32),
        scratch_shapes=[pltpu.VMEM((2, tm, d), F32), pltpu.SemaphoreType.DMA((2,))],
        compiler_params=_cparams("arbitrary"),
        name="moe_combine_ln",
    )(pos3, pos3, h, ln_g.reshape(1, -1), ln_b.reshape(1, -1), y_sorted)


def _moe_kernel(elo_ref, ehi_ref, nlive_ref, x_ref, g0_ref, u0_ref, d0_ref, g1_ref, u1_ref, d1_ref, y_ref,
                wg_ref, wu_ref, wd_ref):
    i = pl.program_id(0)
    d = y_ref.shape[1]
    live = i < nlive_ref[0]
    prev = jnp.maximum(i - 1, 0)
    experts = ((elo_ref, g0_ref, u0_ref, d0_ref), (ehi_ref, g1_ref, u1_ref, d1_ref))

    for j, (e_ref, g_ref, u_ref, d_ref) in enumerate(experts):
        @pl.when(live & ((i == 0) | (e_ref[i] != e_ref[prev])))
        def _():
            wg_ref[j] = g_ref[0].astype(BF16)
            wu_ref[j] = u_ref[0].astype(BF16)
            wd_ref[j] = d_ref[0].astype(BF16)

    @pl.when(jnp.logical_not(live))
    def _():
        y_ref[...] = jnp.zeros_like(y_ref)

    @pl.when(live)
    def _():
        x = x_ref[:, :d].astype(BF16)
        route = x_ref[:, d:]
        y = None
        for j in range(2):
            gate = jnp.dot(x, wg_ref[j], preferred_element_type=F32)
            up = jnp.dot(x, wu_ref[j], preferred_element_type=F32)
            hid = gate * jax.nn.sigmoid(gate) * up * route[:, j + 1:j + 2]
            part = jnp.dot(hid.astype(BF16), wd_ref[j], preferred_element_type=F32)
            y = part if y is None else y + part
        y_ref[...] = y


def _moe_experts(x_sorted, e_lo, e_hi, n_live, w_gate, w_up, w_down):
    p, xw = x_sorted.shape
    d = w_gate.shape[1]
    tm = TM_MOE
    f = w_gate.shape[2]
    lo_in = pl.BlockSpec((1, d, f), lambda i, lo, hi, nl: (lo[i], 0, 0))
    hi_in = pl.BlockSpec((1, d, f), lambda i, lo, hi, nl: (hi[i], 0, 0))
    lo_dn = pl.BlockSpec((1, f, d), lambda i, lo, hi, nl: (lo[i], 0, 0))
    hi_dn = pl.BlockSpec((1, f, d), lambda i, lo, hi, nl: (hi[i], 0, 0))
    grid_spec = pltpu.PrefetchScalarGridSpec(
        num_scalar_prefetch=3,
        grid=(p // tm,),
        in_specs=[pl.BlockSpec((tm, xw), lambda i, lo, hi, nl: (jnp.minimum(i, nl[0] - 1), 0)),
                  lo_in, lo_in, lo_dn, hi_in, hi_in, hi_dn],
        out_specs=pl.BlockSpec((tm, d), lambda i, lo, hi, nl: (i, 0)),
        scratch_shapes=[pltpu.VMEM((2, d, f), BF16), pltpu.VMEM((2, d, f), BF16), pltpu.VMEM((2, f, d), BF16)],
    )
    return pl.pallas_call(
        _moe_kernel,
        grid_spec=grid_spec,
        out_shape=jax.ShapeDtypeStruct((p, d), F32),
        compiler_params=_cparams("arbitrary"),
        name="moe_experts",
    )(e_lo, e_hi, n_live, x_sorted, w_gate, w_up, w_down, w_gate, w_up, w_down)


def _hier_moe_ln(h, ln_g, ln_b, alpha, w_group, b_group, w_expert, b_expert, w_gate, w_up, w_down):
    t, d = h.shape
    n_groups, _, epg = w_expert.shape
    npairs = epg * (epg - 1) // 2
    ncls = n_groups * npairs
    tm = TM_MOE
    w_r = jnp.concatenate([w_group, w_expert.transpose(1, 0, 2).reshape(d, n_groups * epg)], axis=1)
    b_r = jnp.concatenate([b_group, b_expert.reshape(-1)])
    pad = LANES - w_r.shape[1]
    w_r = jnp.pad(w_r, ((0, 0), (0, pad)))
    b_r = jnp.pad(b_r, (0, pad)).reshape(1, LANES)
    route, cnt = _router(h, w_r, b_r, n_groups, epg)

    cls = route[:, 0].astype(jnp.int32)
    rank = route[:, 3].astype(jnp.int32)
    counts = cnt[0, :ncls].astype(jnp.int32)
    padded = ((counts + tm - 1) // tm) * tm
    pend = jnp.cumsum(padded)
    pstart = pend - padded
    pos = jnp.sum(jnp.where(cls[:, None] == jnp.arange(ncls)[None, :], pstart[None, :], 0), axis=1) + rank
    pos3 = pos.astype(jnp.int32).reshape(t // TM_PROJ, 1, TM_PROJ)
    p_rows = t + ncls * tm
    x_sorted = _dispatch(h, route, pos3, counts, pend.astype(jnp.int32), p_rows)

    tile_start = jnp.arange(p_rows // tm, dtype=jnp.int32) * tm
    tile_start = jnp.minimum(tile_start, pend[-1] - tm)
    tile_cls = jnp.sum(pend[None, :] <= tile_start[:, None], axis=1).astype(jnp.int32)
    n_live = (pend[-1:] // tm).astype(jnp.int32)
    pair = tile_cls % npairs
    pair_lo = sum(jnp.where(pair == k, v, 0) for k, v in enumerate(PAIRS_LO))
    pair_hi = sum(jnp.where(pair == k, v, 0) for k, v in enumerate(PAIRS_HI))
    e_lo = ((tile_cls // npairs) * epg + pair_lo).astype(jnp.int32)
    e_hi = ((tile_cls // npairs) * epg + pair_hi).astype(jnp.int32)

    y_sorted = _moe_experts(x_sorted, e_lo, e_hi, n_live, w_gate, w_up, w_down)
    return _combine(h, pos3, y_sorted, ln_g, ln_b, alpha)


def _even_mixer(h, batch, seq, alpha, ln_g, ln_b, w_in, f_bias, s5_params, w_glu, b_glu, w_out):
    t, d = h.shape
    heads = f_bias.shape[0]
    groups, p_state = s5_params[0].shape
    hc = s5_params[3].shape[-1]
    s5_width = groups * hc
    fox_width = d - s5_width
    dh = fox_width // heads
    q_scale = dh ** -0.5
    n_qkv = 3 * fox_width
    col_scale = jnp.where(jnp.arange(n_qkv) < fox_width, q_scale, 1.0).astype(F32)
    w_qkv = (w_in[:, :n_qkv] * col_scale[None, :]).astype(BF16)
    w_u = w_in[:, n_qkv + heads:].astype(BF16)
    hp = heads // 2
    w_f = w_in[:, n_qkv:n_qkv + heads]
    w_f = jnp.pad(w_f.reshape(d, hp, 2), ((0, 0), (0, 0), (0, LANES - 2))).reshape(d, hp * LANES).astype(BF16)
    fb = jnp.pad(f_bias.reshape(hp, 1, 2), ((0, 0), (0, 0), (0, LANES - 2)))
    qkv, u4, f_cols = _proj_even(h, w_qkv, w_u, w_f)
    att = _fox_attention(qkv, f_cols, fb, batch, seq, heads, dh)
    ys4 = _s5(u4, _s5_tables(*s5_params), batch, seq)
    tm = TM_PROJ
    row = lambda i: (i, 0)
    fixed = lambda i: (0, 0)
    fixed_args = (w_glu.astype(BF16), b_glu.reshape(1, -1), w_out.astype(BF16), ln_g.reshape(1, -1), ln_b.reshape(1, -1))
    return pl.pallas_call(
        functools.partial(_out_even_kernel, alpha=alpha),
        grid=(t // tm,),
        in_specs=[pl.BlockSpec((tm, d), row), pl.BlockSpec((tm, fox_width), row),
                  pl.BlockSpec((ys4.shape[0], tm // S5_CHUNK, ys4.shape[2]), lambda i: (0, i, 0))]
        + [pl.BlockSpec(a.shape, fixed) for a in fixed_args],
        out_specs=pl.BlockSpec((tm, d), row),
        out_shape=jax.ShapeDtypeStruct((t, d), F32),
        scratch_shapes=[pltpu.VMEM((ys4.shape[0], tm, LANES), F32)],
        compiler_params=_cparams("parallel"),
        name="out_even",
    )(h, att, ys4, *fixed_args)


def _odd_mixer(h, batch, seq, alpha, ln_g, ln_b, w_in, conv_w, conv_b, i_bias, f_bias, w_out):
    t, d = h.shape
    heads = i_bias.shape[0]
    dmix = conv_w.shape[1] // 2
    dh = dmix // heads
    q, k, v, o, gates = _proj_odd(h, w_in.astype(BF16), conv_w, conv_b, seq, dh)
    nh = MLSTM_HEADS_PER_STEP
    g = gates[:, :2 * heads].reshape(batch, seq, 2, heads).transpose(0, 3, 2, 1)
    gate_rows = g.reshape(batch, heads // nh, nh, 2, seq // MLSTM_CHUNK, MLSTM_CHUNK)
    gate_bias = jnp.stack([i_bias, f_bias], axis=1).reshape(heads // nh, nh, 2, 1, 1)
    hm = _mlstm(q, k, v, o, gate_rows, gate_bias, batch, seq, heads, dh)
    kern = functools.partial(_out_odd_kernel, alpha=alpha)
    return _row_tiled_call(kern, (h, hm), (w_out.astype(BF16), ln_g.reshape(1, -1), ln_b.reshape(1, -1)),
                           F32, "out_odd")


def kernel(x, ln_g, ln_b, even_w_in, fox_f_bias, s5_a_re, s5_a_im, s5_log_dt, s5_b_re, s5_b_im, s5_c_re, s5_c_im, s5_d, s5_w_glu, s5_b_glu, even_w_out, odd_w_in, mlstm_conv_w, mlstm_conv_b, mlstm_i_bias, mlstm_f_bias, odd_w_out, moe_w_group, moe_b_group, moe_w_expert, moe_b_expert, moe_w_gate, moe_w_up, moe_w_down):
    batch, seq, d = x.shape
    depth = ln_g.shape[0]
    alpha = (2 * depth) ** 0.25
    h = x.reshape(batch * seq, d)
    for layer in range(depth):
        j = layer // 2
        if layer % 2 == 0:
            s5_params = (s5_a_re[j], s5_a_im[j], s5_log_dt[j], s5_b_re[j], s5_b_im[j],
                         s5_c_re[j], s5_c_im[j], s5_d[j])
            h = _even_mixer(h, batch, seq, alpha, ln_g[layer, 0], ln_b[layer, 0], even_w_in[j], fox_f_bias[j],
                            s5_params, s5_w_glu[j], s5_b_glu[j], even_w_out[j])
        else:
            h = _odd_mixer(h, batch, seq, alpha, ln_g[layer, 0], ln_b[layer, 0], odd_w_in[j], mlstm_conv_w[j],
                           mlstm_conv_b[j], mlstm_i_bias[j], mlstm_f_bias[j], odd_w_out[j])
        h = _hier_moe_ln(h, ln_g[layer, 1], ln_b[layer, 1], alpha, moe_w_group[layer], moe_b_group[layer],
                         moe_w_expert[layer], moe_b_expert[layer], moe_w_gate[layer], moe_w_up[layer],
                         moe_w_down[layer])
    return h.reshape(batch, seq, d)
```

```python
import functools

import jax
import jax.numpy as jnp
from jax import lax
from jax.experimental import pallas as pl
from jax.experimental.pallas import tpu as pltpu

F32 = jnp.float32
BF16 = jnp.bfloat16
HIGHEST = lax.Precision.HIGHEST
LN_EPS = 1e-5
NEG_INF = float("-inf")

LANES = 128
VMEM_LIMIT = 56 * 1024 * 1024

TM_PROJ = 256
ATT_BLOCK = 256
ATT_QUERY_BLOCK = 512
S5_CHUNK = 16
MLSTM_CHUNK = 256
MLSTM_HEADS_PER_STEP = 4
TM_MOE = 256
PAIRS_LO = (0, 0, 0, 1, 1, 2)
PAIRS_HI = (1, 2, 3, 2, 3, 3)

NT_DIMS = (((1,), (1,)), ((), ()))
TN_DIMS = (((0,), (0,)), ((), ()))


def _cparams(*sem):
    return pltpu.CompilerParams(dimension_semantics=sem, vmem_limit_bytes=VMEM_LIMIT)


def _log_sigmoid(x):
    return jnp.minimum(x, 0.0) - jnp.log1p(jnp.exp(-jnp.abs(x)))


def _layer_norm(x, g, b):
    mu = jnp.mean(x, axis=-1, keepdims=True)
    xc = x - mu
    var = jnp.mean(xc * xc, axis=-1, keepdims=True)
    return xc * lax.rsqrt(var + LN_EPS) * g + b


def _iota(shape, dim):
    return lax.broadcasted_iota(jnp.int32, shape, dim)


def _proj_even_kernel(x_ref, wqkv_ref, wu_ref, wf_ref, qkv_ref, u_ref, f_ref, zs_ref):
    xb = x_ref[...].astype(BF16)
    qkv_ref[...] = jnp.dot(xb, wqkv_ref[...], preferred_element_type=F32).astype(BF16)
    f_ref[...] = jnp.dot(xb, wf_ref[...], preferred_element_type=F32)
    z = jnp.dot(xb, wu_ref[...], preferred_element_type=F32)
    L = S5_CHUNK
    nchunk = x_ref.shape[0] // L
    for q in range(zs_ref.shape[0]):
        zs_ref[q] = z[:, q * LANES:(q + 1) * LANES]
        for s in range(L):
            u_ref[q, :, s * LANES:(s + 1) * LANES] = zs_ref[q, pl.ds(s, nchunk, stride=L), :].astype(BF16)


def _proj_even(x, w_qkv, w_u, w_f):
    t, d = x.shape
    tm = TM_PROJ
    L = S5_CHUNK
    n_qkv, n_u, n_f = w_qkv.shape[1], w_u.shape[1], w_f.shape[1]
    nq = n_u // LANES
    fixed = lambda i: (0, 0)
    return pl.pallas_call(
        _proj_even_kernel,
        grid=(t // tm,),
        in_specs=[pl.BlockSpec((tm, d), lambda i: (i, 0)), pl.BlockSpec(w_qkv.shape, fixed),
                  pl.BlockSpec(w_u.shape, fixed), pl.BlockSpec(w_f.shape, fixed)],
        out_specs=[pl.BlockSpec((tm, n_qkv), lambda i: (i, 0)),
                   pl.BlockSpec((nq, tm // L, L * LANES), lambda i: (0, i, 0)),
                   pl.BlockSpec((tm, n_f), lambda i: (i, 0))],
        out_shape=[jax.ShapeDtypeStruct((t, n_qkv), BF16),
                   jax.ShapeDtypeStruct((nq, t // L, L * LANES), BF16),
                   jax.ShapeDtypeStruct((t, n_f), F32)],
        scratch_shapes=[pltpu.VMEM((nq, tm, LANES), F32)],
        compiler_params=_cparams("parallel"),
        name="proj_even",
    )(x, w_qkv, w_u, w_f)


def _proj_odd_kernel(x_ref, w_ref, cw_ref, cb_ref, q_ref, k_ref, v_ref, o_ref, g_ref, *zs_refs,
                     tm, dmix, k_scale, tiles_per_seq, conv_width):
    i = pl.program_id(0)
    xb = x_ref[...].astype(BF16)
    cw = zs_refs[0].shape[1]

    @pl.when(i % tiles_per_seq == 0)
    def _():
        for zs_ref in zs_refs:
            zs_ref[0:8, :] = jnp.zeros((8, cw), F32)

    gates = jnp.dot(xb, w_ref[:, 4 * dmix:], preferred_element_type=F32)
    g_ref[...] = jnp.concatenate([gates, jnp.zeros((tm, LANES - gates.shape[1]), F32)], axis=1)
    for zs_ref, c0 in zip(zs_refs, range(0, 2 * dmix, cw)):
        zs_ref[8:tm + 8, :] = jnp.dot(xb, w_ref[:, c0:c0 + cw], preferred_element_type=F32)
        vo = jnp.dot(xb, w_ref[:, 2 * dmix + c0:2 * dmix + c0 + cw], preferred_element_type=F32)
        if c0 < dmix:
            v_ref[:, c0:c0 + cw] = vo.astype(BF16)
        else:
            o_ref[:, c0 - dmix:c0 - dmix + cw] = vo
        cols = slice(c0, c0 + cw)
        acc = jnp.broadcast_to(cb_ref[:, cols], (tm, cw))
        for j in range(conv_width):
            acc = acc + cw_ref[j:j + 1, cols] * zs_ref[pl.ds(8 - (conv_width - 1) + j, tm), :]
        y = acc * jax.nn.sigmoid(acc)
        if c0 < dmix:
            q_ref[:, cols] = y.astype(BF16)
        else:
            k_ref[:, c0 - dmix:c0 - dmix + cw] = (y * k_scale).astype(BF16)
        zs_ref[0:8, :] = zs_ref[tm:tm + 8, :]


def _proj_odd(x, w, conv_w, conv_b, seq, head_dim):
    t, d = x.shape
    dmix = conv_w.shape[1] // 2
    tm = TM_PROJ
    kern = functools.partial(_proj_odd_kernel, tm=tm, dmix=dmix, k_scale=head_dim ** -0.5,
                             tiles_per_seq=seq // tm, conv_width=conv_w.shape[0])
    row = lambda i: (i, 0)
    fixed = lambda i: (0, 0)
    return pl.pallas_call(
        kern,
        grid=(t // tm,),
        in_specs=[pl.BlockSpec((tm, d), row), pl.BlockSpec(w.shape, fixed),
                  pl.BlockSpec(conv_w.shape, fixed), pl.BlockSpec((1, 2 * dmix), fixed)],
        out_specs=[pl.BlockSpec((tm, dmix), row)] * 4 + [pl.BlockSpec((tm, LANES), row)],
        out_shape=[jax.ShapeDtypeStruct((t, dmix), BF16)] * 3
        + [jax.ShapeDtypeStruct((t, dmix), F32), jax.ShapeDtypeStruct((t, LANES), F32)],
        scratch_shapes=[pltpu.VMEM((tm + 8, 512), F32)] * (2 * dmix // 512),
        compiler_params=_cparams("arbitrary"),
        name="proj_odd",
    )(x, w, conv_w, conv_b.reshape(1, -1))


def _split3(x):
    hi = x.astype(BF16).astype(F32)
    r = x - hi
    mid = r.astype(BF16).astype(F32)
    lo = (r - mid).astype(BF16).astype(F32)
    return hi, mid, lo


def _fox_kernel(fb_ref, f_ref, q_ref, k_ref, v_ref, o_ref, c_ref, kaug_ref, vt_ref, acc_ref, *, bq, blk, seq, dh):
    qi = pl.program_id(2)
    lane = _iota((blk, LANES), 1)
    head_lanes = (lane < dh, lane >= dh)

    @pl.when(qi == 0)
    def _():
        tril = (_iota((blk, blk), 1) <= _iota((blk, blk), 0)).astype(BF16)
        eye_b = (_iota((LANES, LANES), 0) == _iota((LANES, LANES), 1)).astype(BF16)

        def prep(jb, carry):
            rows = pl.ds(pl.multiple_of(jb * blk, blk), blk)
            ls = _log_sigmoid(f_ref[rows, :] + fb_ref[0])
            c3 = jnp.dot(tril, jnp.concatenate(_split3(ls), axis=1).astype(BF16), preferred_element_type=F32)
            c = c3[:, :LANES] + c3[:, LANES:2 * LANES] + c3[:, 2 * LANES:] + carry
            c_ref[rows, :] = c
            kblk = k_ref[rows, :]
            for j in range(2):
                hi, mid, lo = _split3(c[:, j:j + 1])
                aug = jnp.where(lane < 3, 1.0, jnp.where(lane == 3, -hi, jnp.where(
                    lane == 4, -mid, jnp.where(lane == 5, -lo, 0.0))))
                kaug_ref[j, rows, :] = jnp.concatenate(
                    [jnp.where(head_lanes[j], kblk, jnp.zeros_like(kblk)), aug.astype(BF16)], axis=1)
            vt_ref[jb] = lax.dot_general(eye_b, v_ref[rows, :], NT_DIMS,
                                         preferred_element_type=F32).astype(BF16)
            return c[blk - 1:blk, :]

        lax.fori_loop(0, seq // blk, prep, jnp.zeros((1, LANES), F32))

    q = q_ref[...]
    qlane = _iota((bq, LANES), 1)
    c_q = c_ref[pl.ds(pl.multiple_of(qi * bq, bq), bq), :]
    q_aug = []
    for j in range(2):
        hi, mid, lo = _split3(c_q[:, j:j + 1])
        aug = jnp.where(qlane == 0, hi, jnp.where(qlane == 1, mid, jnp.where(
            qlane == 2, lo, jnp.where(qlane < 6, 1.0, 0.0))))
        q_head = jnp.where((qlane < dh) if j == 0 else (qlane >= dh), q, jnp.zeros_like(q))
        q_aug.append(jnp.concatenate([q_head, aug.astype(BF16)], axis=1))
    acc_ref[...] = jnp.zeros_like(acc_ref)
    r = bq // blk

    def score(kb):
        krows = pl.ds(pl.multiple_of(kb * blk, blk), blk)
        return tuple(lax.dot_general(kaug_ref[j, krows, :], q_aug[j], NT_DIMS, preferred_element_type=F32)
                     for j in range(2))

    def accumulate(kb, probs, alphas):
        vt = vt_ref[kb]
        for j in range(2):
            acc_ref[j] = alphas[j] * acc_ref[j] + jnp.dot(vt[j * dh:(j + 1) * dh, :], probs[j],
                                                          preferred_element_type=F32)

    def softmax(scores, stats, diag):
        out, probs, alphas = [], [], []
        for j in range(2):
            m, l = stats[2 * j], stats[2 * j + 1]
            s = scores[j]
            if diag is not None:
                s = jnp.where(_iota((blk, bq), 0) + diag * blk <= _iota((blk, bq), 1), s, NEG_INF)
            m_new = jnp.maximum(m, jnp.max(s, axis=0, keepdims=True))
            alpha = jnp.exp(m - m_new)
            p = jnp.exp(s - m_new)
            out += [m_new, alpha * l + jnp.sum(p, axis=0, keepdims=True)]
            probs.append(p.astype(BF16))
            alphas.append(alpha)
        return tuple(out), tuple(probs), tuple(alphas)

    m0 = jnp.full((1, bq), NEG_INF, F32)
    l0 = jnp.zeros((1, bq), F32)

    def step(kb, state):
        scores, stats = state
        nxt = score(kb + 1)
        stats, probs, alphas = softmax(scores, stats, None)
        accumulate(kb, probs, alphas)
        return nxt, stats

    first = r * qi
    scores, stats = lax.fori_loop(0, first, step, (score(0), (m0, l0, m0, l0)))
    for diag in range(r):
        nxt = score(first + diag + 1) if diag + 1 < r else None
        stats, probs, alphas = softmax(scores, stats, diag)
        accumulate(first + diag, probs, alphas)
        scores = nxt
    out_t = jnp.concatenate([acc_ref[0] / stats[1], acc_ref[1] / stats[3]], axis=0).astype(BF16)
    eye_q = (_iota((blk, blk), 0) == _iota((blk, blk), 1)).astype(BF16)
    for c0 in range(0, bq, blk):
        o_ref[c0:c0 + blk, :] = lax.dot_general(eye_q, out_t[:, c0:c0 + blk], NT_DIMS,
                                                preferred_element_type=F32).astype(o_ref.dtype)


def _fox_attention(qkv, f_cols, f_bias, batch, seq, heads, dh):
    t = qkv.shape[0]
    blk = ATT_BLOCK
    bq = min(ATT_QUERY_BLOCK, seq)
    nq = seq // bq
    hp = heads * dh // LANES
    kern = functools.partial(_fox_kernel, bq=bq, blk=blk, seq=seq, dh=dh)
    return pl.pallas_call(
        kern,
        grid=(batch, hp, nq),
        in_specs=[pl.BlockSpec((1, 1, LANES), lambda b, p, i: (p, 0, 0)),
                  pl.BlockSpec((seq, LANES), lambda b, p, i: (b, p)),
                  pl.BlockSpec((bq, LANES), lambda b, p, i: (b * nq + i, p)),
                  pl.BlockSpec((seq, LANES), lambda b, p, i: (b, hp + p)),
                  pl.BlockSpec((seq, LANES), lambda b, p, i: (b, 2 * hp + p))],
        out_specs=pl.BlockSpec((bq, LANES), lambda b, p, i: (b * nq + i, p)),
        out_shape=jax.ShapeDtypeStruct((t, heads * dh), BF16),
        scratch_shapes=[pltpu.VMEM((seq, LANES), F32), pltpu.VMEM((2, seq, 2 * LANES), BF16),
                        pltpu.VMEM((seq // blk, LANES, blk), BF16), pltpu.VMEM((2, dh, bq), F32)],
        compiler_params=_cparams("parallel", "parallel", "arbitrary"),
        name="fox_attention",
    )(f_bias, f_cols, qkv, qkv, qkv)


def _s5_tables(a_re, a_im, log_dt, b_re, b_im, c_re, c_im, d_skip):
    L = S5_CHUNK
    g, p = a_re.shape
    hc = b_re.shape[-1]
    dt = jnp.exp(log_dt)[:, None]
    mag = jnp.exp(a_re * dt)
    lb_re = mag * jnp.cos(a_im * dt)
    lb_im = mag * jnp.sin(a_im * dt)
    num_re = lb_re - 1.0
    num_im = lb_im
    den = a_re * a_re + a_im * a_im
    z_re = (num_re * a_re + num_im * a_im) / den
    z_im = (num_im * a_re - num_re * a_im) / den
    bb_re = z_re[..., None] * b_re - z_im[..., None] * b_im
    bb_im = z_re[..., None] * b_im + z_im[..., None] * b_re
    tau = jnp.arange(L + 1, dtype=F32)
    pmag = jnp.exp((a_re * dt)[..., None] * tau)
    pw_re = pmag * jnp.cos((a_im * dt)[..., None] * tau)
    pw_im = pmag * jnp.sin((a_im * dt)[..., None] * tau)
    cp_re = c_re[..., None] * pw_re[:, None] - c_im[..., None] * pw_im[:, None]
    cp_im = c_re[..., None] * pw_im[:, None] + c_im[..., None] * pw_re[:, None]
    kern = (jnp.einsum("gopt,gpi->gtoi", cp_re[..., :L], bb_re, precision=HIGHEST)
            - jnp.einsum("gopt,gpi->gtoi", cp_im[..., :L], bb_im, precision=HIGHEST))
    kern = kern.at[:, 0].add(d_skip[:, :, None] * jnp.eye(hc, dtype=F32))
    rev = (L - 1) - jnp.arange(L)
    e_re = pw_re[:, :, rev][..., None] * bb_re[:, :, None] - pw_im[:, :, rev][..., None] * bb_im[:, :, None]
    e_im = pw_re[:, :, rev][..., None] * bb_im[:, :, None] + pw_im[:, :, rev][..., None] * bb_re[:, :, None]
    o_re = cp_re[..., 1:]
    o_im = -cp_im[..., 1:]

    gq = LANES // hc
    nq = g // gq

    def tile(m, perm):
        m = m.astype(BF16).reshape(nq, gq, *m.shape[1:])
        return jnp.moveaxis(m, 1, perm)

    def embed(m2, spread, row_group, col_group):
        full = jnp.dot(m2, spread.astype(BF16), preferred_element_type=F32)
        keep = row_group(jnp.arange(full.shape[0]))[:, None] == col_group(jnp.arange(full.shape[1]))[None, :]
        return jnp.where(keep, full, 0.0).astype(BF16)

    eye_c = jnp.tile(jnp.eye(hc, dtype=F32), (1, gq))
    eye_p = jnp.tile(jnp.eye(p, dtype=F32), (1, gq))
    chan_group = lambda r: (r // hc) % gq
    k_t = tile(kern.transpose(0, 1, 3, 2), 2).reshape(nq * L * LANES, hc)
    lag_blocks = embed(k_t, eye_c, chan_group, chan_group).reshape(nq, L, LANES, LANES)
    w_end = jnp.concatenate(
        [embed(tile(e.transpose(0, 2, 3, 1), 2).reshape(nq * L * LANES, p), eye_p, chan_group,
               lambda c: c // p).reshape(nq, L * LANES, gq * p) for e in (e_re, e_im)], axis=2)
    w_out = jnp.concatenate(
        [embed(tile(o.transpose(0, 2, 3, 1), 1).reshape(nq * gq * p, L * hc), jnp.kron(jnp.eye(L, dtype=F32), eye_c),
               lambda r: (r // p) % gq, chan_group).reshape(nq, gq * p, L * LANES) for o in (o_re, o_im)], axis=1)
    lam_re = pw_re[..., L].reshape(1, g * p)
    lam_im = pw_im[..., L].reshape(1, g * p)
    return lag_blocks, w_end, w_out, lam_re, lam_im


def _s5_end_kernel(u_ref, w_ref, ere_ref, eim_ref):
    e = jnp.dot(u_ref[0], w_ref[0], preferred_element_type=F32)
    ns = ere_ref.shape[0]
    for k in range(ns):
        ere_ref[k] = e[:, k * LANES:(k + 1) * LANES]
        eim_ref[k] = e[:, (ns + k) * LANES:(ns + k + 1) * LANES]


def _s5_scan_kernel(lre_ref, lim_ref, ere_ref, eim_ref, hre_ref, him_ref, *, batch, nchunk):
    ns = ere_ref.shape[0]
    lr = [lre_ref[k] for k in range(ns)]
    li = [lim_ref[k] for k in range(ns)]

    def step(j, carry):
        sl = pl.ds(j, batch, stride=nchunk)
        out = []
        for k in range(ns):
            hr, hi = carry[2 * k], carry[2 * k + 1]
            hre_ref[k, sl, :] = hr
            him_ref[k, sl, :] = hi
            out += [lr[k] * hr - li[k] * hi + ere_ref[k, sl, :], lr[k] * hi + li[k] * hr + eim_ref[k, sl, :]]
        return tuple(out)

    z = jnp.zeros((batch, LANES), F32)
    lax.fori_loop(0, nchunk, step, (z,) * (2 * ns))


def _s5_out_kernel(u_ref, lag_ref, hre_ref, him_ref, w_ref, y_ref, t_ref):
    L = lag_ref.shape[1]

    @pl.when(pl.program_id(1) == 0)
    def _():
        for s in range(L):
            for t in range(s, L):
                t_ref[s * LANES:(s + 1) * LANES, t * LANES:(t + 1) * LANES] = lag_ref[0, t - s]
            if s % 2 == 1:
                t_ref[s * LANES:(s + 1) * LANES, (s - 1) * LANES:s * LANES] = jnp.zeros((LANES, LANES), BF16)

    half = w_ref.shape[1] // 2
    ns = hre_ref.shape[0]
    h_re = jnp.concatenate([hre_ref[k] for k in range(ns)], axis=1).astype(BF16)
    h_im = jnp.concatenate([him_ref[k] for k in range(ns)], axis=1).astype(BF16)
    inter = jnp.dot(h_re, w_ref[0, :half, :], preferred_element_type=F32)
    inter = inter + jnp.dot(h_im, w_ref[0, half:, :], preferred_element_type=F32)
    ct = 2 * LANES
    for c0 in range(0, t_ref.shape[1], ct):
        k_hi = c0 + ct
        y = jnp.dot(u_ref[0, :, :k_hi], t_ref[:k_hi, c0:c0 + ct], preferred_element_type=F32)
        y_ref[0, :, c0:c0 + ct] = jax.nn.gelu(y + inter[:, c0:c0 + ct])


def _s5(u4, tables, batch, seq):
    lag_blocks, w_end, w_out, lam_re, lam_im = tables
    L = S5_CHUNK
    nq, rows, kc = u4.shape
    nchunk = seq // L
    sw = w_end.shape[2] // 2
    ns = sw // LANES
    rb = rows // 2
    e_re, e_im = pl.pallas_call(
        _s5_end_kernel,
        grid=(nq,),
        in_specs=[pl.BlockSpec((1, rows, kc), lambda q: (q, 0, 0)),
                  pl.BlockSpec((1, kc, 2 * sw), lambda q: (q, 0, 0))],
        out_specs=[pl.BlockSpec((ns, rows, LANES), lambda q: (q, 0, 0))] * 2,
        out_shape=[jax.ShapeDtypeStruct((nq * ns, rows, LANES), F32)] * 2,
        compiler_params=_cparams("parallel"),
        name="s5_chunk_end",
    )(u4, w_end)
    lam_spec = pl.BlockSpec((ns, 1, LANES), lambda q: (q, 0, 0))
    st_spec = pl.BlockSpec((ns, rows, LANES), lambda q: (q, 0, 0))
    h_re, h_im = pl.pallas_call(
        functools.partial(_s5_scan_kernel, batch=batch, nchunk=nchunk),
        grid=(nq,),
        in_specs=[lam_spec, lam_spec, st_spec, st_spec],
        out_specs=[st_spec] * 2,
        out_shape=[jax.ShapeDtypeStruct(e_re.shape, F32)] * 2,
        compiler_params=_cparams("parallel"),
        name="s5_chunk_scan",
    )(lam_re.reshape(nq * ns, 1, LANES), lam_im.reshape(nq * ns, 1, LANES), e_re, e_im)
    hs_spec = pl.BlockSpec((ns, rb, LANES), lambda q, r: (q, r, 0))
    return pl.pallas_call(
        _s5_out_kernel,
        grid=(nq, rows // rb),
        in_specs=[pl.BlockSpec((1, rb, kc), lambda q, r: (q, r, 0)),
                  pl.BlockSpec((1, L, LANES, LANES), lambda q, r: (q, 0, 0, 0)),
                  hs_spec, hs_spec,
                  pl.BlockSpec((1, 2 * sw, kc), lambda q, r: (q, 0, 0))],
        out_specs=pl.BlockSpec((1, rb, kc), lambda q, r: (q, r, 0)),
        out_shape=jax.ShapeDtypeStruct((nq, rows, kc), F32),
        scratch_shapes=[pltpu.VMEM((kc, kc), BF16)],
        compiler_params=_cparams("parallel", "arbitrary"),
        name="s5_out",
    )(u4, lag_blocks, h_re, h_im, w_out)


def _rows_to_cols(eye3, sub, a_row, b_row):
    a3 = jnp.concatenate(_split3(a_row), axis=1)
    b3 = jnp.concatenate(_split3(b_row), axis=1)
    rows = jnp.where(sub < LANES, a3, b3).astype(BF16)
    cols = lax.dot_general(eye3, rows, NT_DIMS, preferred_element_type=F32)
    return cols[:, :LANES], cols[:, LANES:]


def _mlstm_kernel(gb_ref, g_ref, q_ref, k_ref, v_ref, o_ref, h_ref, c_ref, b_scr, i_scr, *, L, seq, dh, nh):
    tri = (_iota((L, L), 0) <= _iota((L, L), 1)).astype(F32)
    eye = jnp.concatenate([(_iota((L, L), 0) == _iota((L, L), 1)).astype(BF16)] * 3, axis=1)
    causal = _iota((L, L), 1) <= _iota((L, L), 0)
    lane = _iota((L, LANES), 1)
    one_col = (lane == 0).astype(BF16)
    sub = _iota((2 * LANES, 3 * L), 0)
    c_ref[...] = jnp.zeros_like(c_ref)
    for hh in range(nh):
        log_f = _log_sigmoid(g_ref[0, 0, hh, 1] + gb_ref[0, hh, 1])
        b_scr[hh] = jnp.dot(log_f, tri, precision=HIGHEST, preferred_element_type=F32)
        i_scr[hh] = g_ref[0, 0, hh, 0] + gb_ref[0, hh, 0]

    def chunk(c, carry):
        st = pl.multiple_of(c * L, L)
        heads = range(nh)
        cols_h = [slice(hh * dh, (hh + 1) * dh) for hh in heads]
        b_row = [b_scr[hh, pl.ds(c, 1), :] for hh in heads]
        li_row = [i_scr[hh, pl.ds(c, 1), :] for hh in heads]
        cols = [_rows_to_cols(eye, sub, b_row[hh], li_row[hh]) for hh in heads]
        q = [q_ref[pl.ds(st, L), cols_h[hh]] for hh in heads]
        k = [k_ref[pl.ds(st, L), cols_h[hh]] for hh in heads]
        v = [v_ref[pl.ds(st, L), cols_h[hh]] for hh in heads]
        qk = [lax.dot_general(q[hh], k[hh], NT_DIMS, preferred_element_type=F32) for hh in heads]
        qc = [jnp.dot(q[hh], c_ref[hh].astype(BF16), preferred_element_type=F32) for hh in heads]

        s, m_t, m_inter, m_new, w_col, decay = [], [], [], [], [], []
        for hh in heads:
            m_prev = carry[hh]
            b_full, li_full = cols[hh]
            b_col = b_full[:, 0:1]
            b_last = b_row[hh][:, L - 1:L]
            log_d = jnp.where(causal, jnp.tile(b_full, (1, L // LANES)) - b_row[hh] + li_row[hh], NEG_INF)
            m_inter.append(b_col + m_prev)
            m_t.append(jnp.maximum(m_inter[hh], jnp.max(log_d, axis=1, keepdims=True)))
            s.append((qk[hh] * jnp.exp(log_d - m_t[hh])).astype(BF16))
            g_row = b_last - b_row[hh] + li_row[hh]
            m_new.append(jnp.maximum(b_last + m_prev, jnp.max(g_row, axis=1, keepdims=True)))
            w_col.append(jnp.exp(b_last - b_full + li_full - m_new[hh]))
            decay.append(jnp.exp(b_last + m_prev - m_new[hh]))

        tot = [jnp.dot(s[hh], jnp.concatenate([v[hh], one_col], axis=1), preferred_element_type=F32)
               for hh in heads]
        upd = [lax.dot_general(k[hh], jnp.concatenate(
            [(v[hh].astype(F32) * w_col[hh]).astype(BF16), jnp.where(lane == 0, w_col[hh], 0.0).astype(BF16)],
            axis=1), TN_DIMS, preferred_element_type=F32) for hh in heads]
        for hh in heads:
            t_h = tot[hh] + jnp.exp(m_inter[hh] - m_t[hh]) * qc[hh]
            den = jnp.maximum(jnp.abs(t_h[:, dh:dh + 1]), jnp.exp(-m_t[hh]))
            h = t_h[:, :dh] / den
            h_ref[pl.ds(st, L), cols_h[hh]] = (h * jax.nn.sigmoid(o_ref[pl.ds(st, L), cols_h[hh]])).astype(h_ref.dtype)
            c_ref[hh] = decay[hh] * c_ref[hh] + upd[hh]
        return tuple(m_new)

    lax.fori_loop(0, seq // L, chunk, (jnp.zeros((1, 1), F32),) * nh)


def _mlstm(q, k, v, o, gate_rows, gate_bias, batch, seq, heads, dh):
    t = q.shape[0]
    L = MLSTM_CHUNK
    nh = MLSTM_HEADS_PER_STEP
    nc = seq // L
    col = pl.BlockSpec((seq, nh * dh), lambda b, h: (b, h))
    return pl.pallas_call(
        functools.partial(_mlstm_kernel, L=L, seq=seq, dh=dh, nh=nh),
        grid=(batch, heads // nh),
        in_specs=[pl.BlockSpec((1, nh, 2, 1, 1), lambda b, h: (h, 0, 0, 0, 0)),
                  pl.BlockSpec((1, 1, nh, 2, nc, L), lambda b, h: (b, h, 0, 0, 0, 0)),
                  col, col, col, col],
        out_specs=col,
        out_shape=jax.ShapeDtypeStruct((t, heads * dh), BF16),
        scratch_shapes=[pltpu.VMEM((nh, dh, 2 * dh), F32), pltpu.VMEM((nh, nc, L), F32),
                        pltpu.VMEM((nh, nc, L), F32)],
        compiler_params=_cparams("parallel", "parallel"),
        name="mlstm",
    )(gate_bias, gate_rows, q, k, v, o)


def _out_even_kernel(h_ref, att_ref, ys4_ref, wg_ref, bg_ref, wo_ref, g_ref, b_ref, o_ref, ys_ref, *, alpha):
    L = S5_CHUNK
    nq = ys4_ref.shape[0]
    nchunk = ys_ref.shape[1] // L
    for q in range(nq):
        for t in range(L):
            ys_ref[q, pl.ds(t, nchunk, stride=L), :] = ys4_ref[q, :, t * LANES:(t + 1) * LANES]
    ys = jnp.concatenate([ys_ref[q] for q in range(nq)], axis=1)
    half = att_ref.shape[1]
    gate = jax.nn.sigmoid(jnp.dot(ys.astype(BF16), wg_ref[...], preferred_element_type=F32) + bg_ref[...])
    mix = jnp.dot(att_ref[...], wo_ref[:half, :], preferred_element_type=F32)
    mix = mix + jnp.dot((ys * gate).astype(BF16), wo_ref[half:, :], preferred_element_type=F32)
    o_ref[...] = _layer_norm(alpha * h_ref[...] + mix, g_ref[...], b_ref[...])


def _out_odd_kernel(h_ref, hm_ref, wo_ref, g_ref, b_ref, o_ref, *, alpha):
    mix = jnp.dot(hm_ref[...], wo_ref[...], preferred_element_type=F32)
    o_ref[...] = _layer_norm(alpha * h_ref[...] + mix, g_ref[...], b_ref[...])


def _row_tiled_call(kern, row_args, fixed_args, out_dtype, name):
    t = row_args[0].shape[0]
    tm = TM_PROJ
    in_specs = [pl.BlockSpec((tm, a.shape[1]), lambda i: (i, 0)) for a in row_args]
    in_specs += [pl.BlockSpec(a.shape, lambda i: (0, 0)) for a in fixed_args]
    d = row_args[0].shape[1]
    return pl.pallas_call(
        kern,
        grid=(t // tm,),
        in_specs=in_specs,
        out_specs=pl.BlockSpec((tm, d), lambda i: (i, 0)),
        out_shape=jax.ShapeDtypeStruct((t, d), out_dtype),
        compiler_params=_cparams("parallel"),
        name=name,
    )(*row_args, *fixed_args)


def _router_kernel(h_ref, w_ref, b_ref, o_ref, cnt_ref, run_ref, w2_ref, *, n_groups, epg):
    @pl.when(pl.program_id(0) == 0)
    def _():
        run_ref[...] = jnp.zeros_like(run_ref)
        w = w_ref[...]
        w_hi = w.astype(BF16)
        w2_ref[:, :LANES] = w_hi
        w2_ref[:, LANES:] = (w - w_hi.astype(F32)).astype(BF16)

    h = h_ref[...]
    h_hi = h.astype(BF16)
    h_lo = (h - h_hi.astype(F32)).astype(BF16)
    part = jnp.dot(h_hi, w2_ref[...], preferred_element_type=F32)
    logits = (part[:, :LANES] + (part[:, LANES:] + jnp.dot(h_lo, w2_ref[:, :LANES], preferred_element_type=F32))
              + b_ref[...])
    lane = _iota(logits.shape, 1)
    big = jnp.int32(LANES)
    lg = jnp.where(lane < n_groups, logits, NEG_INF)
    mg = jnp.max(lg, axis=1, keepdims=True)
    g_val = 1.0 / jnp.sum(jnp.exp(lg - mg), axis=1, keepdims=True)
    g_idx = jnp.min(jnp.where(lg == mg, lane, big), axis=1, keepdims=True)
    lo_lane = n_groups + g_idx * epg
    le = jnp.where((lane >= lo_lane) & (lane < lo_lane + epg), logits, NEG_INF)
    m1 = jnp.max(le, axis=1, keepdims=True)
    i1 = jnp.min(jnp.where(le == m1, lane, big), axis=1, keepdims=True)
    le2 = jnp.where(lane == i1, NEG_INF, le)
    m2 = jnp.max(le2, axis=1, keepdims=True)
    i2 = jnp.min(jnp.where(le2 == m2, lane, big), axis=1, keepdims=True)
    r = jnp.exp(m2 - m1)
    w1 = g_val / (1.0 + r)
    w2 = g_val * r / (1.0 + r)
    e1 = i1 - lo_lane
    e2 = i2 - lo_lane
    first_lo = e1 < e2
    lo = jnp.where(first_lo, e1, e2)
    hi = jnp.where(first_lo, e2, e1)
    w_lo = jnp.where(first_lo, w1, w2)
    w_hi = jnp.where(first_lo, w2, w1)
    pair = (lo * (2 * epg - 1 - lo)) // 2 + (hi - lo - 1)
    cls = g_idx * (epg * (epg - 1) // 2) + pair
    tm = h.shape[0]
    onehot = lane == cls
    earlier = (_iota((tm, tm), 1) < _iota((tm, tm), 0)).astype(BF16)
    before = jnp.dot(earlier, onehot.astype(BF16), preferred_element_type=F32) + run_ref[...]
    rank = jnp.sum(jnp.where(onehot, before, 0.0), axis=1, keepdims=True)
    run = run_ref[...] + jnp.sum(onehot.astype(F32), axis=0, keepdims=True)
    run_ref[...] = run
    cnt_ref[...] = run
    out = jnp.where(lane == 0, cls.astype(F32),
                    jnp.where(lane == 1, w_lo, jnp.where(lane == 2, w_hi, jnp.where(lane == 3, rank, 0.0))))
    o_ref[...] = out


def _router(h, w_r, b_r, n_groups, epg):
    t, d = h.shape
    tm = TM_PROJ
    return pl.pallas_call(
        functools.partial(_router_kernel, n_groups=n_groups, epg=epg),
        grid=(t // tm,),
        in_specs=[pl.BlockSpec((tm, d), lambda i: (i, 0)),
                  pl.BlockSpec(w_r.shape, lambda i: (0, 0)),
                  pl.BlockSpec(b_r.shape, lambda i: (0, 0))],
        out_specs=[pl.BlockSpec((tm, LANES), lambda i: (i, 0)), pl.BlockSpec((1, LANES), lambda i: (0, 0))],
        out_shape=[jax.ShapeDtypeStruct((t, LANES), F32), jax.ShapeDtypeStruct((1, LANES), F32)],
        scratch_shapes=[pltpu.VMEM((1, LANES), F32), pltpu.VMEM((d, 2 * LANES), BF16)],
        compiler_params=_cparams("arbitrary"),
        name="router",
    )(h, w_r, b_r)


def _rows_wait(buf, sem):
    pltpu.make_async_copy(buf, buf, sem).wait()


def _dispatch_kernel(pos_ref, cnt_ref, pend_ref, x_ref, r_ref, o_hbm, xbuf, sem, zsem, *, tm, nsteps, ztile):
    i = pl.program_id(0)
    slot = i % 2
    d = x_ref.shape[1]

    @pl.when(i == 0)
    def _():
        zeros = xbuf.at[1, pl.ds(0, ztile)]
        xbuf[1] = jnp.zeros(xbuf.shape[1:], F32)
        ncls = cnt_ref.shape[0]
        used = pend_ref[ncls - 1]
        firsts = [(cnt_ref[c] > 0, pend_ref[c] - ztile) for c in range(ncls)]
        firsts += [(used + k * ztile < o_hbm.shape[0], used + k * ztile) for k in range(ncls)]
        for cond, first in firsts:
            @pl.when(cond)
            def _():
                pltpu.make_async_copy(zeros, o_hbm.at[pl.ds(pl.multiple_of(first, ztile), ztile)], zsem).start()
        for cond, _ in firsts:
            @pl.when(cond)
            def _():
                pltpu.make_async_copy(zeros, o_hbm.at[pl.ds(0, ztile)], zsem).wait()

    @pl.when(i >= 2)
    def _():
        _rows_wait(xbuf.at[slot], sem.at[slot])

    xbuf[slot, :, :d] = x_ref[...]
    xbuf[slot, :, d:] = r_ref[...]

    def body(g, c):
        for j in range(8):
            r = g * 8 + j
            pltpu.make_async_copy(xbuf.at[slot, pl.ds(r, 1)], o_hbm.at[pl.ds(pos_ref[0, 0, r], 1)],
                                  sem.at[slot]).start(priority=j % 2)
        return c
    lax.fori_loop(0, tm // 8, body, 0)

    @pl.when(i == nsteps - 1)
    def _():
        _rows_wait(xbuf.at[slot], sem.at[slot])
        if nsteps > 1:
            _rows_wait(xbuf.at[1 - slot], sem.at[1 - slot])


def _dispatch(h, route, pos3, counts, pend, p_rows):
    t, d = h.shape
    w = d + route.shape[1]
    tm = pos3.shape[2]
    assert TM_MOE <= tm
    nsteps = t // tm
    smem = pl.BlockSpec(memory_space=pltpu.SMEM)
    return pl.pallas_call(
        functools.partial(_dispatch_kernel, tm=tm, nsteps=nsteps, ztile=TM_MOE),
        grid=(nsteps,),
        in_specs=[pl.BlockSpec((1, 1, tm), lambda i: (i, 0, 0), memory_space=pltpu.SMEM), smem, smem,
                  pl.BlockSpec((tm, d), lambda i: (i, 0)), pl.BlockSpec((tm, route.shape[1]), lambda i: (i, 0))],
        out_specs=pl.BlockSpec(memory_space=pl.ANY),
        out_shape=jax.ShapeDtypeStruct((p_rows, w), F32),
        scratch_shapes=[pltpu.VMEM((2, tm, w), F32), pltpu.SemaphoreType.DMA((2,)), pltpu.SemaphoreType.DMA(())],
        compiler_params=_cparams("arbitrary"),
        name="moe_dispatch",
    )(pos3, counts, pend, h, route)


def _combine_kernel(pos_ref, nxt_ref, h_ref, g_ref, b_ref, y_hbm, o_ref, ybuf, sem, *, alpha, tm, nsteps):
    i = pl.program_id(0)
    slot = i % 2

    def gather(p_ref, s):
        def body(g, c):
            for j in range(8):
                r = g * 8 + j
                pltpu.make_async_copy(y_hbm.at[pl.ds(p_ref[0, 0, r], 1)], ybuf.at[s, pl.ds(r, 1)],
                                      sem.at[s]).start(priority=j % 2)
            return c
        lax.fori_loop(0, tm // 8, body, 0)

    @pl.when(i == 0)
    def _():
        gather(pos_ref, 0)

    @pl.when(i + 1 < nsteps)
    def _():
        gather(nxt_ref, 1 - slot)

    _rows_wait(ybuf.at[slot], sem.at[slot])
    o_ref[...] = _layer_norm(alpha * h_ref[...] + ybuf[slot], g_ref[...], b_ref[...])


def _combine(h, pos3, y_sorted, ln_g, ln_b, alpha):
    t, d = h.shape
    tm = pos3.shape[2]
    nsteps = t // tm
    row = lambda i: (i, 0)
    fixed = lambda i: (0, 0)
    return pl.pallas_call(
        functools.partial(_combine_kernel, alpha=alpha, tm=tm, nsteps=nsteps),
        grid=(nsteps,),
        in_specs=[pl.BlockSpec((1, 1, tm), lambda i: (i, 0, 0), memory_space=pltpu.SMEM),
                  pl.BlockSpec((1, 1, tm), lambda i: (jnp.minimum(i + 1, nsteps - 1), 0, 0), memory_space=pltpu.SMEM),
                  pl.BlockSpec((tm, d), row),
                  pl.BlockSpec((1, d), fixed), pl.BlockSpec((1, d), fixed),
                  pl.BlockSpec(memory_space=pl.ANY)],
        out_specs=pl.BlockSpec((tm, d), row),
        out_shape=jax.ShapeDtypeStruct((t, d), F32),
        scratch_shapes=[pltpu.VMEM((2, tm, d), F32), pltpu.SemaphoreType.DMA((2,))],
        compiler_params=_cparams("arbitrary"),
        name="moe_combine_ln",
    )(pos3, pos3, h, ln_g.reshape(1, -1), ln_b.reshape(1, -1), y_sorted)


def _moe_kernel(elo_ref, ehi_ref, nlive_ref, x_ref, g0_ref, u0_ref, d0_ref, g1_ref, u1_ref, d1_ref, y_ref,
                wg_ref, wu_ref, wd_ref):
    i = pl.program_id(0)
    d = y_ref.shape[1]
    live = i < nlive_ref[0]
    prev = jnp.maximum(i - 1, 0)
    experts = ((elo_ref, g0_ref, u0_ref, d0_ref), (ehi_ref, g1_ref, u1_ref, d1_ref))

    for j, (e_ref, g_ref, u_ref, d_ref) in enumerate(experts):
        @pl.when(live & ((i == 0) | (e_ref[i] != e_ref[prev])))
        def _():
            wg_ref[j] = g_ref[0].astype(BF16)
            wu_ref[j] = u_ref[0].astype(BF16)
            wd_ref[j] = d_ref[0].astype(BF16)

    @pl.when(jnp.logical_not(live))
    def _():
        y_ref[...] = jnp.zeros_like(y_ref)

    @pl.when(live)
    def _():
        x = x_ref[:, :d].astype(BF16)
        route = x_ref[:, d:]
        y = None
        for j in range(2):
            gate = jnp.dot(x, wg_ref[j], preferred_element_type=F32)
            up = jnp.dot(x, wu_ref[j], preferred_element_type=F32)
            hid = gate * jax.nn.sigmoid(gate) * up * route[:, j + 1:j + 2]
            part = jnp.dot(hid.astype(BF16), wd_ref[j], preferred_element_type=F32)
            y = part if y is None else y + part
        y_ref[...] = y


def _moe_experts(x_sorted, e_lo, e_hi, n_live, w_gate, w_up, w_down):
    p, xw = x_sorted.shape
    d = w_gate.shape[1]
    tm = TM_MOE
    f = w_gate.shape[2]
    lo_in = pl.BlockSpec((1, d, f), lambda i, lo, hi, nl: (lo[i], 0, 0))
    hi_in = pl.BlockSpec((1, d, f), lambda i, lo, hi, nl: (hi[i], 0, 0))
    lo_dn = pl.BlockSpec((1, f, d), lambda i, lo, hi, nl: (lo[i], 0, 0))
    hi_dn = pl.BlockSpec((1, f, d), lambda i, lo, hi, nl: (hi[i], 0, 0))
    grid_spec = pltpu.PrefetchScalarGridSpec(
        num_scalar_prefetch=3,
        grid=(p // tm,),
        in_specs=[pl.BlockSpec((tm, xw), lambda i, lo, hi, nl: (jnp.minimum(i, nl[0] - 1), 0)),
                  lo_in, lo_in, lo_dn, hi_in, hi_in, hi_dn],
        out_specs=pl.BlockSpec((tm, d), lambda i, lo, hi, nl: (i, 0)),
        scratch_shapes=[pltpu.VMEM((2, d, f), BF16), pltpu.VMEM((2, d, f), BF16), pltpu.VMEM((2, f, d), BF16)],
    )
    return pl.pallas_call(
        _moe_kernel,
        grid_spec=grid_spec,
        out_shape=jax.ShapeDtypeStruct((p, d), F32),
        compiler_params=_cparams("arbitrary"),
        name="moe_experts",
    )(e_lo, e_hi, n_live, x_sorted, w_gate, w_up, w_down, w_gate, w_up, w_down)


def _hier_moe_ln(h, ln_g, ln_b, alpha, w_group, b_group, w_expert, b_expert, w_gate, w_up, w_down):
    t, d = h.shape
    n_groups, _, epg = w_expert.shape
    npairs = epg * (epg - 1) // 2
    ncls = n_groups * npairs
    tm = TM_MOE
    w_r = jnp.concatenate([w_group, w_expert.transpose(1, 0, 2).reshape(d, n_groups * epg)], axis=1)
    b_r = jnp.concatenate([b_group, b_expert.reshape(-1)])
    pad = LANES - w_r.shape[1]
    w_r = jnp.pad(w_r, ((0, 0), (0, pad)))
    b_r = jnp.pad(b_r, (0, pad)).reshape(1, LANES)
    route, cnt = _router(h, w_r, b_r, n_groups, epg)

    cls = route[:, 0].astype(jnp.int32)
    rank = route[:, 3].astype(jnp.int32)
    counts = cnt[0, :ncls].astype(jnp.int32)
    padded = ((counts + tm - 1) // tm) * tm
    pend = jnp.cumsum(padded)
    pstart = pend - padded
    pos = jnp.sum(jnp.where(cls[:, None] == jnp.arange(ncls)[None, :], pstart[None, :], 0), axis=1) + rank
    pos3 = pos.astype(jnp.int32).reshape(t // TM_PROJ, 1, TM_PROJ)
    p_rows = t + ncls * tm
    x_sorted = _dispatch(h, route, pos3, counts, pend.astype(jnp.int32), p_rows)

    tile_start = jnp.arange(p_rows // tm, dtype=jnp.int32) * tm
    tile_start = jnp.minimum(tile_start, pend[-1] - tm)
    tile_cls = jnp.sum(pend[None, :] <= tile_start[:, None], axis=1).astype(jnp.int32)
    n_live = (pend[-1:] // tm).astype(jnp.int32)
    pair = tile_cls % npairs
    pair_lo = sum(jnp.where(pair == k, v, 0) for k, v in enumerate(PAIRS_LO))
    pair_hi = sum(jnp.where(pair == k, v, 0) for k, v in enumerate(PAIRS_HI))
    e_lo = ((tile_cls // npairs) * epg + pair_lo).astype(jnp.int32)
    e_hi = ((tile_cls // npairs) * epg + pair_hi).astype(jnp.int32)

    y_sorted = _moe_experts(x_sorted, e_lo, e_hi, n_live, w_gate, w_up, w_down)
    return _combine(h, pos3, y_sorted, ln_g, ln_b, alpha)


def _even_mixer(h, batch, seq, alpha, ln_g, ln_b, w_in, f_bias, s5_params, w_glu, b_glu, w_out):
    t, d = h.shape
    heads = f_bias.shape[0]
    groups, p_state = s5_params[0].shape
    hc = s5_params[3].shape[-1]
    s5_width = groups * hc
    fox_width = d - s5_width
    dh = fox_width // heads
    q_scale = dh ** -0.5
    n_qkv = 3 * fox_width
    col_scale = jnp.where(jnp.arange(n_qkv) < fox_width, q_scale, 1.0).astype(F32)
    w_qkv = (w_in[:, :n_qkv] * col_scale[None, :]).astype(BF16)
    w_u = w_in[:, n_qkv + heads:].astype(BF16)
    hp = heads // 2
    w_f = w_in[:, n_qkv:n_qkv + heads]
    w_f = jnp.pad(w_f.reshape(d, hp, 2), ((0, 0), (0, 0), (0, LANES - 2))).reshape(d, hp * LANES).astype(BF16)
    fb = jnp.pad(f_bias.reshape(hp, 1, 2), ((0, 0), (0, 0), (0, LANES - 2)))
    qkv, u4, f_cols = _proj_even(h, w_qkv, w_u, w_f)
    att = _fox_attention(qkv, f_cols, fb, batch, seq, heads, dh)
    ys4 = _s5(u4, _s5_tables(*s5_params), batch, seq)
    tm = TM_PROJ
    row = lambda i: (i, 0)
    fixed = lambda i: (0, 0)
    fixed_args = (w_glu.astype(BF16), b_glu.reshape(1, -1), w_out.astype(BF16), ln_g.reshape(1, -1), ln_b.reshape(1, -1))
    return pl.pallas_call(
        functools.partial(_out_even_kernel, alpha=alpha),
        grid=(t // tm,),
        in_specs=[pl.BlockSpec((tm, d), row), pl.BlockSpec((tm, fox_width), row),
                  pl.BlockSpec((ys4.shape[0], tm // S5_CHUNK, ys4.shape[2]), lambda i: (0, i, 0))]
        + [pl.BlockSpec(a.shape, fixed) for a in fixed_args],
        out_specs=pl.BlockSpec((tm, d), row),
        out_shape=jax.ShapeDtypeStruct((t, d), F32),
        scratch_shapes=[pltpu.VMEM((ys4.shape[0], tm, LANES), F32)],
        compiler_params=_cparams("parallel"),
        name="out_even",
    )(h, att, ys4, *fixed_args)


def _odd_mixer(h, batch, seq, alpha, ln_g, ln_b, w_in, conv_w, conv_b, i_bias, f_bias, w_out):
    t, d = h.shape
    heads = i_bias.shape[0]
    dmix = conv_w.shape[1] // 2
    dh = dmix // heads
    q, k, v, o, gates = _proj_odd(h, w_in.astype(BF16), conv_w, conv_b, seq, dh)
    nh = MLSTM_HEADS_PER_STEP
    g = gates[:, :2 * heads].reshape(batch, seq, 2, heads).transpose(0, 3, 2, 1)
    gate_rows = g.reshape(batch, heads // nh, nh, 2, seq // MLSTM_CHUNK, MLSTM_CHUNK)
    gate_bias = jnp.stack([i_bias, f_bias], axis=1).reshape(heads // nh, nh, 2, 1, 1)
    hm = _mlstm(q, k, v, o, gate_rows, gate_bias, batch, seq, heads, dh)
    kern = functools.partial(_out_odd_kernel, alpha=alpha)
    return _row_tiled_call(kern, (h, hm), (w_out.astype(BF16), ln_g.reshape(1, -1), ln_b.reshape(1, -1)),
                           F32, "out_odd")


def kernel(x, ln_g, ln_b, even_w_in, fox_f_bias, s5_a_re, s5_a_im, s5_log_dt, s5_b_re, s5_b_im, s5_c_re, s5_c_im, s5_d, s5_w_glu, s5_b_glu, even_w_out, odd_w_in, mlstm_conv_w, mlstm_conv_b, mlstm_i_bias, mlstm_f_bias, odd_w_out, moe_w_group, moe_b_group, moe_w_expert, moe_b_expert, moe_w_gate, moe_w_up, moe_w_down):
    batch, seq, d = x.shape
    depth = ln_g.shape[0]
    alpha = (2 * depth) ** 0.25
    h = x.reshape(batch * seq, d)
    for layer in range(depth):
        j = layer // 2
        if layer % 2 == 0:
            s5_params = (s5_a_re[j], s5_a_im[j], s5_log_dt[j], s5_b_re[j], s5_b_im[j],
                         s5_c_re[j], s5_c_im[j], s5_d[j])
            h = _even_mixer(h, batch, seq, alpha, ln_g[layer, 0], ln_b[layer, 0], even_w_in[j], fox_f_bias[j],
                            s5_params, s5_w_glu[j], s5_b_glu[j], even_w_out[j])
        else:
            h = _odd_mixer(h, batch, seq, alpha, ln_g[layer, 0], ln_b[layer, 0], odd_w_in[j], mlstm_conv_w[j],
                           mlstm_conv_b[j], mlstm_i_bias[j], mlstm_f_bias[j], odd_w_out[j])
        h = _hier_moe_ln(h, ln_g[layer, 1], ln_b[layer, 1], alpha, moe_w_group[layer], moe_b_group[layer],
                         moe_w_expert[layer], moe_b_expert[layer], moe_w_gate[layer], moe_w_up[layer],
                         moe_w_down[layer])
    return h.reshape(batch, seq, d)
```
